```python
import math
import jax
import jax.numpy as jnp
from jax import lax
import numpy as np

D_MODEL = 1024
BATCH = 16
SEQ = 256
DEPTH = 2
DEC_BATCH = 8
DEC_SEQ = 4096
PAST_LEN = 256

GRID_W = 64
BLOCK = 128
N_AB = (DEPTH + 1) // 2
N_CD = DEPTH // 2
A_HEADS = 8
A_KV = 2
A_HD = 64
A_WIN = 128
B_HEADS = 4
B_HD = 64
B_VD = 2 * B_HD
C_HEADS = 4
C_DK = 128
C_DV = 128
D_HEADS = 8
D_KV = 2
D_HD = 64
A_W = A_HEADS * A_HD
B_W = B_HEADS * B_VD
C_W = C_HEADS * C_DV
D_W = D_HEADS * D_HD
AB_SPLIT = (A_HEADS * A_HD, A_KV * A_HD, A_KV * A_HD, B_HEADS * 2 * B_HD, B_HEADS * 2 * B_HD, B_HEADS * B_VD, A_W, B_W)
CD_SPLIT = (C_HEADS * C_DK, C_HEADS * C_DK, C_HEADS * C_DV, D_HEADS * D_HD, D_KV * D_HD, D_KV * D_HD, C_W, D_W)
P_AB = sum(AB_SPLIT)
P_CD = sum(CD_SPLIT)
ROPE_THETA = 10000.0
EPS = 1e-6
NEG = -1e30
F32 = jnp.float32

kernel_name = 'hybrid_diffusion_prefix_step'


def rms_norm(x, g):
    xf = x.astype(F32)
    y = xf * lax.rsqrt(jnp.mean(xf * xf, axis=-1, keepdims=True) + EPS)
    return (y * g.astype(F32)).astype(x.dtype)


def adaln(x, cond, g, w, b):
    m = jax.nn.silu(cond) @ w + b
    shift, scale, gate = jnp.split(m[:, None, :], 3, axis=-1)
    return rms_norm(x, g) * (1 + scale) + shift, gate


def split_cols(z, sizes):
    return jnp.split(z, [int(s) for s in np.cumsum(sizes)[:-1]], axis=-1)


def heads(t, h):
    bsz, n, w = t.shape
    return t.reshape(bsz, n, h, w // h).transpose(0, 2, 1, 3)


def gqa_heads(t, g, h):
    bsz, n, w = t.shape
    return t.reshape(bsz, n, g, h // g, w // h).transpose(0, 2, 3, 1, 4)


def merge_heads(o):
    bsz, h, n, d = o.shape
    return o.transpose(0, 2, 1, 3).reshape(bsz, n, h * d)


def merge_gqa(o):
    bsz, g, r, n, d = o.shape
    return o.transpose(0, 3, 1, 2, 4).reshape(bsz, n, g * r * d)


def grid_rope(n, hd):
    rows = n // GRID_W
    row = jnp.repeat(jnp.arange(rows, dtype=F32), GRID_W)
    col = jnp.tile(jnp.arange(GRID_W, dtype=F32), rows)
    nf = hd // 4
    inv = ROPE_THETA ** (-jnp.arange(nf, dtype=F32) / nf)
    ang = jnp.concatenate([row[:, None] * inv, col[:, None] * inv], axis=-1)
    return jnp.cos(ang), jnp.sin(ang)


def rope(x, cs):
    cos, sin = cs
    x = x.astype(F32)
    half = x.shape[-1] // 2
    x1, x2 = x[..., :half], x[..., half:]
    return jnp.concatenate([x1 * cos - x2 * sin, x2 * cos + x1 * sin], axis=-1)


def to_blocks(t):
    *lead, n, d = t.shape
    return jnp.moveaxis(t.reshape(*lead, n // BLOCK, BLOCK, d), -3, 0)


def from_blocks(o):
    o = jnp.moveaxis(o, 0, -3)
    *lead, nb, bl, d = o.shape
    return o.reshape(*lead, nb * bl, d)


def softmax_sink(s, sink):
    if sink is None:
        return jax.nn.softmax(s, axis=-1)
    sk = jnp.broadcast_to(sink.astype(F32)[None, :, :, None, None], s.shape[:-1] + (1,))
    return jax.nn.softmax(jnp.concatenate([s, sk], axis=-1), axis=-1)[..., :-1]


def gqa_attention(q, k, v, sink):
    scale = q.shape[-1] ** -0.5

    def one(qi):
        s = jnp.einsum('bgrqd,bgkd->bgrqk', qi, k) * scale
        p = softmax_sink(s, sink)
        return jnp.einsum('bgrqk,bgkd->bgrqd', p, v)

    return from_blocks(lax.map(one, to_blocks(q)))


def window_attention(q, k, v, k_ctx, v_ctx, sink):
    n = q.shape[-2]
    scale = q.shape[-1] ** -0.5
    pad = ((0, 0), (0, 0), (BLOCK, BLOCK), (0, 0))
    kp, vp = jnp.pad(k, pad), jnp.pad(v, pad)
    r = jnp.arange(BLOCK)[:, None]
    m = jnp.arange(3 * BLOCK)[None, :]
    band = (m - r >= BLOCK - A_WIN) & (m - r <= BLOCK + A_WIN)

    def one(args):
        qi, b = args
        kw = lax.dynamic_slice_in_dim(kp, b * BLOCK, 3 * BLOCK, axis=2)
        vw = lax.dynamic_slice_in_dim(vp, b * BLOCK, 3 * BLOCK, axis=2)
        j = (b - 1) * BLOCK + m
        valid = band & (j >= 0) & (j < n)
        s_w = jnp.where(valid, jnp.einsum('bgrqd,bgkd->bgrqk', qi, kw) * scale, NEG)
        s_c = jnp.einsum('bgrqd,bgkd->bgrqk', qi, k_ctx) * scale
        p = softmax_sink(jnp.concatenate([s_w, s_c], axis=-1), sink)
        return (jnp.einsum('bgrqk,bgkd->bgrqd', p[..., :3 * BLOCK], vw)
                + jnp.einsum('bgrqk,bgkd->bgrqd', p[..., 3 * BLOCK:], v_ctx))

    return from_blocks(lax.map(one, (to_blocks(q), jnp.arange(n // BLOCK))))


def diff_attention(q1, q2, k1, k2, v, lam):
    scale = q1.shape[-1] ** -0.5

    def one(args):
        a, b = args
        p1 = jax.nn.softmax(jnp.einsum('bhqd,bhkd->bhqk', a, k1) * scale, axis=-1)
        p2 = jax.nn.softmax(jnp.einsum('bhqd,bhkd->bhqk', b, k2) * scale, axis=-1)
        return jnp.einsum('bhqk,bhke->bhqe', p1 - lam * p2, v)

    return from_blocks(lax.map(one, (to_blocks(q1), to_blocks(q2))))


def retention_chunks(q, k, v, lg, s0, inclusive):
    inc = 1 if inclusive else 0
    idx = jnp.arange(BLOCK, dtype=F32)
    rel = idx[:, None] - idx[None, :] - (1 - inc)
    mask = rel >= 0
    dmat = jnp.where(mask, jnp.exp(lg[:, None, None] * jnp.where(mask, rel, 0.0)), 0.0)
    cross = jnp.exp(lg[:, None] * (idx + inc))[None, :, :, None]
    kdec = jnp.exp(lg[:, None] * (BLOCK - 1 - idx))[None, :, :, None]
    cdec = jnp.exp(lg * BLOCK)[None, :, None, None]

    def step(state, xs):
        qc, kc, vc = xs
        att = jnp.einsum('bhid,bhjd->bhij', qc, kc) * dmat
        o = jnp.einsum('bhij,bhje->bhie', att, vc) + cross * jnp.einsum('bhid,bhde->bhie', qc, state)
        state = cdec * state + jnp.einsum('bhjd,bhje->bhde', kc * kdec, vc)
        return state, o

    s, o = lax.scan(step, s0, (to_blocks(q), to_blocks(k), to_blocks(v)))
    return from_blocks(o), s


def retention_bidir(q, k, v, lg_f, lg_b, s_f, s_b):
    o_f, s_f = retention_chunks(q, k, v, lg_f, s_f, True)
    o_b, s_b = retention_chunks(q[:, :, ::-1], k[:, :, ::-1], v[:, :, ::-1], lg_b, s_b, False)
    return o_f + o_b[:, :, ::-1], s_f, s_b


def head_group_norm(o, g):
    mu = jnp.mean(o, axis=-1, keepdims=True)
    var = jnp.mean(jnp.square(o - mu), axis=-1, keepdims=True)
    return (o - mu) * lax.rsqrt(var + EPS) * g.astype(F32)[None, :, None, :]


def decay(p):
    return -jnp.exp(p.astype(F32))


def diff_lambda(lq1, lk1, lq2, lk2, lam_init):
    e = lambda a, b: jnp.exp(jnp.sum(a.astype(F32) * b.astype(F32)))
    return e(lq1, lk1) - e(lq2, lk2) + lam_init


def gated_out(o1, o2, g1, g2, w_out):
    dt = g1.dtype
    y = jnp.concatenate([o1.astype(dt) * jax.nn.silu(g1), o2.astype(dt) * jax.nn.silu(g2)], axis=-1)
    return y @ w_out


def ab_project(h, w_in):
    qa, ka, va, qb, kb, vb, ga, gb = split_cols(h @ w_in, AB_SPLIT)
    return (gqa_heads(qa, A_KV, A_HEADS).astype(F32), heads(ka, A_KV), heads(va, A_KV),
            heads(qb, B_HEADS).astype(F32), heads(kb, B_HEADS), heads(vb, B_HEADS), ga, gb)


def ab_merge(oa, ob, ga, gb, w_out, b_norm_g, lam_init):
    ob = rms_norm(ob, b_norm_g) * (1.0 - lam_init)
    return gated_out(merge_gqa(oa), merge_heads(ob), ga, gb, w_out)


def ab_context(h, w_in, w_out, sink, lq1, lk1, lq2, lk2, b_norm_g, lam_init):
    qa, ka, va, qb, kb, vb, ga, gb = ab_project(h, w_in)
    oa = gqa_attention(qa, ka.astype(F32), va.astype(F32), sink.reshape(A_KV, A_HEADS // A_KV))
    lam = diff_lambda(lq1, lk1, lq2, lk2, lam_init)
    kbf = kb.astype(F32)
    ob = diff_attention(qb[..., :B_HD], qb[..., B_HD:], kbf[..., :B_HD], kbf[..., B_HD:], vb.astype(F32), lam)
    return ab_merge(oa, ob, ga, gb, w_out, b_norm_g, lam_init), ka, va, kb, vb


def ab_latent(h, ka_c, va_c, kb_c, vb_c, w_in, w_out, sink, lq1, lk1, lq2, lk2, b_norm_g, lam_init):
    qa, ka, va, qb, kb, vb, ga, gb = ab_project(h, w_in)
    n = h.shape[1]
    cs_a = grid_rope(n, A_HD)
    cs_b = grid_rope(n, B_HD)
    oa = window_attention(rope(qa, cs_a), rope(ka, cs_a), va.astype(F32), ka_c.astype(F32), va_c.astype(F32),
                          sink.reshape(A_KV, A_HEADS // A_KV))
    lam = diff_lambda(lq1, lk1, lq2, lk2, lam_init)
    kbc = kb_c.astype(F32)
    k1 = jnp.concatenate([rope(kb[..., :B_HD], cs_b), kbc[..., :B_HD]], axis=2)
    k2 = jnp.concatenate([rope(kb[..., B_HD:], cs_b), kbc[..., B_HD:]], axis=2)
    v = jnp.concatenate([vb.astype(F32), vb_c.astype(F32)], axis=2)
    ob = diff_attention(rope(qb[..., :B_HD], cs_b), rope(qb[..., B_HD:], cs_b), k1, k2, v, lam)
    return ab_merge(oa, ob, ga, gb, w_out, b_norm_g, lam_init)


def cd_project(h, w_in, dqg, dkg):
    qc, kc, vc, qd, kd, vd, gc, gd = split_cols(h @ w_in, CD_SPLIT)
    qc = heads(qc, C_HEADS).astype(F32)
    kc = heads(kc, C_HEADS).astype(F32) * (C_DK ** -0.5)
    vc = heads(vc, C_HEADS).astype(F32)
    qd = rms_norm(gqa_heads(qd, D_KV, D_HEADS).astype(F32), dqg)
    kd = rms_norm(heads(kd, D_KV), dkg)
    vd = heads(vd, D_KV)
    return qc, kc, vc, qd, kd, vd, gc, gd


def cd_merge(oc, od, gc, gd, w_out, c_norm_g):
    return gated_out(merge_heads(head_group_norm(oc, c_norm_g)), merge_gqa(od), gc, gd, w_out)


def cd_context(h, w_in, w_out, dec_f, dec_b, c_norm_g, dqg, dkg):
    qc, kc, vc, qd, kd, vd, gc, gd = cd_project(h, w_in, dqg, dkg)
    zero = jnp.zeros(qc.shape[:2] + (C_DK, C_DV), F32)
    oc, s_f, s_b = retention_bidir(qc, kc, vc, decay(dec_f), decay(dec_b), zero, zero)
    od = gqa_attention(qd, kd.astype(F32), vd.astype(F32), None)
    dt = h.dtype
    return cd_merge(oc, od, gc, gd, w_out, c_norm_g), s_f.astype(dt), s_b.astype(dt), kd, vd


def cd_latent(h, s_f, s_b, kd_c, vd_c, w_in, w_out, dec_f, dec_b, c_norm_g, dqg, dkg):
    qc, kc, vc, qd, kd, vd, gc, gd = cd_project(h, w_in, dqg, dkg)
    oc, _, _ = retention_bidir(qc, kc, vc, decay(dec_f), decay(dec_b), s_f.astype(F32), s_b.astype(F32))
    cs = grid_rope(h.shape[1], D_HD)
    k = jnp.concatenate([rope(kd, cs), kd_c.astype(F32)], axis=2)
    v = jnp.concatenate([vd.astype(F32), vd_c.astype(F32)], axis=2)
    od = gqa_attention(rope(qd, cs), k, v, None)
    return cd_merge(oc, od, gc, gd, w_out, c_norm_g)


def setup_inputs(seed: int = 0) -> dict:
    key = jax.random.key(seed)
    ks = iter(jax.random.split(key, 48))

    def nrm(shape, scale=1.0):
        return jax.random.normal(next(ks), shape, F32) * scale

    base = jnp.log(-jnp.log1p(-(2.0 ** (-5.0 - jnp.arange(C_HEADS, dtype=F32)))))
    return {
        'x_prompt': nrm((BATCH, SEQ, D_MODEL)),
        'x_sample': nrm((DEC_BATCH, DEC_SEQ, D_MODEL)),
        'cache_a_k': nrm((DEC_BATCH, N_AB, A_KV, PAST_LEN, A_HD)),
        'cache_a_v': nrm((DEC_BATCH, N_AB, A_KV, PAST_LEN, A_HD)),
        'cache_b_k': nrm((DEC_BATCH, N_AB, B_HEADS, PAST_LEN, 2 * B_HD)),
        'cache_b_v': nrm((DEC_BATCH, N_AB, B_HEADS, PAST_LEN, B_VD)),
        'state_c_fwd': nrm((DEC_BATCH, N_CD, C_HEADS, C_DK, C_DV), 0.5),
        'state_c_bwd': nrm((DEC_BATCH, N_CD, C_HEADS, C_DK, C_DV), 0.5),
        'cache_d_k': nrm((DEC_BATCH, N_CD, D_KV, PAST_LEN, D_HD)),
        'cache_d_v': nrm((DEC_BATCH, N_CD, D_KV, PAST_LEN, D_HD)),
        'c': nrm((DEC_BATCH, D_MODEL)),
        'c_ctx': nrm((D_MODEL,)),
        'norm_g': 1.0 + nrm((DEPTH, D_MODEL), 0.02),
        'mod_w': nrm((DEPTH, D_MODEL, 3 * D_MODEL), D_MODEL ** -0.5),
        'mod_b': nrm((DEPTH, 3 * D_MODEL), 0.02),
        'ab_w_in': nrm((N_AB, D_MODEL, P_AB), D_MODEL ** -0.5),
        'ab_w_out': nrm((N_AB, A_W + B_W, D_MODEL), (A_W + B_W) ** -0.5),
        'a_sink': nrm((N_AB, A_HEADS), 0.5),
        'b_lq1': nrm((N_AB, B_HD), 0.1),
        'b_lk1': nrm((N_AB, B_HD), 0.1),
        'b_lq2': nrm((N_AB, B_HD), 0.1),
        'b_lk2': nrm((N_AB, B_HD), 0.1),
        'b_norm_g': 1.0 + nrm((N_AB, B_VD), 0.02),
        'cd_w_in': nrm((N_CD, D_MODEL, P_CD), D_MODEL ** -0.5),
        'cd_w_out': nrm((N_CD, C_W + D_W, D_MODEL), (C_W + D_W) ** -0.5),
        'c_decay_f': base + nrm((N_CD, C_HEADS), 0.05),
        'c_decay_b': base + nrm((N_CD, C_HEADS), 0.05),
        'c_norm_g': 1.0 + nrm((N_CD, C_HEADS, C_DV), 0.02),
        'd_q_norm_g': 1.0 + nrm((N_CD, D_HD), 0.02),
        'd_k_norm_g': 1.0 + nrm((N_CD, D_HD), 0.02),
        'final_g': 1.0 + nrm((D_MODEL,), 0.02),
    }


def reference(x_prompt, x_sample, cache_a_k, cache_a_v, cache_b_k, cache_b_v, state_c_fwd, state_c_bwd,
              cache_d_k, cache_d_v, c, c_ctx, norm_g, mod_w, mod_b, ab_w_in, ab_w_out, a_sink,
              b_lq1, b_lk1, b_lq2, b_lk2, b_norm_g, cd_w_in, cd_w_out, c_decay_f, c_decay_b, c_norm_g,
              d_q_norm_g, d_k_norm_g, final_g):
    xp, xs = x_prompt, x_sample
    c_ctx_row = c_ctx[None, :]
    a_k, a_v, b_k, b_v, c_f, c_b, d_k, d_v = [], [], [], [], [], [], [], []
    for layer in range(DEPTH):
        i = layer // 2
        hp, gate_p = adaln(xp, c_ctx_row, norm_g[layer], mod_w[layer], mod_b[layer])
        hs, gate_s = adaln(xs, c, norm_g[layer], mod_w[layer], mod_b[layer])
        if layer % 2 == 0:
            lam_init = 0.8 - 0.6 * math.exp(-0.3 * layer)
            ab_w = (ab_w_in[i], ab_w_out[i], a_sink[i], b_lq1[i], b_lk1[i], b_lq2[i], b_lk2[i], b_norm_g[i], lam_init)
            yp, ka, va, kb, vb = ab_context(hp, *ab_w)
            ys = ab_latent(hs, cache_a_k[:, i], cache_a_v[:, i], cache_b_k[:, i], cache_b_v[:, i], *ab_w)
            a_k.append(ka)
            a_v.append(va)
            b_k.append(kb)
            b_v.append(vb)
        else:
            cd_w = (cd_w_in[i], cd_w_out[i], c_decay_f[i], c_decay_b[i], c_norm_g[i], d_q_norm_g[i], d_k_norm_g[i])
            yp, sf, sb, kd, vd = cd_context(hp, *cd_w)
            ys = cd_latent(hs, state_c_fwd[:, i], state_c_bwd[:, i], cache_d_k[:, i], cache_d_v[:, i], *cd_w)
            c_f.append(sf)
            c_b.append(sb)
            d_k.append(kd)
            d_v.append(vd)
        xp = xp + gate_p * yp
        xs = xs + gate_s * ys
    y_prompt = rms_norm(xp, final_g)
    y_sample = rms_norm(xs, final_g)
    new_a_k = jnp.stack(a_k, axis=1)
    new_a_v = jnp.stack(a_v, axis=1)
    new_b_k = jnp.stack(b_k, axis=1)
    new_b_v = jnp.stack(b_v, axis=1)
    new_c_fwd = jnp.stack(c_f, axis=1)
    new_c_bwd = jnp.stack(c_b, axis=1)
    new_d_k = jnp.stack(d_k, axis=1)
    new_d_v = jnp.stack(d_v, axis=1)
    return (y_prompt, y_sample, new_a_k, new_a_v, new_b_k, new_b_v, new_c_fwd, new_c_bwd, new_d_k, new_d_v)
```

```python
import functools
import math

import numpy as np
import jax
import jax.numpy as jnp
from jax import lax
from jax.experimental import pallas as pl
from jax.experimental.pallas import tpu as pltpu

F32 = jnp.float32
BF16 = jnp.bfloat16

LANES = 128
HEAD = 64
GRID_W = 64
CHUNK = 128
WINDOW = 128
ROPE_THETA = 10000.0
EPS = 1e-6
NEG = -1e30
VMEM_LIMIT = 56 * 1024 * 1024

GQA_HEADS = 8
PAIR_ORDER = (0, 4, 1, 5, 2, 6, 3, 7)
N_QCOL = 4
C_HEADS = 4
C_DK = 128

N_QKV_BLK = 18
N_GATE_BLK = 8
AB_OPS = ("qrope",) * 4 + ("rope", "plain") + ("qrope",) * 4 + ("rope",) * 4 + ("plain",) * 4
CD_OPS = ("plain",) * 4 + ("kscale",) * 4 + ("plain",) * 4 + ("qnorm",) * 4 + ("knorm", "plain")
AB_GROUPS = ((0, 4), (4, 6), (6, 10), (10, 14), (14, 18))
CD_GROUPS = ((0, 4), (4, 8), (8, 12), (12, 16), (16, 18))


def _silu(x):
    return x / (1.0 + jnp.exp(-x))


def _lane_lo(shape):
    return lax.broadcasted_iota(jnp.int32, shape, len(shape) - 1) % LANES < HEAD


def _mod_kernel(cond_ref, w_ref, b_ref, o_ref):
    s = _silu(cond_ref[...])
    o_ref[0] = jnp.dot(s.astype(BF16), w_ref[0].astype(BF16), preferred_element_type=F32) + b_ref[0]


def _modulation(cond, mod_w, mod_b):
    depth, d, d3 = mod_w.shape
    rows = cond.shape[0]
    return pl.pallas_call(
        _mod_kernel,
        grid=(depth, d3 // d),
        in_specs=[pl.BlockSpec((rows, d), lambda l, j: (0, 0)),
                  pl.BlockSpec((1, d, d), lambda l, j: (l, 0, j)),
                  pl.BlockSpec((1, 1, d), lambda l, j: (l, 0, j))],
        out_specs=pl.BlockSpec((1, rows, d), lambda l, j: (l, 0, j)),
        out_shape=jax.ShapeDtypeStruct((depth, rows, d3), F32),
        compiler_params=pltpu.CompilerParams(vmem_limit_bytes=VMEM_LIMIT),
        name="modulation",
    )(cond, mod_w, mod_b.reshape(depth, 1, d3))


def _rot_half(x, first_half):
    return jnp.where(first_half, pltpu.roll(x, LANES - HEAD // 2, 1), pltpu.roll(x, HEAD // 2, 1))


def _head_sumsq(x, bd):
    sq = x * x
    hi = sq.astype(BF16)
    lo = (sq - hi.astype(F32)).astype(BF16)
    return jnp.dot(hi, bd, preferred_element_type=F32) + jnp.dot(lo, bd, preferred_element_type=F32)


def _proj_kernel(*refs, ops, groups, use_rope, has_norm):
    it = iter(refs)
    x_ref, mod_ref, g_ref, w_ref = next(it), next(it), next(it), next(it)
    cos_ref = sin_ref = qg_ref = kg_ref = bd_ref = None
    if use_rope:
        cos_ref, sin_ref = next(it), next(it)
    if has_norm:
        qg_ref, kg_ref, bd_ref = next(it), next(it), next(it)
    zq_ref, zg_ref = next(it), next(it)

    x = x_ref[...]
    h = x * lax.rsqrt(jnp.mean(x * x, axis=-1, keepdims=True) + EPS) * g_ref[...]
    h = h * (1.0 + mod_ref[0, 1:2, :]) + mod_ref[0, 0:1, :]
    hb = h.astype(BF16)

    tm = x.shape[0]
    if use_rope:
        cos, sin = cos_ref[...], sin_ref[...]
        first_half = lax.broadcasted_iota(jnp.int32, (tm, LANES), 1) % HEAD < HEAD // 2

    def rope(z):
        if not use_rope:
            return z
        return z * cos + _rot_half(z, first_half) * sin

    def head_norm(z, gain_ref):
        ss = _head_sumsq(z, bd_ref[...])
        return z * lax.rsqrt(ss * (1.0 / HEAD) + EPS) * gain_ref[...]

    q_scale = HEAD ** -0.5
    for b0, b1 in groups:
        z = jnp.dot(hb, w_ref[:, b0 * LANES:b1 * LANES], preferred_element_type=F32)
        for j in range(b0, b1):
            zz = z[:, (j - b0) * LANES:(j - b0 + 1) * LANES]
            op = ops[j]
            if op == "qrope":
                zz = rope(zz * q_scale)
            elif op == "rope":
                zz = rope(zz)
            elif op == "kscale":
                zz = zz * (C_DK ** -0.5)
            elif op == "qnorm":
                zz = rope(head_norm(zz, qg_ref)) * q_scale
            elif op == "knorm":
                zz = rope(head_norm(zz, kg_ref))
            zq_ref[:, j * LANES:(j + 1) * LANES] = zz.astype(zq_ref.dtype)
    half = N_GATE_BLK // 2
    for g0 in (0, half):
        c0 = (N_QKV_BLK + g0) * LANES
        z = jnp.dot(hb, w_ref[:, c0:c0 + half * LANES], preferred_element_type=F32)
        zg_ref[:, g0 * LANES:(g0 + half) * LANES] = _silu(z)


def _project(x2d, mod, mod_row_of_tile, norm_g, w_bf16, *, tm, kind, rope_tabs, norm_params, out_dtype):
    t, d = x2d.shape
    use_rope = rope_tabs is not None
    has_norm = kind == "cd"
    ops, groups = (AB_OPS, AB_GROUPS) if kind == "ab" else (CD_OPS, CD_GROUPS)
    in_specs = [pl.BlockSpec((tm, d), lambda i: (i, 0)),
                pl.BlockSpec((1, 3, d), lambda i: (mod_row_of_tile(i), 0, 0)),
                pl.BlockSpec((1, d), lambda i: (0, 0)),
                pl.BlockSpec(w_bf16.shape, lambda i: (0, 0))]
    args = [x2d, mod, norm_g.reshape(1, d), w_bf16]
    if use_rope:
        n_seq_tiles = rope_tabs[0].shape[0] // tm
        in_specs += [pl.BlockSpec((tm, LANES), lambda i: (i % n_seq_tiles, 0))] * 2
        args += list(rope_tabs)
    if has_norm:
        in_specs += [pl.BlockSpec((1, LANES), lambda i: (0, 0))] * 2 + [pl.BlockSpec((LANES, LANES), lambda i: (0, 0))]
        args += list(norm_params)
    return pl.pallas_call(
        functools.partial(_proj_kernel, ops=ops, groups=groups, use_rope=use_rope, has_norm=has_norm),
        grid=(t // tm,),
        in_specs=in_specs,
        out_specs=[pl.BlockSpec((tm, N_QKV_BLK * LANES), lambda i: (i, 0)),
                   pl.BlockSpec((tm, N_GATE_BLK * LANES), lambda i: (i, 0))],
        out_shape=[jax.ShapeDtypeStruct((t, N_QKV_BLK * LANES), out_dtype),
                   jax.ShapeDtypeStruct((t, N_GATE_BLK * LANES), F32)],
        compiler_params=pltpu.CompilerParams(vmem_limit_bytes=VMEM_LIMIT),
        name="project_" + kind,
    )(*args)


def _pair_rows(q):
    lo = jnp.where(_lane_lo((1, LANES)), 1.0, 0.0).astype(BF16)
    qb = q.astype(BF16)
    return jnp.concatenate([qb * lo, qb * (1.0 - lo).astype(BF16)], axis=0)


def _softmax_step(qrows, k, v, m, l, acc, mask=None):
    s = lax.dot_general(qrows, k, (((1,), (1,)), ((), ())), preferred_element_type=F32)
    if mask is not None:
        s = jnp.where(mask, s, NEG)
    m_new = jnp.maximum(m, jnp.max(s, axis=-1, keepdims=True))
    alpha = jnp.exp(m - m_new)
    p = jnp.exp(s - m_new)
    l = alpha * l + jnp.sum(p, axis=-1, keepdims=True)
    acc = alpha * acc + jnp.dot(p.astype(BF16), v, preferred_element_type=F32)
    return m_new, l, acc


def _dense_attn_kernel(*refs, mode, has_sink, has_ctx, tk, lam_init):
    it = iter(refs)
    q_ref, k_ref, v_ref = next(it), next(it), next(it)
    kx_ref = vx_ref = sink_ref = None
    if has_ctx:
        kx_ref, vx_ref = next(it), next(it)
    if has_sink:
        sink_ref = next(it)
    if mode == "diff":
        lq1_ref, lk1_ref, lq2_ref, lk2_ref, bg_ref = next(it), next(it), next(it), next(it), next(it)
    gate_ref, y_ref = next(it), next(it)

    tq = q_ref.shape[1]
    qrows = _pair_rows(q_ref[0])
    if has_sink:
        c = pl.program_id(1)
        m0 = jnp.concatenate([jnp.full((tq, 1), sink_ref[c], F32),
                              jnp.full((tq, 1), sink_ref[c + N_QCOL], F32)], axis=0)
        l0 = jnp.ones((2 * tq, 1), F32)
    else:
        m0 = jnp.full((2 * tq, 1), NEG, F32)
        l0 = jnp.zeros((2 * tq, 1), F32)
    carry = (m0, l0, jnp.zeros((2 * tq, LANES), F32))

    n_keys = k_ref.shape[1]
    if n_keys == tk:
        carry = _softmax_step(qrows, k_ref[0].astype(BF16), v_ref[0].astype(BF16), *carry)
    else:
        def body(j, cr):
            rows = pl.ds(pl.multiple_of(j * tk, tk), tk)
            return _softmax_step(qrows, k_ref[0, rows, :].astype(BF16), v_ref[0, rows, :].astype(BF16), *cr)
        carry = lax.fori_loop(0, n_keys // tk, body, carry)
    if has_ctx:
        carry = _softmax_step(qrows, kx_ref[0].astype(BF16), vx_ref[0].astype(BF16), *carry)
    _, l, acc = carry
    o2 = acc / l
    if mode == "pair":
        o = jnp.where(_lane_lo((tq, LANES)), o2[:tq], o2[tq:])
    else:
        lam = (jnp.exp(jnp.sum(lq1_ref[...] * lk1_ref[...], axis=-1, keepdims=True))
               - jnp.exp(jnp.sum(lq2_ref[...] * lk2_ref[...], axis=-1, keepdims=True)) + lam_init)
        o = o2[:tq] - lam * o2[tq:]
        o = o * lax.rsqrt(jnp.mean(o * o, axis=-1, keepdims=True) + EPS) * bg_ref[...] * (1.0 - lam_init)
    y_ref[0] = (o * gate_ref[0]).astype(y_ref.dtype)


def _dense_attention(zq, zg, *, q_blk, k_blk, v_blk, kv_per_col, gate_blk, mode, tq, tk,
                     ctx_kv=None, sink=None, diff_params=None, lam_init=0.0):
    b, n, _ = zq.shape
    kb = (lambda c: k_blk + c) if kv_per_col else (lambda c: k_blk)
    vb = (lambda c: v_blk + c) if kv_per_col else (lambda c: v_blk)
    in_specs = [pl.BlockSpec((1, tq, LANES), lambda bi, c, qi: (bi, qi, q_blk + c)),
                pl.BlockSpec((1, n, LANES), lambda bi, c, qi: (bi, 0, kb(c))),
                pl.BlockSpec((1, n, LANES), lambda bi, c, qi: (bi, 0, vb(c)))]
    args = [zq, zq, zq]
    if ctx_kv is not None:
        kx, vx = ctx_kv
        if kx.ndim == 4:
            spec = pl.BlockSpec((None, 1, kx.shape[2], LANES), lambda bi, c, qi: (bi, c, 0, 0))
        else:
            spec = pl.BlockSpec((1, kx.shape[1], LANES), lambda bi, c, qi: (bi, 0, 0))
        in_specs += [spec, spec]
        args += [kx, vx]
    if sink is not None:
        in_specs.append(pl.BlockSpec(memory_space=pltpu.SMEM))
        args.append(sink)
    if mode == "diff":
        in_specs += [pl.BlockSpec((1, HEAD), lambda bi, c, qi: (0, 0))] * 4
        in_specs.append(pl.BlockSpec((1, LANES), lambda bi, c, qi: (0, 0)))
        args += list(diff_params)
    in_specs.append(pl.BlockSpec((1, tq, LANES), lambda bi, c, qi: (bi, qi, gate_blk + c)))
    args.append(zg)
    return pl.pallas_call(
        functools.partial(_dense_attn_kernel, mode=mode, has_sink=sink is not None,
                          has_ctx=ctx_kv is not None, tk=tk, lam_init=lam_init),
        grid=(b, N_QCOL, n // tq),
        in_specs=in_specs,
        out_specs=pl.BlockSpec((1, tq, LANES), lambda bi, c, qi: (bi, qi, c)),
        out_shape=jax.ShapeDtypeStruct((b, n, N_QCOL * LANES), BF16),
        compiler_params=pltpu.CompilerParams(vmem_limit_bytes=VMEM_LIMIT),
        name="attention_" + mode,
    )(*args)


def _window_attn_kernel(q_ref, kp_ref, kc_ref, kn_ref, vp_ref, vc_ref, vn_ref, kx_ref, vx_ref,
                        sink_ref, gate_ref, y_ref):
    i = pl.program_id(1)
    n_blocks = pl.num_programs(1)
    q = q_ref[0]
    qrows = jnp.concatenate([_pair_rows(q[:, c * LANES:(c + 1) * LANES]) for c in range(N_QCOL)], axis=0)
    k = jnp.concatenate([kp_ref[0], kc_ref[0], kn_ref[0], kx_ref[0].astype(BF16)], axis=0)
    v = jnp.concatenate([vp_ref[0], vc_ref[0], vn_ref[0], vx_ref[0].astype(BF16)], axis=0)
    s = lax.dot_general(qrows, k, (((1,), (1,)), ((), ())), preferred_element_type=F32)

    n_keys = k.shape[0]
    r = lax.broadcasted_iota(jnp.int32, (CHUNK, n_keys), 0)
    m = lax.broadcasted_iota(jnp.int32, (CHUNK, n_keys), 1)
    dist = m - CHUNK - r
    first_m = jnp.where(i > 0, 0, CHUNK)
    end_m = jnp.where(i < n_blocks - 1, 3 * CHUNK, 2 * CHUNK)
    valid = ((dist >= -WINDOW) & (dist <= WINDOW) & (m >= first_m) & (m < end_m)) | (m >= 3 * CHUNK)

    outs = []
    for hrow in range(GQA_HEADS):
        sh = jnp.where(valid, s[hrow * CHUNK:(hrow + 1) * CHUNK], NEG)
        sk = sink_ref[PAIR_ORDER[hrow]]
        mx = jnp.maximum(jnp.max(sh, axis=-1, keepdims=True), sk)
        p = jnp.exp(sh - mx)
        l = jnp.sum(p, axis=-1, keepdims=True) + jnp.exp(sk - mx)
        outs.append(jnp.dot(p.astype(BF16), v, preferred_element_type=F32) / l)
    lo = _lane_lo((CHUNK, LANES))
    o = jnp.concatenate([jnp.where(lo, outs[2 * c], outs[2 * c + 1]) for c in range(N_QCOL)], axis=1)
    y_ref[0] = (o * gate_ref[0]).astype(y_ref.dtype)


def _window_attention(zq, zg, kx, vx, sink):
    b, n, _ = zq.shape
    nb = n // CHUNK
    w4 = N_QCOL * LANES

    def kv_specs(blk):
        return [pl.BlockSpec((1, CHUNK, LANES), lambda bi, i: (bi, jnp.maximum(i - 1, 0), blk)),
                pl.BlockSpec((1, CHUNK, LANES), lambda bi, i: (bi, i, blk)),
                pl.BlockSpec((1, CHUNK, LANES), lambda bi, i: (bi, jnp.minimum(i + 1, nb - 1), blk))]

    ctx_spec = pl.BlockSpec((1, kx.shape[1], LANES), lambda bi, i: (bi, 0, 0))
    return pl.pallas_call(
        _window_attn_kernel,
        grid=(b, nb),
        in_specs=[pl.BlockSpec((1, CHUNK, w4), lambda bi, i: (bi, i, 0))] + kv_specs(4) + kv_specs(5)
                 + [ctx_spec, ctx_spec, pl.BlockSpec(memory_space=pltpu.SMEM),
                    pl.BlockSpec((1, CHUNK, w4), lambda bi, i: (bi, i, 0))],
        out_specs=pl.BlockSpec((1, CHUNK, w4), lambda bi, i: (bi, i, 0)),
        out_shape=jax.ShapeDtypeStruct((b, n, w4), BF16),
        compiler_params=pltpu.CompilerParams(vmem_limit_bytes=VMEM_LIMIT),
        name="attention_window",
    )(zq, zq, zq, zq, zq, zq, zq, kx, vx, sink, zg)


def _retention_kernel(*refs, has_state_in, emit_state):
    it = iter(refs)
    q_ref, k_ref, v_ref, decf_ref, decb_ref, cg_ref, gate_ref = (next(it) for _ in range(7))
    sf_in = sb_in = sf_out = sb_out = None
    if has_state_in:
        sf_in, sb_in = next(it), next(it)
    y_ref = next(it)
    if emit_state:
        sf_out, sb_out = next(it), next(it)
    o_scr, st_scr, dm_scr = next(it), next(it), next(it)

    nc = q_ref.shape[1] // CHUNK
    ri = lax.broadcasted_iota(jnp.int32, (CHUNK, CHUNK), 0).astype(F32)
    ci = lax.broadcasted_iota(jnp.int32, (CHUNK, CHUNK), 1).astype(F32)
    col = lax.broadcasted_iota(jnp.int32, (CHUNK, 1), 0).astype(F32)

    def decay_vectors(dec_ref, forward):
        lg = -jnp.exp(dec_ref[0][:, :1])
        if forward:
            cross = jnp.exp(lg * (col + 1.0))
            kdec = jnp.exp(lg * (CHUNK - 1.0 - col))
        else:
            cross = jnp.exp(lg * (CHUNK - 1.0 - col))
            kdec = jnp.exp(lg * col)
        return lg, cross, kdec, jnp.exp(lg * CHUNK)

    def decay_matrix(lg, forward):
        rel = (ri - ci) if forward else (ci - ri - 1.0)
        ok = rel >= 0.0
        return jnp.where(ok, jnp.exp(lg * jnp.where(ok, rel, 0.0)), 0.0)

    def chunk_step(c, cross, kdec, cdec):
        rows = pl.ds(pl.multiple_of(c * CHUNK, CHUNK), CHUNK)
        qh = q_ref[0, rows, :].astype(BF16)
        kf = k_ref[0, rows, :].astype(F32)
        vh = v_ref[0, rows, :].astype(BF16)
        att = lax.dot_general(qh, kf.astype(BF16), (((1,), (1,)), ((), ())),
                              preferred_element_type=F32) * dm_scr[...]
        state = st_scr[...]
        o = (jnp.dot(att.astype(BF16), vh, preferred_element_type=F32)
             + cross * jnp.dot(qh, state.astype(BF16), preferred_element_type=F32))
        st_scr[...] = cdec * state + jnp.dot((kf * kdec).T.astype(BF16), vh, preferred_element_type=F32)
        return o, rows

    lg, cross_f, kdec_f, cdec_f = decay_vectors(decf_ref, True)
    dm_scr[...] = decay_matrix(lg, True)
    st_scr[...] = sf_in[0, 0] if has_state_in else jnp.zeros((CHUNK, CHUNK), F32)

    def fwd_body(c, _):
        o, rows = chunk_step(c, cross_f, kdec_f, cdec_f)
        o_scr[rows, :] = o
        return 0
    lax.fori_loop(0, nc, fwd_body, 0)
    if emit_state:
        sf_out[0, 0] = st_scr[...].astype(sf_out.dtype)

    lg, cross_b, kdec_b, cdec_b = decay_vectors(decb_ref, False)
    dm_scr[...] = decay_matrix(lg, False)
    st_scr[...] = sb_in[0, 0] if has_state_in else jnp.zeros((CHUNK, CHUNK), F32)

    def bwd_body(t, _):
        o, rows = chunk_step(nc - 1 - t, cross_b, kdec_b, cdec_b)
        o = o + o_scr[rows, :]
        mu = jnp.mean(o, axis=-1, keepdims=True)
        d = o - mu
        var = jnp.mean(d * d, axis=-1, keepdims=True)
        y = d * lax.rsqrt(var + EPS) * cg_ref[0]
        y_ref[0, rows, :] = (y * gate_ref[0, rows, :]).astype(y_ref.dtype)
        return 0
    lax.fori_loop(0, nc, bwd_body, 0)
    if emit_state:
        sb_out[0, 0] = st_scr[...].astype(sb_out.dtype)


def _retention(zq, zg, dec_f, dec_b, c_norm_g, states, emit_state):
    b, n, _ = zq.shape
    seq_spec = lambda blk: pl.BlockSpec((1, n, LANES), lambda bi, h: (bi, 0, blk + h))
    st_spec = pl.BlockSpec((1, 1, CHUNK, CHUNK), lambda bi, h: (bi, h, 0, 0))
    head_spec = pl.BlockSpec((1, 1, LANES), lambda bi, h: (h, 0, 0))
    bcast = lambda p: jnp.broadcast_to(p.astype(F32)[:, None, None], (C_HEADS, 1, LANES))
    in_specs = [seq_spec(0), seq_spec(C_HEADS), seq_spec(2 * C_HEADS), head_spec, head_spec, head_spec,
                seq_spec(0)]
    args = [zq, zq, zq, bcast(dec_f), bcast(dec_b), c_norm_g.astype(F32)[:, None, :], zg]
    if states is not None:
        in_specs += [st_spec, st_spec]
        args += list(states)
    out_specs = [seq_spec(0)]
    out_shape = [jax.ShapeDtypeStruct((b, n, C_HEADS * LANES), BF16)]
    if emit_state:
        out_specs += [st_spec, st_spec]
        out_shape += [jax.ShapeDtypeStruct((b, C_HEADS, CHUNK, CHUNK), F32)] * 2
    return pl.pallas_call(
        functools.partial(_retention_kernel, has_state_in=states is not None, emit_state=emit_state),
        grid=(b, C_HEADS),
        in_specs=in_specs,
        out_specs=out_specs,
        out_shape=out_shape,
        scratch_shapes=[pltpu.VMEM((n, LANES), F32),
                        pltpu.VMEM((CHUNK, CHUNK), F32),
                        pltpu.VMEM((CHUNK, CHUNK), F32)],
        compiler_params=pltpu.CompilerParams(vmem_limit_bytes=VMEM_LIMIT),
        name="retention",
    )(*args)


def _out_kernel(y1_ref, y2_ref, x_ref, mod_ref, w_ref, fg_ref, o_ref, *, final_norm):
    half = y1_ref.shape[1]
    y = (jnp.dot(y1_ref[...], w_ref[:half, :], preferred_element_type=F32)
         + jnp.dot(y2_ref[...], w_ref[half:, :], preferred_element_type=F32))
    x = x_ref[...] + mod_ref[0, 2:3, :] * y
    if final_norm:
        x = x * lax.rsqrt(jnp.mean(x * x, axis=-1, keepdims=True) + EPS) * fg_ref[...]
    o_ref[...] = x


def _out_project(y1, y2, x2d, mod, mod_row_of_tile, w_bf16, final_g, *, tm, final_norm):
    t, d = x2d.shape
    half = y1.shape[1]
    return pl.pallas_call(
        functools.partial(_out_kernel, final_norm=final_norm),
        grid=(t // tm,),
        in_specs=[pl.BlockSpec((tm, half), lambda i: (i, 0)),
                  pl.BlockSpec((tm, half), lambda i: (i, 0)),
                  pl.BlockSpec((tm, d), lambda i: (i, 0)),
                  pl.BlockSpec((1, 3, d), lambda i: (mod_row_of_tile(i), 0, 0)),
                  pl.BlockSpec(w_bf16.shape, lambda i: (0, 0)),
                  pl.BlockSpec((1, d), lambda i: (0, 0))],
        out_specs=pl.BlockSpec((tm, d), lambda i: (i, 0)),
        out_shape=jax.ShapeDtypeStruct((t, d), F32),
        compiler_params=pltpu.CompilerParams(vmem_limit_bytes=VMEM_LIMIT),
        name="out_project",
    )(y1, y2, x2d, mod, w_bf16, final_g.reshape(1, d))


def _pair_perm():
    return np.concatenate([np.arange(h * HEAD, (h + 1) * HEAD) for h in PAIR_ORDER])


def _rope_tables(n):
    rows = n // GRID_W
    row = jnp.repeat(jnp.arange(rows, dtype=F32), GRID_W)
    col = jnp.tile(jnp.arange(GRID_W, dtype=F32), rows)
    nf = HEAD // 4
    inv = ROPE_THETA ** (-jnp.arange(nf, dtype=F32) / nf)
    ang = jnp.concatenate([row[:, None] * inv, col[:, None] * inv], axis=-1)
    cos, sin = jnp.cos(ang), jnp.sin(ang)
    return jnp.tile(cos, (1, 4)), jnp.tile(jnp.concatenate([-sin, sin], axis=-1), (1, 2))


def _pair_kv(cache):
    b, g, p, d = cache.shape
    return cache.transpose(0, 2, 1, 3).reshape(b, p, g * d)


def _unpair_kv(z, blk, n_blk, heads):
    b, p, _ = z.shape
    t = z[:, :, blk * LANES:(blk + n_blk) * LANES]
    return t.reshape(b, p, heads, t.shape[-1] // heads).transpose(0, 2, 1, 3)


def kernel(x_prompt, x_sample, cache_a_k, cache_a_v, cache_b_k, cache_b_v, state_c_fwd, state_c_bwd, cache_d_k, cache_d_v, c, c_ctx, norm_g, mod_w, mod_b, ab_w_in, ab_w_out, a_sink, b_lq1, b_lk1, b_lq2, b_lk2, b_norm_g, cd_w_in, cd_w_out, c_decay_f, c_decay_b, c_norm_g, d_q_norm_g, d_k_norm_g, final_g):
    depth = norm_g.shape[0]
    bp, sp, d = x_prompt.shape
    bs, ss, _ = x_sample.shape
    dt = x_prompt.dtype

    ctx_row = bs
    pad = (-(bs + 1)) % 8
    cond = jnp.concatenate([c, c_ctx[None, :], jnp.zeros((pad, d), c.dtype)], axis=0)
    mod = _modulation(cond, mod_w, mod_b).reshape(depth, cond.shape[0], 3, d)

    perm = _pair_perm()
    qkv_w = N_QKV_BLK * LANES
    rope_tabs = _rope_tables(ss)
    bd = jnp.asarray(np.kron(np.eye(LANES // HEAD), np.ones((HEAD, HEAD))), BF16)

    tm_s = 512
    tm_p = 512
    tiles_per_seq = ss // tm_s
    row_s = lambda i: i // tiles_per_seq
    row_p = lambda i: ctx_row

    xp = x_prompt.reshape(bp * sp, d)
    xs = x_sample.reshape(bs * ss, d)
    outs = {k: [] for k in ("a_k", "a_v", "b_k", "b_v", "c_f", "c_b", "d_k", "d_v")}

    for layer in range(depth):
        i = layer // 2
        last = layer == depth - 1
        if layer % 2 == 0:
            lam_init = 0.8 - 0.6 * math.exp(-0.3 * layer)
            cols = np.arange(ab_w_in.shape[2])
            cols[0:N_QCOL * LANES] = perm
            cols[qkv_w:qkv_w + N_QCOL * LANES] = qkv_w + perm
            w_in = ab_w_in[i][:, cols].astype(BF16)
            rows = np.arange(ab_w_out.shape[1])
            rows[0:N_QCOL * LANES] = perm
            w_out = ab_w_out[i][rows, :].astype(BF16)
            diff_params = (b_lq1[i][None], b_lk1[i][None], b_lq2[i][None], b_lk2[i][None], b_norm_g[i][None])

            zq, zg = _project(xp, mod[layer], row_p, norm_g[layer], w_in, tm=tm_p, kind="ab",
                              rope_tabs=None, norm_params=None, out_dtype=F32)
            zq3, zg3 = zq.reshape(bp, sp, -1), zg.reshape(bp, sp, -1)
            ya = _dense_attention(zq3, zg3, q_blk=0, k_blk=4, v_blk=5, kv_per_col=False, gate_blk=0,
                                  mode="pair", tq=sp, tk=sp, sink=a_sink[i])
            yb = _dense_attention(zq3, zg3, q_blk=6, k_blk=10, v_blk=14, kv_per_col=True, gate_blk=4,
                                  mode="diff", tq=sp, tk=sp, diff_params=diff_params, lam_init=lam_init)
            xp = _out_project(ya.reshape(bp * sp, -1), yb.reshape(bp * sp, -1), xp, mod[layer], row_p,
                              w_out, final_g, tm=tm_p, final_norm=last)
            outs["a_k"].append(_unpair_kv(zq3, 4, 1, 2))
            outs["a_v"].append(_unpair_kv(zq3, 5, 1, 2))
            outs["b_k"].append(_unpair_kv(zq3, 10, 4, 4))
            outs["b_v"].append(_unpair_kv(zq3, 14, 4, 4))

            zq, zg = _project(xs, mod[layer], row_s, norm_g[layer], w_in, tm=tm_s, kind="ab",
                              rope_tabs=rope_tabs, norm_params=None, out_dtype=BF16)
            zq3, zg3 = zq.reshape(bs, ss, -1), zg.reshape(bs, ss, -1)
            ya = _window_attention(zq3, zg3, _pair_kv(cache_a_k[:, i]), _pair_kv(cache_a_v[:, i]), a_sink[i])
            yb = _dense_attention(zq3, zg3, q_blk=6, k_blk=10, v_blk=14, kv_per_col=True, gate_blk=4,
                                  mode="diff", tq=256, tk=512, ctx_kv=(cache_b_k[:, i], cache_b_v[:, i]),
                                  diff_params=diff_params, lam_init=lam_init)
            xs = _out_project(ya.reshape(bs * ss, -1), yb.reshape(bs * ss, -1), xs, mod[layer], row_s,
                              w_out, final_g, tm=tm_s, final_norm=last)
        else:
            cols = np.arange(cd_w_in.shape[2])
            cols[12 * LANES:16 * LANES] = 12 * LANES + perm
            cols[qkv_w + N_QCOL * LANES:qkv_w + 2 * N_QCOL * LANES] = qkv_w + N_QCOL * LANES + perm
            w_in = cd_w_in[i][:, cols].astype(BF16)
            rows = np.arange(cd_w_out.shape[1])
            rows[N_QCOL * LANES:] = N_QCOL * LANES + perm
            w_out = cd_w_out[i][rows, :].astype(BF16)
            norm_params = (jnp.tile(d_q_norm_g[i], 2)[None], jnp.tile(d_k_norm_g[i], 2)[None], bd)

            zq, zg = _project(xp, mod[layer], row_p, norm_g[layer], w_in, tm=tm_p, kind="cd",
                              rope_tabs=None, norm_params=norm_params, out_dtype=F32)
            zq3, zg3 = zq.reshape(bp, sp, -1), zg.reshape(bp, sp, -1)
            yc, s_f, s_b = _retention(zq3, zg3, c_decay_f[i], c_decay_b[i], c_norm_g[i], None, True)
            yd = _dense_attention(zq3, zg3, q_blk=12, k_blk=16, v_blk=17, kv_per_col=False, gate_blk=4,
                                  mode="pair", tq=sp, tk=sp)
            xp = _out_project(yc.reshape(bp * sp, -1), yd.reshape(bp * sp, -1), xp, mod[layer], row_p,
                              w_out, final_g, tm=tm_p, final_norm=last)
            outs["c_f"].append(s_f.astype(dt))
            outs["c_b"].append(s_b.astype(dt))
            outs["d_k"].append(_unpair_kv(zq3, 16, 1, 2))
            outs["d_v"].append(_unpair_kv(zq3, 17, 1, 2))

            zq, zg = _project(xs, mod[layer], row_s, norm_g[layer], w_in, tm=tm_s, kind="cd",
                              rope_tabs=rope_tabs, norm_params=norm_params, out_dtype=BF16)
            zq3, zg3 = zq.reshape(bs, ss, -1), zg.reshape(bs, ss, -1)
            yc = _retention(zq3, zg3, c_decay_f[i], c_decay_b[i], c_norm_g[i],
                            (state_c_fwd[:, i], state_c_bwd[:, i]), False)[0]
            yd = _dense_attention(zq3, zg3, q_blk=12, k_blk=16, v_blk=17, kv_per_col=False, gate_blk=4,
                                  mode="pair", tq=256, tk=512,
                                  ctx_kv=(_pair_kv(cache_d_k[:, i]), _pair_kv(cache_d_v[:, i])))
            xs = _out_project(yc.reshape(bs * ss, -1), yd.reshape(bs * ss, -1), xs, mod[layer], row_s,
                              w_out, final_g, tm=tm_s, final_norm=last)

    stack = lambda k: jnp.stack(outs[k], axis=1)
    return (xp.reshape(bp, sp, d), xs.reshape(bs, ss, d), stack("a_k"), stack("a_v"), stack("b_k"), stack("b_v"),
            stack("c_f"), stack("c_b"), stack("d_k"), stack("d_v"))
```

```python
import functools
import math

import numpy as np
import jax
import jax.numpy as jnp
from jax import lax
from jax.experimental import pallas as pl
from jax.experimental.pallas import tpu as pltpu

F32 = jnp.float32
BF16 = jnp.bfloat16

LANES = 128
HEAD = 64
GRID_W = 64
CHUNK = 128
WINDOW = 128
ROPE_THETA = 10000.0
EPS = 1e-6
NEG = -1e30
VMEM_LIMIT = 56 * 1024 * 1024
TQ_DENSE = 256
TK_DENSE = 512

GQA_HEADS = 8
PAIR_ORDER = (0, 4, 1, 5, 2, 6, 3, 7)
N_QCOL = 4
C_HEADS = 4
C_DK = 128

N_QKV_BLK = 18
N_GATE_BLK = 8
AB_OPS = ("qrope",) * 4 + ("rope", "plain") + ("qrope",) * 4 + ("rope",) * 4 + ("plain",) * 4
CD_OPS = ("plain",) * 4 + ("kscale",) * 4 + ("plain",) * 4 + ("qnorm",) * 4 + ("knorm", "plain")
AB_GROUPS = ((0, 4), (4, 6), (6, 10), (10, 14), (14, 18))
CD_GROUPS = ((0, 4), (4, 8), (8, 12), (12, 16), (16, 18))


def _silu(x):
    return x / (1.0 + jnp.exp(-x))


def _lane_lo(shape):
    return lax.broadcasted_iota(jnp.int32, shape, len(shape) - 1) % LANES < HEAD


def _mod_kernel(cond_ref, w_ref, b_ref, o_ref):
    s = _silu(cond_ref[...])
    o_ref[0] = jnp.dot(s.astype(BF16), w_ref[0].astype(BF16), preferred_element_type=F32) + b_ref[0]


def _modulation(cond, mod_w, mod_b):
    depth, d, d3 = mod_w.shape
    rows = cond.shape[0]
    return pl.pallas_call(
        _mod_kernel,
        grid=(depth, d3 // d),
        in_specs=[pl.BlockSpec((rows, d), lambda l, j: (0, 0)),
                  pl.BlockSpec((1, d, d), lambda l, j: (l, 0, j)),
                  pl.BlockSpec((1, 1, d), lambda l, j: (l, 0, j))],
        out_specs=pl.BlockSpec((1, rows, d), lambda l, j: (l, 0, j)),
        out_shape=jax.ShapeDtypeStruct((depth, rows, d3), F32),
        compiler_params=pltpu.CompilerParams(vmem_limit_bytes=VMEM_LIMIT),
        name="modulation",
    )(cond, mod_w, mod_b.reshape(depth, 1, d3))


def _rot_half(x, first_half):
    return jnp.where(first_half, pltpu.roll(x, LANES - HEAD // 2, 1), pltpu.roll(x, HEAD // 2, 1))


def _head_sumsq(x, bd):
    sq = x * x
    hi = sq.astype(BF16)
    lo = (sq - hi.astype(F32)).astype(BF16)
    return jnp.dot(hi, bd, preferred_element_type=F32) + jnp.dot(lo, bd, preferred_element_type=F32)


def _proj_kernel(*refs, ops, groups, use_rope, has_norm, vt_blocks):
    it = iter(refs)
    x_ref, mod_ref, g_ref, w_ref = next(it), next(it), next(it), next(it)
    cos_ref = sin_ref = qg_ref = kg_ref = bd_ref = vt_ref = None
    if use_rope:
        cos_ref, sin_ref = next(it), next(it)
    if has_norm:
        qg_ref, kg_ref, bd_ref = next(it), next(it), next(it)
    zq_ref, zg_ref = next(it), next(it)
    if vt_blocks:
        vt_ref = next(it)

    x = x_ref[...]
    h = x * lax.rsqrt(jnp.mean(x * x, axis=-1, keepdims=True) + EPS) * g_ref[...]
    h = h * (1.0 + mod_ref[0, 1:2, :]) + mod_ref[0, 0:1, :]
    hb = h.astype(BF16)

    tm = x.shape[0]
    if use_rope:
        cos, sin = cos_ref[...], sin_ref[...]
        first_half = lax.broadcasted_iota(jnp.int32, (tm, LANES), 1) % HEAD < HEAD // 2

    def rope(z):
        if not use_rope:
            return z
        return z * cos + _rot_half(z, first_half) * sin

    def head_norm(z, gain_ref):
        ss = _head_sumsq(z, bd_ref[...])
        return z * lax.rsqrt(ss * (1.0 / HEAD) + EPS) * gain_ref[...]

    q_scale = HEAD ** -0.5
    for b0, b1 in groups:
        z = jnp.dot(hb, w_ref[:, b0 * LANES:b1 * LANES], preferred_element_type=F32)
        for j in range(b0, b1):
            zz = z[:, (j - b0) * LANES:(j - b0 + 1) * LANES]
            op = ops[j]
            if op == "qrope":
                zz = rope(zz * q_scale)
            elif op == "rope":
                zz = rope(zz)
            elif op == "kscale":
                zz = zz * (C_DK ** -0.5)
            elif op == "qnorm":
                zz = rope(head_norm(zz, qg_ref)) * q_scale
            elif op == "knorm":
                zz = rope(head_norm(zz, kg_ref))
            zq_ref[:, j * LANES:(j + 1) * LANES] = zz.astype(zq_ref.dtype)
            if j in vt_blocks:
                vt_ref[vt_blocks.index(j)] = zz.T.astype(vt_ref.dtype)
    half = N_GATE_BLK // 2
    for g0 in (0, half):
        c0 = (N_QKV_BLK + g0) * LANES
        z = jnp.dot(hb, w_ref[:, c0:c0 + half * LANES], preferred_element_type=F32)
        zg_ref[:, g0 * LANES:(g0 + half) * LANES] = _silu(z)


def _project(x2d, mod, mod_row_of_tile, norm_g, w_bf16, *, tm, kind, rope_tabs, norm_params, out_dtype,
             vt_blocks=()):
    t, d = x2d.shape
    out_specs = [pl.BlockSpec((tm, N_QKV_BLK * LANES), lambda i: (i, 0)),
                 pl.BlockSpec((tm, N_GATE_BLK * LANES), lambda i: (i, 0))]
    out_shape = [jax.ShapeDtypeStruct((t, N_QKV_BLK * LANES), out_dtype),
                 jax.ShapeDtypeStruct((t, N_GATE_BLK * LANES), F32)]
    if vt_blocks:
        out_specs.append(pl.BlockSpec((len(vt_blocks), LANES, tm), lambda i: (0, 0, i)))
        out_shape.append(jax.ShapeDtypeStruct((len(vt_blocks), LANES, t), out_dtype))
    use_rope = rope_tabs is not None
    has_norm = kind == "cd"
    ops, groups = (AB_OPS, AB_GROUPS) if kind == "ab" else (CD_OPS, CD_GROUPS)
    in_specs = [pl.BlockSpec((tm, d), lambda i: (i, 0)),
                pl.BlockSpec((1, 3, d), lambda i: (mod_row_of_tile(i), 0, 0)),
                pl.BlockSpec((1, d), lambda i: (0, 0)),
                pl.BlockSpec(w_bf16.shape, lambda i: (0, 0))]
    args = [x2d, mod, norm_g.reshape(1, d), w_bf16]
    if use_rope:
        n_seq_tiles = rope_tabs[0].shape[0] // tm
        in_specs += [pl.BlockSpec((tm, LANES), lambda i: (i % n_seq_tiles, 0))] * 2
        args += list(rope_tabs)
    if has_norm:
        in_specs += [pl.BlockSpec((1, LANES), lambda i: (0, 0))] * 2 + [pl.BlockSpec((LANES, LANES), lambda i: (0, 0))]
        args += list(norm_params)
    return pl.pallas_call(
        functools.partial(_proj_kernel, ops=ops, groups=groups, use_rope=use_rope, has_norm=has_norm,
                          vt_blocks=tuple(vt_blocks)),
        grid=(t // tm,),
        in_specs=in_specs,
        out_specs=out_specs,
        out_shape=out_shape,
        compiler_params=pltpu.CompilerParams(vmem_limit_bytes=VMEM_LIMIT),
        name="project_" + kind,
    )(*args)


def _pair_rows(q):
    lo = jnp.where(_lane_lo((1, LANES)), 1.0, 0.0).astype(BF16)
    qb = q.astype(BF16)
    return jnp.concatenate([qb * lo, qb * (1.0 - lo).astype(BF16)], axis=0)


def _softmax_step(qrows, k, v, m, l, acc, mask=None):
    s = lax.dot_general(qrows, k, (((1,), (1,)), ((), ())), preferred_element_type=F32)
    if mask is not None:
        s = jnp.where(mask, s, NEG)
    m_new = jnp.maximum(m, jnp.max(s, axis=-1, keepdims=True))
    alpha = jnp.exp(m - m_new)
    p = jnp.exp(s - m_new)
    l = alpha * l + jnp.sum(p, axis=-1, keepdims=True)
    acc = alpha * acc + jnp.dot(p.astype(BF16), v, preferred_element_type=F32)
    return m_new, l, acc


def _dense_attn_kernel(*refs, mode, has_sink, has_ctx, tk, lam_init):
    it = iter(refs)
    q_ref, k_ref, v_ref = next(it), next(it), next(it)
    kx_ref = vx_ref = sink_ref = None
    if has_ctx:
        kx_ref, vx_ref = next(it), next(it)
    if has_sink:
        sink_ref = next(it)
    if mode == "diff":
        lq1_ref, lk1_ref, lq2_ref, lk2_ref, bg_ref = next(it), next(it), next(it), next(it), next(it)
    gate_ref, y_ref = next(it), next(it)

    tq = q_ref.shape[1]
    qrows = _pair_rows(q_ref[0])
    if has_sink:
        c = pl.program_id(1)
        m0 = jnp.concatenate([jnp.full((tq, 1), sink_ref[c], F32),
                              jnp.full((tq, 1), sink_ref[c + N_QCOL], F32)], axis=0)
        l0 = jnp.ones((2 * tq, 1), F32)
    else:
        m0 = jnp.full((2 * tq, 1), NEG, F32)
        l0 = jnp.zeros((2 * tq, 1), F32)
    carry = (m0, l0, jnp.zeros((2 * tq, LANES), F32))

    n_keys = k_ref.shape[1]
    if n_keys == tk:
        carry = _softmax_step(qrows, k_ref[0].astype(BF16), v_ref[0].astype(BF16), *carry)
    else:
        def body(j, cr):
            rows = pl.ds(pl.multiple_of(j * tk, tk), tk)
            return _softmax_step(qrows, k_ref[0, rows, :].astype(BF16), v_ref[0, rows, :].astype(BF16), *cr)
        carry = lax.fori_loop(0, n_keys // tk, body, carry)
    if has_ctx:
        carry = _softmax_step(qrows, kx_ref[0].astype(BF16), vx_ref[0].astype(BF16), *carry)
    _, l, acc = carry
    o2 = acc / l
    if mode == "pair":
        o = jnp.where(_lane_lo((tq, LANES)), o2[:tq], o2[tq:])
    else:
        lam = (jnp.exp(jnp.sum(lq1_ref[...] * lk1_ref[...], axis=-1, keepdims=True))
               - jnp.exp(jnp.sum(lq2_ref[...] * lk2_ref[...], axis=-1, keepdims=True)) + lam_init)
        o = o2[:tq] - lam * o2[tq:]
        o = o * lax.rsqrt(jnp.mean(o * o, axis=-1, keepdims=True) + EPS) * bg_ref[...] * (1.0 - lam_init)
    y_ref[0] = (o * gate_ref[0]).astype(y_ref.dtype)


def _dense_attention(zq, zg, *, q_blk, k_blk, v_blk, kv_per_col, gate_blk, mode, tq, tk,
                     ctx_kv=None, sink=None, diff_params=None, lam_init=0.0):
    b, n, _ = zq.shape
    kb = (lambda c: k_blk + c) if kv_per_col else (lambda c: k_blk)
    vb = (lambda c: v_blk + c) if kv_per_col else (lambda c: v_blk)
    in_specs = [pl.BlockSpec((1, tq, LANES), lambda bi, c, qi: (bi, qi, q_blk + c)),
                pl.BlockSpec((1, n, LANES), lambda bi, c, qi: (bi, 0, kb(c))),
                pl.BlockSpec((1, n, LANES), lambda bi, c, qi: (bi, 0, vb(c)))]
    args = [zq, zq, zq]
    if ctx_kv is not None:
        kx, vx = ctx_kv
        if kx.ndim == 4:
            spec = pl.BlockSpec((None, 1, kx.shape[2], LANES), lambda bi, c, qi: (bi, c, 0, 0))
        else:
            spec = pl.BlockSpec((1, kx.shape[1], LANES), lambda bi, c, qi: (bi, 0, 0))
        in_specs += [spec, spec]
        args += [kx, vx]
    if sink is not None:
        in_specs.append(pl.BlockSpec(memory_space=pltpu.SMEM))
        args.append(sink)
    if mode == "diff":
        in_specs += [pl.BlockSpec((1, HEAD), lambda bi, c, qi: (0, 0))] * 4
        in_specs.append(pl.BlockSpec((1, LANES), lambda bi, c, qi: (0, 0)))
        args += list(diff_params)
    in_specs.append(pl.BlockSpec((1, tq, LANES), lambda bi, c, qi: (bi, qi, gate_blk + c)))
    args.append(zg)
    return pl.pallas_call(
        functools.partial(_dense_attn_kernel, mode=mode, has_sink=sink is not None,
                          has_ctx=ctx_kv is not None, tk=tk, lam_init=lam_init),
        grid=(b, N_QCOL, n // tq),
        in_specs=in_specs,
        out_specs=pl.BlockSpec((1, tq, LANES), lambda bi, c, qi: (bi, qi, c)),
        out_shape=jax.ShapeDtypeStruct((b, n, N_QCOL * LANES), BF16),
        compiler_params=pltpu.CompilerParams(vmem_limit_bytes=VMEM_LIMIT),
        name="attention_" + mode,
    )(*args)


def _keymajor_attn_kernel(*refs, mode, tk, lam_init):
    it = iter(refs)
    q_ref, k_ref, vt_ref, kx_ref, vxt_ref = (next(it) for _ in range(5))
    if mode == "diff":
        lq1_ref, lk1_ref, lq2_ref, lk2_ref, bg_ref = (next(it) for _ in range(5))
    gate_ref, y_ref, s_scr, sx_scr = next(it), next(it), next(it), next(it)

    tq = q_ref.shape[1]
    qrows = _pair_rows(q_ref[0])

    def scores(k, slot):
        s = lax.dot_general(k, qrows, (((1,), (1,)), ((), ())), preferred_element_type=F32)
        slot[...] = s
        return jnp.max(s, axis=0, keepdims=True)

    def consume(slot, cmax, vt, m, l, acc):
        m_new = jnp.maximum(m, cmax)
        alpha = jnp.exp(m - m_new)
        p = jnp.exp(slot[...] - m_new)
        l = alpha * l + jnp.sum(p, axis=0, keepdims=True)
        acc = alpha * acc + jnp.dot(vt, p.astype(BF16), preferred_element_type=F32)
        return m_new, l, acc

    n_chunks = k_ref.shape[1] // tk
    carry = (jnp.full((1, 2 * tq), NEG, F32), jnp.zeros((1, 2 * tq), F32), jnp.zeros((LANES, 2 * tq), F32))
    cmax = scores(k_ref[0, 0:tk, :], s_scr.at[0])
    cmax_x = scores(kx_ref[0].astype(BF16), sx_scr)
    for j in range(n_chunks):
        if j + 1 < n_chunks:
            cmax_next = scores(k_ref[0, (j + 1) * tk:(j + 2) * tk, :], s_scr.at[(j + 1) % 2])
        carry = consume(s_scr.at[j % 2], cmax, vt_ref[0, :, j * tk:(j + 1) * tk], *carry)
        cmax = cmax_next
    _, l, acc = consume(sx_scr, cmax_x, vxt_ref[0].astype(BF16), *carry)

    ot = acc / l
    if mode == "pair":
        ot = jnp.concatenate([ot[:HEAD, :tq], ot[HEAD:, tq:]], axis=0)
        o = ot.T
    else:
        lam = (jnp.exp(jnp.sum(lq1_ref[...] * lk1_ref[...], axis=-1, keepdims=True))
               - jnp.exp(jnp.sum(lq2_ref[...] * lk2_ref[...], axis=-1, keepdims=True)) + lam_init)
        o = (ot[:, :tq] - lam * ot[:, tq:]).T
        o = o * lax.rsqrt(jnp.mean(o * o, axis=-1, keepdims=True) + EPS) * bg_ref[...] * (1.0 - lam_init)
    y_ref[0] = (o * gate_ref[0]).astype(y_ref.dtype)


def _keymajor_attention(zq, zg, vt, kx, vxt, *, q_blk, k_blk, kv_per_col, gate_blk, mode, tq, tk,
                        diff_params=None, lam_init=0.0):
    b, n, _ = zq.shape
    per = (lambda c: c) if kv_per_col else (lambda c: 0)
    if kx.ndim == 4:
        kx_spec = pl.BlockSpec((None, 1) + kx.shape[2:], lambda bi, c, qi: (bi, c, 0, 0))
        vx_spec = pl.BlockSpec((None, 1) + vxt.shape[2:], lambda bi, c, qi: (bi, c, 0, 0))
    else:
        kx_spec = pl.BlockSpec((1,) + kx.shape[1:], lambda bi, c, qi: (bi, 0, 0))
        vx_spec = pl.BlockSpec((1,) + vxt.shape[1:], lambda bi, c, qi: (bi, 0, 0))
    in_specs = [pl.BlockSpec((1, tq, LANES), lambda bi, c, qi: (bi, qi, q_blk + c)),
                pl.BlockSpec((1, n, LANES), lambda bi, c, qi: (bi, 0, k_blk + per(c))),
                pl.BlockSpec((1, LANES, n), lambda bi, c, qi: (per(c), 0, bi)),
                kx_spec, vx_spec]
    args = [zq, zq, vt, kx, vxt]
    if mode == "diff":
        in_specs += [pl.BlockSpec((1, HEAD), lambda bi, c, qi: (0, 0))] * 4
        in_specs.append(pl.BlockSpec((1, LANES), lambda bi, c, qi: (0, 0)))
        args += list(diff_params)
    in_specs.append(pl.BlockSpec((1, tq, LANES), lambda bi, c, qi: (bi, qi, gate_blk + c)))
    args.append(zg)
    return pl.pallas_call(
        functools.partial(_keymajor_attn_kernel, mode=mode, tk=tk, lam_init=lam_init),
        grid=(b, N_QCOL, n // tq),
        in_specs=in_specs,
        out_specs=pl.BlockSpec((1, tq, LANES), lambda bi, c, qi: (bi, qi, c)),
        out_shape=jax.ShapeDtypeStruct((b, n, N_QCOL * LANES), BF16),
        scratch_shapes=[pltpu.VMEM((2, tk, 2 * tq), F32), pltpu.VMEM((kx.shape[-2], 2 * tq), F32)],
        compiler_params=pltpu.CompilerParams(vmem_limit_bytes=VMEM_LIMIT),
        name="attention_keymajor_" + mode,
    )(*args)


def _window_attn_kernel(q_ref, kp_ref, kc_ref, kn_ref, vp_ref, vc_ref, vn_ref, kx_ref, vx_ref,
                        sink_ref, gate_ref, y_ref):
    i = pl.program_id(1)
    n_blocks = pl.num_programs(1)
    q = q_ref[0]
    qrows = jnp.concatenate([_pair_rows(q[:, c * LANES:(c + 1) * LANES]) for c in range(N_QCOL)], axis=0)
    k = jnp.concatenate([kp_ref[0], kc_ref[0], kn_ref[0], kx_ref[0].astype(BF16)], axis=0)
    v = jnp.concatenate([vp_ref[0], vc_ref[0], vn_ref[0], vx_ref[0].astype(BF16)], axis=0)
    s = lax.dot_general(qrows, k, (((1,), (1,)), ((), ())), preferred_element_type=F32)

    n_keys = k.shape[0]
    r = lax.broadcasted_iota(jnp.int32, (CHUNK, n_keys), 0)
    m = lax.broadcasted_iota(jnp.int32, (CHUNK, n_keys), 1)
    dist = m - CHUNK - r
    first_m = jnp.where(i > 0, 0, CHUNK)
    end_m = jnp.where(i < n_blocks - 1, 3 * CHUNK, 2 * CHUNK)
    valid = ((dist >= -WINDOW) & (dist <= WINDOW) & (m >= first_m) & (m < end_m)) | (m >= 3 * CHUNK)

    outs = []
    for hrow in range(GQA_HEADS):
        sh = jnp.where(valid, s[hrow * CHUNK:(hrow + 1) * CHUNK], NEG)
        sk = sink_ref[PAIR_ORDER[hrow]]
        mx = jnp.maximum(jnp.max(sh, axis=-1, keepdims=True), sk)
        p = jnp.exp(sh - mx)
        l = jnp.sum(p, axis=-1, keepdims=True) + jnp.exp(sk - mx)
        outs.append(jnp.dot(p.astype(BF16), v, preferred_element_type=F32) / l)
    lo = _lane_lo((CHUNK, LANES))
    o = jnp.concatenate([jnp.where(lo, outs[2 * c], outs[2 * c + 1]) for c in range(N_QCOL)], axis=1)
    y_ref[0] = (o * gate_ref[0]).astype(y_ref.dtype)


def _window_attention(zq, zg, kx, vx, sink):
    b, n, _ = zq.shape
    nb = n // CHUNK
    w4 = N_QCOL * LANES

    def kv_specs(blk):
        return [pl.BlockSpec((1, CHUNK, LANES), lambda bi, i: (bi, jnp.maximum(i - 1, 0), blk)),
                pl.BlockSpec((1, CHUNK, LANES), lambda bi, i: (bi, i, blk)),
                pl.BlockSpec((1, CHUNK, LANES), lambda bi, i: (bi, jnp.minimum(i + 1, nb - 1), blk))]

    ctx_spec = pl.BlockSpec((1, kx.shape[1], LANES), lambda bi, i: (bi, 0, 0))
    return pl.pallas_call(
        _window_attn_kernel,
        grid=(b, nb),
        in_specs=[pl.BlockSpec((1, CHUNK, w4), lambda bi, i: (bi, i, 0))] + kv_specs(4) + kv_specs(5)
                 + [ctx_spec, ctx_spec, pl.BlockSpec(memory_space=pltpu.SMEM),
                    pl.BlockSpec((1, CHUNK, w4), lambda bi, i: (bi, i, 0))],
        out_specs=pl.BlockSpec((1, CHUNK, w4), lambda bi, i: (bi, i, 0)),
        out_shape=jax.ShapeDtypeStruct((b, n, w4), BF16),
        compiler_params=pltpu.CompilerParams(vmem_limit_bytes=VMEM_LIMIT),
        name="attention_window",
    )(zq, zq, zq, zq, zq, zq, zq, kx, vx, sink, zg)


def _retention_kernel(*refs, has_state_in, emit_state):
    it = iter(refs)
    q_ref, k_ref, v_ref, decf_ref, decb_ref, cg_ref, gate_ref = (next(it) for _ in range(7))
    sf_in = sb_in = sf_out = sb_out = None
    if has_state_in:
        sf_in, sb_in = next(it), next(it)
    y_ref = next(it)
    if emit_state:
        sf_out, sb_out = next(it), next(it)
    o_scr, st_scr, dm_scr = next(it), next(it), next(it)

    nc = q_ref.shape[1] // CHUNK
    ri = lax.broadcasted_iota(jnp.int32, (CHUNK, CHUNK), 0).astype(F32)
    ci = lax.broadcasted_iota(jnp.int32, (CHUNK, CHUNK), 1).astype(F32)
    col = lax.broadcasted_iota(jnp.int32, (CHUNK, 1), 0).astype(F32)

    def decay_vectors(dec_ref, forward):
        lg = -jnp.exp(dec_ref[0][:, :1])
        if forward:
            cross = jnp.exp(lg * (col + 1.0))
            kdec = jnp.exp(lg * (CHUNK - 1.0 - col))
        else:
            cross = jnp.exp(lg * (CHUNK - 1.0 - col))
            kdec = jnp.exp(lg * col)
        return lg, cross, kdec, jnp.exp(lg * CHUNK)

    def decay_matrix(lg, forward):
        rel = (ri - ci) if forward else (ci - ri - 1.0)
        ok = rel >= 0.0
        return jnp.where(ok, jnp.exp(lg * jnp.where(ok, rel, 0.0)), 0.0)

    def chunk_step(c, cross, kdec, cdec):
        rows = pl.ds(pl.multiple_of(c * CHUNK, CHUNK), CHUNK)
        qh = q_ref[0, rows, :].astype(BF16)
        kf = k_ref[0, rows, :].astype(F32)
        vh = v_ref[0, rows, :].astype(BF16)
        att = lax.dot_general(qh, kf.astype(BF16), (((1,), (1,)), ((), ())),
                              preferred_element_type=F32) * dm_scr[...]
        state = st_scr[...]
        o = (jnp.dot(att.astype(BF16), vh, preferred_element_type=F32)
             + cross * jnp.dot(qh, state.astype(BF16), preferred_element_type=F32))
        st_scr[...] = cdec * state + jnp.dot((kf * kdec).T.astype(BF16), vh, preferred_element_type=F32)
        return o, rows

    lg, cross_f, kdec_f, cdec_f = decay_vectors(decf_ref, True)
    dm_scr[...] = decay_matrix(lg, True)
    st_scr[...] = sf_in[0, 0] if has_state_in else jnp.zeros((CHUNK, CHUNK), F32)

    def fwd_body(c, _):
        o, rows = chunk_step(c, cross_f, kdec_f, cdec_f)
        o_scr[rows, :] = o
        return 0
    lax.fori_loop(0, nc, fwd_body, 0)
    if emit_state:
        sf_out[0, 0] = st_scr[...].astype(sf_out.dtype)

    lg, cross_b, kdec_b, cdec_b = decay_vectors(decb_ref, False)
    dm_scr[...] = decay_matrix(lg, False)
    st_scr[...] = sb_in[0, 0] if has_state_in else jnp.zeros((CHUNK, CHUNK), F32)

    def bwd_body(t, _):
        o, rows = chunk_step(nc - 1 - t, cross_b, kdec_b, cdec_b)
        o = o + o_scr[rows, :]
        mu = jnp.mean(o, axis=-1, keepdims=True)
        d = o - mu
        var = jnp.mean(d * d, axis=-1, keepdims=True)
        y = d * lax.rsqrt(var + EPS) * cg_ref[0]
        y_ref[0, rows, :] = (y * gate_ref[0, rows, :]).astype(y_ref.dtype)
        return 0
    lax.fori_loop(0, nc, bwd_body, 0)
    if emit_state:
        sb_out[0, 0] = st_scr[...].astype(sb_out.dtype)


def _retention(zq, zg, dec_f, dec_b, c_norm_g, states, emit_state):
    b, n, _ = zq.shape
    seq_spec = lambda blk: pl.BlockSpec((1, n, LANES), lambda bi, h: (bi, 0, blk + h))
    st_spec = pl.BlockSpec((1, 1, CHUNK, CHUNK), lambda bi, h: (bi, h, 0, 0))
    head_spec = pl.BlockSpec((1, 1, LANES), lambda bi, h: (h, 0, 0))
    bcast = lambda p: jnp.broadcast_to(p.astype(F32)[:, None, None], (C_HEADS, 1, LANES))
    in_specs = [seq_spec(0), seq_spec(C_HEADS), seq_spec(2 * C_HEADS), head_spec, head_spec, head_spec,
                seq_spec(0)]
    args = [zq, zq, zq, bcast(dec_f), bcast(dec_b), c_norm_g.astype(F32)[:, None, :], zg]
    if states is not None:
        in_specs += [st_spec, st_spec]
        args += list(states)
    out_specs = [seq_spec(0)]
    out_shape = [jax.ShapeDtypeStruct((b, n, C_HEADS * LANES), BF16)]
    if emit_state:
        out_specs += [st_spec, st_spec]
        out_shape += [jax.ShapeDtypeStruct((b, C_HEADS, CHUNK, CHUNK), F32)] * 2
    return pl.pallas_call(
        functools.partial(_retention_kernel, has_state_in=states is not None, emit_state=emit_state),
        grid=(b, C_HEADS),
        in_specs=in_specs,
        out_specs=out_specs,
        out_shape=out_shape,
        scratch_shapes=[pltpu.VMEM((n, LANES), F32),
                        pltpu.VMEM((CHUNK, CHUNK), F32),
                        pltpu.VMEM((CHUNK, CHUNK), F32)],
        compiler_params=pltpu.CompilerParams(vmem_limit_bytes=VMEM_LIMIT),
        name="retention",
    )(*args)


def _out_kernel(y1_ref, y2_ref, x_ref, mod_ref, w_ref, fg_ref, o_ref, *, final_norm):
    half = y1_ref.shape[1]
    y = (jnp.dot(y1_ref[...], w_ref[:half, :], preferred_element_type=F32)
         + jnp.dot(y2_ref[...], w_ref[half:, :], preferred_element_type=F32))
    x = x_ref[...] + mod_ref[0, 2:3, :] * y
    if final_norm:
        x = x * lax.rsqrt(jnp.mean(x * x, axis=-1, keepdims=True) + EPS) * fg_ref[...]
    o_ref[...] = x


def _out_project(y1, y2, x2d, mod, mod_row_of_tile, w_bf16, final_g, *, tm, final_norm):
    t, d = x2d.shape
    half = y1.shape[1]
    return pl.pallas_call(
        functools.partial(_out_kernel, final_norm=final_norm),
        grid=(t // tm,),
        in_specs=[pl.BlockSpec((tm, half), lambda i: (i, 0)),
                  pl.BlockSpec((tm, half), lambda i: (i, 0)),
                  pl.BlockSpec((tm, d), lambda i: (i, 0)),
                  pl.BlockSpec((1, 3, d), lambda i: (mod_row_of_tile(i), 0, 0)),
                  pl.BlockSpec(w_bf16.shape, lambda i: (0, 0)),
                  pl.BlockSpec((1, d), lambda i: (0, 0))],
        out_specs=pl.BlockSpec((tm, d), lambda i: (i, 0)),
        out_shape=jax.ShapeDtypeStruct((t, d), F32),
        compiler_params=pltpu.CompilerParams(vmem_limit_bytes=VMEM_LIMIT),
        name="out_project",
    )(y1, y2, x2d, mod, w_bf16, final_g.reshape(1, d))


def _pair_perm():
    return np.concatenate([np.arange(h * HEAD, (h + 1) * HEAD) for h in PAIR_ORDER])


def _rope_tables(n):
    rows = n // GRID_W
    row = jnp.repeat(jnp.arange(rows, dtype=F32), GRID_W)
    col = jnp.tile(jnp.arange(GRID_W, dtype=F32), rows)
    nf = HEAD // 4
    inv = ROPE_THETA ** (-jnp.arange(nf, dtype=F32) / nf)
    ang = jnp.concatenate([row[:, None] * inv, col[:, None] * inv], axis=-1)
    cos, sin = jnp.cos(ang), jnp.sin(ang)
    return jnp.tile(cos, (1, 4)), jnp.tile(jnp.concatenate([-sin, sin], axis=-1), (1, 2))


def _pair_kv(cache):
    b, g, p, d = cache.shape
    return cache.transpose(0, 2, 1, 3).reshape(b, p, g * d)


def _unpair_kv(z, blk, n_blk, heads):
    b, p, _ = z.shape
    t = z[:, :, blk * LANES:(blk + n_blk) * LANES]
    return t.reshape(b, p, heads, t.shape[-1] // heads).transpose(0, 2, 1, 3)


def kernel(x_prompt, x_sample, cache_a_k, cache_a_v, cache_b_k, cache_b_v, state_c_fwd, state_c_bwd, cache_d_k, cache_d_v, c, c_ctx, norm_g, mod_w, mod_b, ab_w_in, ab_w_out, a_sink, b_lq1, b_lk1, b_lq2, b_lk2, b_norm_g, cd_w_in, cd_w_out, c_decay_f, c_decay_b, c_norm_g, d_q_norm_g, d_k_norm_g, final_g):
    depth = norm_g.shape[0]
    bp, sp, d = x_prompt.shape
    bs, ss, _ = x_sample.shape
    dt = x_prompt.dtype

    ctx_row = bs
    pad = (-(bs + 1)) % 8
    cond = jnp.concatenate([c, c_ctx[None, :], jnp.zeros((pad, d), c.dtype)], axis=0)
    mod = _modulation(cond, mod_w, mod_b).reshape(depth, cond.shape[0], 3, d)

    perm = _pair_perm()
    qkv_w = N_QKV_BLK * LANES
    rope_tabs = _rope_tables(ss)
    bd = jnp.asarray(np.kron(np.eye(LANES // HEAD), np.ones((HEAD, HEAD))), BF16)

    tm_s = 512
    tm_p = 512
    tiles_per_seq = ss // tm_s
    row_s = lambda i: i // tiles_per_seq
    row_p = lambda i: ctx_row

    xp = x_prompt.reshape(bp * sp, d)
    xs = x_sample.reshape(bs * ss, d)
    outs = {k: [] for k in ("a_k", "a_v", "b_k", "b_v", "c_f", "c_b", "d_k", "d_v")}

    for layer in range(depth):
        i = layer // 2
        last = layer == depth - 1
        if layer % 2 == 0:
            lam_init = 0.8 - 0.6 * math.exp(-0.3 * layer)
            cols = np.arange(ab_w_in.shape[2])
            cols[0:N_QCOL * LANES] = perm
            cols[qkv_w:qkv_w + N_QCOL * LANES] = qkv_w + perm
            w_in = ab_w_in[i][:, cols].astype(BF16)
            rows = np.arange(ab_w_out.shape[1])
            rows[0:N_QCOL * LANES] = perm
            w_out = ab_w_out[i][rows, :].astype(BF16)
            diff_params = (b_lq1[i][None], b_lk1[i][None], b_lq2[i][None], b_lk2[i][None], b_norm_g[i][None])

            zq, zg = _project(xp, mod[layer], row_p, norm_g[layer], w_in, tm=tm_p, kind="ab",
                              rope_tabs=None, norm_params=None, out_dtype=F32)
            zq3, zg3 = zq.reshape(bp, sp, -1), zg.reshape(bp, sp, -1)
            ya = _dense_attention(zq3, zg3, q_blk=0, k_blk=4, v_blk=5, kv_per_col=False, gate_blk=0,
                                  mode="pair", tq=sp, tk=sp, sink=a_sink[i])
            yb = _dense_attention(zq3, zg3, q_blk=6, k_blk=10, v_blk=14, kv_per_col=True, gate_blk=4,
                                  mode="diff", tq=sp, tk=sp, diff_params=diff_params, lam_init=lam_init)
            xp = _out_project(ya.reshape(bp * sp, -1), yb.reshape(bp * sp, -1), xp, mod[layer], row_p,
                              w_out, final_g, tm=tm_p, final_norm=last)
            outs["a_k"].append(_unpair_kv(zq3, 4, 1, 2))
            outs["a_v"].append(_unpair_kv(zq3, 5, 1, 2))
            outs["b_k"].append(_unpair_kv(zq3, 10, 4, 4))
            outs["b_v"].append(_unpair_kv(zq3, 14, 4, 4))

            zq, zg, vt = _project(xs, mod[layer], row_s, norm_g[layer], w_in, tm=tm_s, kind="ab",
                                  rope_tabs=rope_tabs, norm_params=None, out_dtype=BF16,
                                  vt_blocks=(14, 15, 16, 17))
            zq3, zg3 = zq.reshape(bs, ss, -1), zg.reshape(bs, ss, -1)
            ya = _window_attention(zq3, zg3, _pair_kv(cache_a_k[:, i]), _pair_kv(cache_a_v[:, i]), a_sink[i])
            yb = _keymajor_attention(zq3, zg3, vt, cache_b_k[:, i], cache_b_v[:, i].transpose(0, 1, 3, 2),
                                     q_blk=6, k_blk=10, kv_per_col=True, gate_blk=4, mode="diff",
                                     tq=TQ_DENSE, tk=TK_DENSE, diff_params=diff_params, lam_init=lam_init)
            xs = _out_project(ya.reshape(bs * ss, -1), yb.reshape(bs * ss, -1), xs, mod[layer], row_s,
                              w_out, final_g, tm=tm_s, final_norm=last)
        else:
            cols = np.arange(cd_w_in.shape[2])
            cols[12 * LANES:16 * LANES] = 12 * LANES + perm
            cols[qkv_w + N_QCOL * LANES:qkv_w + 2 * N_QCOL * LANES] = qkv_w + N_QCOL * LANES + perm
            w_in = cd_w_in[i][:, cols].astype(BF16)
            rows = np.arange(cd_w_out.shape[1])
            rows[N_QCOL * LANES:] = N_QCOL * LANES + perm
            w_out = cd_w_out[i][rows, :].astype(BF16)
            norm_params = (jnp.tile(d_q_norm_g[i], 2)[None], jnp.tile(d_k_norm_g[i], 2)[None], bd)

            zq, zg = _project(xp, mod[layer], row_p, norm_g[layer], w_in, tm=tm_p, kind="cd",
                              rope_tabs=None, norm_params=norm_params, out_dtype=F32)
            zq3, zg3 = zq.reshape(bp, sp, -1), zg.reshape(bp, sp, -1)
            yc, s_f, s_b = _retention(zq3, zg3, c_decay_f[i], c_decay_b[i], c_norm_g[i], None, True)
            yd = _dense_attention(zq3, zg3, q_blk=12, k_blk=16, v_blk=17, kv_per_col=False, gate_blk=4,
                                  mode="pair", tq=sp, tk=sp)
            xp = _out_project(yc.reshape(bp * sp, -1), yd.reshape(bp * sp, -1), xp, mod[layer], row_p,
                              w_out, final_g, tm=tm_p, final_norm=last)
            outs["c_f"].append(s_f.astype(dt))
            outs["c_b"].append(s_b.astype(dt))
            outs["d_k"].append(_unpair_kv(zq3, 16, 1, 2))
            outs["d_v"].append(_unpair_kv(zq3, 17, 1, 2))

            zq, zg, vt = _project(xs, mod[layer], row_s, norm_g[layer], w_in, tm=tm_s, kind="cd",
                                  rope_tabs=rope_tabs, norm_params=norm_params, out_dtype=BF16, vt_blocks=(17,))
            zq3, zg3 = zq.reshape(bs, ss, -1), zg.reshape(bs, ss, -1)
            yc = _retention(zq3, zg3, c_decay_f[i], c_decay_b[i], c_norm_g[i],
                            (state_c_fwd[:, i], state_c_bwd[:, i]), False)[0]
            yd = _keymajor_attention(zq3, zg3, vt, _pair_kv(cache_d_k[:, i]),
                                     _pair_kv(cache_d_v[:, i]).transpose(0, 2, 1),
                                     q_blk=12, k_blk=16, kv_per_col=False, gate_blk=4, mode="pair",
                                     tq=TQ_DENSE, tk=TK_DENSE)
            xs = _out_project(yc.reshape(bs * ss, -1), yd.reshape(bs * ss, -1), xs, mod[layer], row_s,
                              w_out, final_g, tm=tm_s, final_norm=last)

    stack = lambda k: jnp.stack(outs[k], axis=1)
    return (xp.reshape(bp, sp, d), xs.reshape(bs, ss, d), stack("a_k"), stack("a_v"), stack("b_k"), stack("b_v"),
            stack("c_f"), stack("c_b"), stack("d_k"), stack("d_v"))
```

```python
import functools
import math

import numpy as np
import jax
import jax.numpy as jnp
from jax import lax
from jax.experimental import pallas as pl
from jax.experimental.pallas import tpu as pltpu

F32 = jnp.float32
BF16 = jnp.bfloat16

LANES = 128
HEAD = 64
GRID_W = 64
CHUNK = 128
WINDOW = 128
ROPE_THETA = 10000.0
EPS = 1e-6
NEG = -1e30
VMEM_LIMIT = 56 * 1024 * 1024
TQ_DENSE = 1024
TK_DENSE = 1024
SCORE_SLOTS = 3
RET_UNROLL = 8
ONES_ROWS = 16
LOG2E = 1.4426950408889634

GQA_HEADS = 8
PAIR_ORDER = (0, 4, 1, 5, 2, 6, 3, 7)
N_QCOL = 4
C_HEADS = 4
C_DK = 128

N_QKV_BLK = 18
N_GATE_BLK = 8
AB_OPS = ("qrope",) * 4 + ("rope", "plain") + ("qrope",) * 4 + ("rope",) * 4 + ("plain",) * 4
CD_OPS = ("plain",) * 4 + ("kscale",) * 4 + ("plain",) * 4 + ("qnorm",) * 4 + ("knorm", "plain")
AB_GROUPS = ((0, 4), (4, 6), (6, 10), (10, 14), (14, 18))
CD_GROUPS = ((0, 4), (4, 8), (8, 12), (12, 16), (16, 18))


def _silu(x):
    return x / (1.0 + jnp.exp(-x))


def _lane_lo(shape):
    return lax.broadcasted_iota(jnp.int32, shape, len(shape) - 1) % LANES < HEAD


def _mod_kernel(cond_ref, w_ref, b_ref, o_ref):
    s = _silu(cond_ref[...])
    o_ref[0] = jnp.dot(s.astype(BF16), w_ref[0].astype(BF16), preferred_element_type=F32) + b_ref[0]


def _modulation(cond, mod_w, mod_b):
    depth, d, d3 = mod_w.shape
    rows = cond.shape[0]
    return pl.pallas_call(
        _mod_kernel,
        grid=(depth, d3 // d),
        in_specs=[pl.BlockSpec((rows, d), lambda l, j: (0, 0)),
                  pl.BlockSpec((1, d, d), lambda l, j: (l, 0, j)),
                  pl.BlockSpec((1, 1, d), lambda l, j: (l, 0, j))],
        out_specs=pl.BlockSpec((1, rows, d), lambda l, j: (l, 0, j)),
        out_shape=jax.ShapeDtypeStruct((depth, rows, d3), F32),
        compiler_params=pltpu.CompilerParams(vmem_limit_bytes=VMEM_LIMIT),
        name="modulation",
    )(cond, mod_w, mod_b.reshape(depth, 1, d3))


def _rot_half(x, first_half):
    return jnp.where(first_half, pltpu.roll(x, LANES - HEAD // 2, 1), pltpu.roll(x, HEAD // 2, 1))


def _head_sumsq(x, bd):
    sq = x * x
    hi = sq.astype(BF16)
    lo = (sq - hi.astype(F32)).astype(BF16)
    return jnp.dot(hi, bd, preferred_element_type=F32) + jnp.dot(lo, bd, preferred_element_type=F32)


def _proj_kernel(*refs, ops, groups, use_rope, has_norm, vt_blocks):
    it = iter(refs)
    x_ref, mod_ref, g_ref, w_ref = next(it), next(it), next(it), next(it)
    cos_ref = sin_ref = qg_ref = kg_ref = bd_ref = vt_ref = None
    if use_rope:
        cos_ref, sin_ref = next(it), next(it)
    if has_norm:
        qg_ref, kg_ref, bd_ref = next(it), next(it), next(it)
    zq_ref, zg_ref = next(it), next(it)
    if vt_blocks:
        vt_ref = next(it)

    x = x_ref[...]
    h = x * lax.rsqrt(jnp.mean(x * x, axis=-1, keepdims=True) + EPS) * g_ref[...]
    h = h * (1.0 + mod_ref[0, 1:2, :]) + mod_ref[0, 0:1, :]
    hb = h.astype(BF16)

    tm = x.shape[0]
    if use_rope:
        cos, sin = cos_ref[...], sin_ref[...]
        first_half = lax.broadcasted_iota(jnp.int32, (tm, LANES), 1) % HEAD < HEAD // 2

    def rope(z):
        if not use_rope:
            return z
        return z * cos + _rot_half(z, first_half) * sin

    def head_norm(z, gain_ref):
        ss = _head_sumsq(z, bd_ref[...])
        return z * lax.rsqrt(ss * (1.0 / HEAD) + EPS) * gain_ref[...]

    q_scale = HEAD ** -0.5 * LOG2E
    for b0, b1 in groups:
        z = jnp.dot(hb, w_ref[:, b0 * LANES:b1 * LANES], preferred_element_type=F32)
        for j in range(b0, b1):
            zz = z[:, (j - b0) * LANES:(j - b0 + 1) * LANES]
            op = ops[j]
            if op == "qrope":
                zz = rope(zz * q_scale)
            elif op == "rope":
                zz = rope(zz)
            elif op == "kscale":
                zz = zz * (C_DK ** -0.5)
            elif op == "qnorm":
                zz = rope(head_norm(zz, qg_ref)) * q_scale
            elif op == "knorm":
                zz = rope(head_norm(zz, kg_ref))
            zq_ref[:, j * LANES:(j + 1) * LANES] = zz.astype(zq_ref.dtype)
            if j in vt_blocks:
                vt_ref[vt_blocks.index(j)] = zz.T.astype(vt_ref.dtype)
    half = N_GATE_BLK // 2
    for g0 in (0, half):
        c0 = (N_QKV_BLK + g0) * LANES
        z = jnp.dot(hb, w_ref[:, c0:c0 + half * LANES], preferred_element_type=F32)
        zg_ref[:, g0 * LANES:(g0 + half) * LANES] = _silu(z)


def _project(x2d, mod, mod_row_of_tile, norm_g, w_bf16, *, tm, kind, rope_tabs, norm_params, out_dtype,
             vt_blocks=()):
    t, d = x2d.shape
    out_specs = [pl.BlockSpec((tm, N_QKV_BLK * LANES), lambda i: (i, 0)),
                 pl.BlockSpec((tm, N_GATE_BLK * LANES), lambda i: (i, 0))]
    out_shape = [jax.ShapeDtypeStruct((t, N_QKV_BLK * LANES), out_dtype),
                 jax.ShapeDtypeStruct((t, N_GATE_BLK * LANES), F32)]
    if vt_blocks:
        out_specs.append(pl.BlockSpec((len(vt_blocks), LANES, tm), lambda i: (0, 0, i)))
        out_shape.append(jax.ShapeDtypeStruct((len(vt_blocks), LANES, t), out_dtype))
    use_rope = rope_tabs is not None
    has_norm = kind == "cd"
    ops, groups = (AB_OPS, AB_GROUPS) if kind == "ab" else (CD_OPS, CD_GROUPS)
    in_specs = [pl.BlockSpec((tm, d), lambda i: (i, 0)),
                pl.BlockSpec((1, 3, d), lambda i: (mod_row_of_tile(i), 0, 0)),
                pl.BlockSpec((1, d), lambda i: (0, 0)),
                pl.BlockSpec(w_bf16.shape, lambda i: (0, 0))]
    args = [x2d, mod, norm_g.reshape(1, d), w_bf16]
    if use_rope:
        n_seq_tiles = rope_tabs[0].shape[0] // tm
        in_specs += [pl.BlockSpec((tm, LANES), lambda i: (i % n_seq_tiles, 0))] * 2
        args += list(rope_tabs)
    if has_norm:
        in_specs += [pl.BlockSpec((1, LANES), lambda i: (0, 0))] * 2 + [pl.BlockSpec((LANES, LANES), lambda i: (0, 0))]
        args += list(norm_params)
    return pl.pallas_call(
        functools.partial(_proj_kernel, ops=ops, groups=groups, use_rope=use_rope, has_norm=has_norm,
                          vt_blocks=tuple(vt_blocks)),
        grid=(t // tm,),
        in_specs=in_specs,
        out_specs=out_specs,
        out_shape=out_shape,
        compiler_params=pltpu.CompilerParams(vmem_limit_bytes=VMEM_LIMIT),
        name="project_" + kind,
    )(*args)


def _pair_rows(q):
    lo = jnp.where(_lane_lo((1, LANES)), 1.0, 0.0).astype(BF16)
    qb = q.astype(BF16)
    return jnp.concatenate([qb * lo, qb * (1.0 - lo).astype(BF16)], axis=0)


def _softmax_step(qrows, k, v, m, l, acc, mask=None):
    s = lax.dot_general(qrows, k, (((1,), (1,)), ((), ())), preferred_element_type=F32)
    if mask is not None:
        s = jnp.where(mask, s, NEG)
    m_new = jnp.maximum(m, jnp.max(s, axis=-1, keepdims=True))
    alpha = jnp.exp2(m - m_new)
    p = jnp.exp2(s - m_new)
    l = alpha * l + jnp.sum(p, axis=-1, keepdims=True)
    acc = alpha * acc + jnp.dot(p.astype(BF16), v, preferred_element_type=F32)
    return m_new, l, acc


def _dense_attn_kernel(*refs, mode, has_sink, has_ctx, tk, lam_init):
    it = iter(refs)
    q_ref, k_ref, v_ref = next(it), next(it), next(it)
    kx_ref = vx_ref = sink_ref = None
    if has_ctx:
        kx_ref, vx_ref = next(it), next(it)
    if has_sink:
        sink_ref = next(it)
    if mode == "diff":
        lq1_ref, lk1_ref, lq2_ref, lk2_ref, bg_ref = next(it), next(it), next(it), next(it), next(it)
    gate_ref, y_ref = next(it), next(it)

    tq = q_ref.shape[1]
    qrows = _pair_rows(q_ref[0])
    if has_sink:
        c = pl.program_id(1)
        m0 = jnp.concatenate([jnp.full((tq, 1), sink_ref[c] * LOG2E, F32),
                              jnp.full((tq, 1), sink_ref[c + N_QCOL] * LOG2E, F32)], axis=0)
        l0 = jnp.ones((2 * tq, 1), F32)
    else:
        m0 = jnp.full((2 * tq, 1), NEG, F32)
        l0 = jnp.zeros((2 * tq, 1), F32)
    carry = (m0, l0, jnp.zeros((2 * tq, LANES), F32))

    n_keys = k_ref.shape[1]
    if n_keys == tk:
        carry = _softmax_step(qrows, k_ref[0].astype(BF16), v_ref[0].astype(BF16), *carry)
    else:
        def body(j, cr):
            rows = pl.ds(pl.multiple_of(j * tk, tk), tk)
            return _softmax_step(qrows, k_ref[0, rows, :].astype(BF16), v_ref[0, rows, :].astype(BF16), *cr)
        carry = lax.fori_loop(0, n_keys // tk, body, carry)
    if has_ctx:
        carry = _softmax_step(qrows, kx_ref[0].astype(BF16), vx_ref[0].astype(BF16), *carry)
    _, l, acc = carry
    o2 = acc / l
    if mode == "pair":
        o = jnp.where(_lane_lo((tq, LANES)), o2[:tq], o2[tq:])
    else:
        lam = (jnp.exp(jnp.sum(lq1_ref[...] * lk1_ref[...], axis=-1, keepdims=True))
               - jnp.exp(jnp.sum(lq2_ref[...] * lk2_ref[...], axis=-1, keepdims=True)) + lam_init)
        o = o2[:tq] - lam * o2[tq:]
        o = o * lax.rsqrt(jnp.mean(o * o, axis=-1, keepdims=True) + EPS) * bg_ref[...] * (1.0 - lam_init)
    y_ref[0] = (o * gate_ref[0]).astype(y_ref.dtype)


def _dense_attention(zq, zg, *, q_blk, k_blk, v_blk, kv_per_col, gate_blk, mode, tq, tk,
                     ctx_kv=None, sink=None, diff_params=None, lam_init=0.0):
    b, n, _ = zq.shape
    kb = (lambda c: k_blk + c) if kv_per_col else (lambda c: k_blk)
    vb = (lambda c: v_blk + c) if kv_per_col else (lambda c: v_blk)
    in_specs = [pl.BlockSpec((1, tq, LANES), lambda bi, c, qi: (bi, qi, q_blk + c)),
                pl.BlockSpec((1, n, LANES), lambda bi, c, qi: (bi, 0, kb(c))),
                pl.BlockSpec((1, n, LANES), lambda bi, c, qi: (bi, 0, vb(c)))]
    args = [zq, zq, zq]
    if ctx_kv is not None:
        kx, vx = ctx_kv
        if kx.ndim == 4:
            spec = pl.BlockSpec((None, 1, kx.shape[2], LANES), lambda bi, c, qi: (bi, c, 0, 0))
        else:
            spec = pl.BlockSpec((1, kx.shape[1], LANES), lambda bi, c, qi: (bi, 0, 0))
        in_specs += [spec, spec]
        args += [kx, vx]
    if sink is not None:
        in_specs.append(pl.BlockSpec(memory_space=pltpu.SMEM))
        args.append(sink)
    if mode == "diff":
        in_specs += [pl.BlockSpec((1, HEAD), lambda bi, c, qi: (0, 0))] * 4
        in_specs.append(pl.BlockSpec((1, LANES), lambda bi, c, qi: (0, 0)))
        args += list(diff_params)
    in_specs.append(pl.BlockSpec((1, tq, LANES), lambda bi, c, qi: (bi, qi, gate_blk + c)))
    args.append(zg)
    return pl.pallas_call(
        functools.partial(_dense_attn_kernel, mode=mode, has_sink=sink is not None,
                          has_ctx=ctx_kv is not None, tk=tk, lam_init=lam_init),
        grid=(b, N_QCOL, n // tq),
        in_specs=in_specs,
        out_specs=pl.BlockSpec((1, tq, LANES), lambda bi, c, qi: (bi, qi, c)),
        out_shape=jax.ShapeDtypeStruct((b, n, N_QCOL * LANES), BF16),
        compiler_params=pltpu.CompilerParams(vmem_limit_bytes=VMEM_LIMIT),
        name="attention_" + mode,
    )(*args)


def _keymajor_attn_kernel(*refs, mode, tk, lam_init):
    it = iter(refs)
    q_ref, k_ref, vt_ref, kx_ref, vxt_ref = (next(it) for _ in range(5))
    if mode == "diff":
        lq1_ref, lk1_ref, lq2_ref, lk2_ref, bg_ref = (next(it) for _ in range(5))
    gate_ref, y_ref, s_scr, sx_scr = next(it), next(it), next(it), next(it)

    tq = q_ref.shape[1]
    qrows = _pair_rows(q_ref[0])

    def scores(k, slot):
        s = lax.dot_general(k, qrows, (((1,), (1,)), ((), ())), preferred_element_type=F32)
        slot[...] = s
        return jnp.max(s, axis=0, keepdims=True)

    vd = HEAD if mode == "pair" else LANES

    def with_ones(vt):
        return jnp.concatenate([vt, jnp.ones((ONES_ROWS, vt.shape[1]), BF16)], axis=0)

    def consume(slot, cmax, vt, m, acc_lo, acc_hi):
        m_new = jnp.maximum(m, cmax)
        alpha = jnp.exp2(m - m_new)
        p = jnp.exp2(slot[...] - m_new).astype(BF16)
        if mode == "pair":
            v_lo, v_hi = with_ones(vt[:HEAD]), with_ones(vt[HEAD:])
        else:
            v_lo = v_hi = with_ones(vt)
        acc_lo = alpha[:, :tq] * acc_lo + jnp.dot(v_lo, p[:, :tq], preferred_element_type=F32)
        acc_hi = alpha[:, tq:] * acc_hi + jnp.dot(v_hi, p[:, tq:], preferred_element_type=F32)
        return m_new, acc_lo, acc_hi

    n_chunks = k_ref.shape[1] // tk
    carry = (jnp.full((1, 2 * tq), NEG, F32), jnp.zeros((vd + ONES_ROWS, tq), F32),
             jnp.zeros((vd + ONES_ROWS, tq), F32))
    n_slots = s_scr.shape[0]
    ahead = n_slots - 1
    chunk_scores = lambda j: scores(k_ref[0, j * tk:(j + 1) * tk, :], s_scr.at[j % n_slots])
    cmax = [chunk_scores(j) for j in range(min(ahead, n_chunks))]
    cmax_x = scores(kx_ref[0].astype(BF16), sx_scr)
    for j in range(n_chunks):
        if j + ahead < n_chunks:
            cmax.append(chunk_scores(j + ahead))
        carry = consume(s_scr.at[j % n_slots], cmax[j], vt_ref[0, :, j * tk:(j + 1) * tk], *carry)
    _, acc_lo, acc_hi = consume(sx_scr, cmax_x, vxt_ref[0].astype(BF16), *carry)

    ot_lo = acc_lo[:vd] / acc_lo[vd:vd + 1]
    ot_hi = acc_hi[:vd] / acc_hi[vd:vd + 1]
    if mode == "pair":
        o = jnp.concatenate([ot_lo, ot_hi], axis=0).T
    else:
        lam = (jnp.exp(jnp.sum(lq1_ref[...] * lk1_ref[...], axis=-1, keepdims=True))
               - jnp.exp(jnp.sum(lq2_ref[...] * lk2_ref[...], axis=-1, keepdims=True)) + lam_init)
        o = (ot_lo - lam * ot_hi).T
        o = o * lax.rsqrt(jnp.mean(o * o, axis=-1, keepdims=True) + EPS) * bg_ref[...] * (1.0 - lam_init)
    y_ref[0] = (o * gate_ref[0]).astype(y_ref.dtype)


def _keymajor_attention(zq, zg, vt, kx, vxt, *, q_blk, k_blk, kv_per_col, gate_blk, mode, tq, tk,
                        diff_params=None, lam_init=0.0):
    b, n, _ = zq.shape
    per = (lambda c: c) if kv_per_col else (lambda c: 0)
    if kx.ndim == 4:
        kx_spec = pl.BlockSpec((None, 1) + kx.shape[2:], lambda bi, c, qi: (bi, c, 0, 0))
        vx_spec = pl.BlockSpec((None, 1) + vxt.shape[2:], lambda bi, c, qi: (bi, c, 0, 0))
    else:
        kx_spec = pl.BlockSpec((1,) + kx.shape[1:], lambda bi, c, qi: (bi, 0, 0))
        vx_spec = pl.BlockSpec((1,) + vxt.shape[1:], lambda bi, c, qi: (bi, 0, 0))
    in_specs = [pl.BlockSpec((1, tq, LANES), lambda bi, c, qi: (bi, qi, q_blk + c)),
                pl.BlockSpec((1, n, LANES), lambda bi, c, qi: (bi, 0, k_blk + per(c))),
                pl.BlockSpec((1, LANES, n), lambda bi, c, qi: (per(c), 0, bi)),
                kx_spec, vx_spec]
    args = [zq, zq, vt, kx, vxt]
    if mode == "diff":
        in_specs += [pl.BlockSpec((1, HEAD), lambda bi, c, qi: (0, 0))] * 4
        in_specs.append(pl.BlockSpec((1, LANES), lambda bi, c, qi: (0, 0)))
        args += list(diff_params)
    in_specs.append(pl.BlockSpec((1, tq, LANES), lambda bi, c, qi: (bi, qi, gate_blk + c)))
    args.append(zg)
    return pl.pallas_call(
        functools.partial(_keymajor_attn_kernel, mode=mode, tk=tk, lam_init=lam_init),
        grid=(b, N_QCOL, n // tq),
        in_specs=in_specs,
        out_specs=pl.BlockSpec((1, tq, LANES), lambda bi, c, qi: (bi, qi, c)),
        out_shape=jax.ShapeDtypeStruct((b, n, N_QCOL * LANES), BF16),
        scratch_shapes=[pltpu.VMEM((SCORE_SLOTS, tk, 2 * tq), F32), pltpu.VMEM((kx.shape[-2], 2 * tq), F32)],
        compiler_params=pltpu.CompilerParams(vmem_limit_bytes=VMEM_LIMIT),
        name="attention_keymajor_" + mode,
    )(*args)


def _window_attn_kernel(q_ref, kp_ref, kc_ref, kn_ref, vp_ref, vc_ref, vn_ref, kx_ref, vx_ref,
                        sink_ref, gate_ref, y_ref):
    i = pl.program_id(1)
    n_blocks = pl.num_programs(1)
    q = q_ref[0]
    qrows = jnp.concatenate([_pair_rows(q[:, c * LANES:(c + 1) * LANES]) for c in range(N_QCOL)], axis=0)
    k = jnp.concatenate([kp_ref[0], kc_ref[0], kn_ref[0], kx_ref[0].astype(BF16)], axis=0)
    v = jnp.concatenate([vp_ref[0], vc_ref[0], vn_ref[0], vx_ref[0].astype(BF16)], axis=0)
    s = lax.dot_general(qrows, k, (((1,), (1,)), ((), ())), preferred_element_type=F32)

    n_keys = k.shape[0]
    r = lax.broadcasted_iota(jnp.int32, (CHUNK, n_keys), 0)
    m = lax.broadcasted_iota(jnp.int32, (CHUNK, n_keys), 1)
    dist = m - CHUNK - r
    first_m = jnp.where(i > 0, 0, CHUNK)
    end_m = jnp.where(i < n_blocks - 1, 3 * CHUNK, 2 * CHUNK)
    valid = ((dist >= -WINDOW) & (dist <= WINDOW) & (m >= first_m) & (m < end_m)) | (m >= 3 * CHUNK)

    outs = []
    for hrow in range(GQA_HEADS):
        sh = jnp.where(valid, s[hrow * CHUNK:(hrow + 1) * CHUNK], NEG)
        sk = sink_ref[PAIR_ORDER[hrow]] * LOG2E
        mx = jnp.maximum(jnp.max(sh, axis=-1, keepdims=True), sk)
        p = jnp.exp2(sh - mx)
        l = jnp.sum(p, axis=-1, keepdims=True) + jnp.exp2(sk - mx)
        outs.append(jnp.dot(p.astype(BF16), v, preferred_element_type=F32) / l)
    lo = _lane_lo((CHUNK, LANES))
    o = jnp.concatenate([jnp.where(lo, outs[2 * c], outs[2 * c + 1]) for c in range(N_QCOL)], axis=1)
    y_ref[0] = (o * gate_ref[0]).astype(y_ref.dtype)


def _window_attention(zq, zg, kx, vx, sink):
    b, n, _ = zq.shape
    nb = n // CHUNK
    w4 = N_QCOL * LANES

    def kv_specs(blk):
        return [pl.BlockSpec((1, CHUNK, LANES), lambda bi, i: (bi, jnp.maximum(i - 1, 0), blk)),
                pl.BlockSpec((1, CHUNK, LANES), lambda bi, i: (bi, i, blk)),
                pl.BlockSpec((1, CHUNK, LANES), lambda bi, i: (bi, jnp.minimum(i + 1, nb - 1), blk))]

    ctx_spec = pl.BlockSpec((1, kx.shape[1], LANES), lambda bi, i: (bi, 0, 0))
    return pl.pallas_call(
        _window_attn_kernel,
        grid=(b, nb),
        in_specs=[pl.BlockSpec((1, CHUNK, w4), lambda bi, i: (bi, i, 0))] + kv_specs(4) + kv_specs(5)
                 + [ctx_spec, ctx_spec, pl.BlockSpec(memory_space=pltpu.SMEM),
                    pl.BlockSpec((1, CHUNK, w4), lambda bi, i: (bi, i, 0))],
        out_specs=pl.BlockSpec((1, CHUNK, w4), lambda bi, i: (bi, i, 0)),
        out_shape=jax.ShapeDtypeStruct((b, n, w4), BF16),
        compiler_params=pltpu.CompilerParams(vmem_limit_bytes=VMEM_LIMIT),
        name="attention_window",
    )(zq, zq, zq, zq, zq, zq, zq, kx, vx, sink, zg)


def _retention_kernel(*refs, has_state_in, emit_state):
    it = iter(refs)
    q_ref, k_ref, v_ref, decf_ref, decb_ref, cg_ref, gate_ref = (next(it) for _ in range(7))
    sf_in = sb_in = sf_out = sb_out = None
    if has_state_in:
        sf_in, sb_in = next(it), next(it)
    y_ref = next(it)
    if emit_state:
        sf_out, sb_out = next(it), next(it)
    o_scr, u_scr, dm_scr = next(it), next(it), next(it)

    nc = q_ref.shape[1] // CHUNK
    ri = lax.broadcasted_iota(jnp.int32, (CHUNK, CHUNK), 0).astype(F32)
    ci = lax.broadcasted_iota(jnp.int32, (CHUNK, CHUNK), 1).astype(F32)
    tok_col = lax.broadcasted_iota(jnp.int32, (CHUNK, 1), 0).astype(F32)
    tok_row = lax.broadcasted_iota(jnp.int32, (1, CHUNK), 1).astype(F32)

    lg_f = -jnp.exp(decf_ref[0][:, :1])
    lg_b = -jnp.exp(decb_ref[0][:, :1])
    cross_f, cross_b = jnp.exp(lg_f * (tok_col + 1.0)), jnp.exp(lg_b * (CHUNK - 1.0 - tok_col))
    kdec_f, kdec_b = jnp.exp(lg_f * (CHUNK - 1.0 - tok_row)), jnp.exp(lg_b * tok_row)
    cdec_f, cdec_b = jnp.exp(lg_f * CHUNK), jnp.exp(lg_b * CHUNK)
    rel_f, rel_b = ri - ci, ci - ri - 1.0
    dm_scr[...] = jnp.where(rel_f >= 0.0, jnp.exp(lg_f * jnp.maximum(rel_f, 0.0)),
                            jnp.exp(lg_b * jnp.maximum(rel_b, 0.0)))

    def chunk_rows(c):
        return pl.ds(pl.multiple_of(c * CHUNK, CHUNK), CHUNK)

    def local_body(c, _):
        rows = chunk_rows(c)
        qh = q_ref[0, rows, :].astype(BF16)
        kf = k_ref[0, rows, :].astype(F32)
        vh = v_ref[0, rows, :].astype(BF16)
        att = lax.dot_general(qh, kf.astype(BF16), (((1,), (1,)), ((), ())),
                              preferred_element_type=F32) * dm_scr[...]
        kt = kf.T
        lhs = jnp.concatenate([att.astype(BF16), (kt * kdec_f).astype(BF16), (kt * kdec_b).astype(BF16)], axis=0)
        r = jnp.dot(lhs, vh, preferred_element_type=F32)
        o_scr[rows, :] = r[:CHUNK]
        u_scr[c, 0] = r[CHUNK:2 * CHUNK]
        u_scr[c, 1] = r[2 * CHUNK:]
        return 0
    lax.fori_loop(0, nc, local_body, 0, unroll=min(nc, RET_UNROLL))

    def scan_body(t, states):
        s_f, s_b = states
        inc_f, inc_b = u_scr[t, 0], u_scr[nc - 1 - t, 1]
        u_scr[t, 0] = s_f
        u_scr[nc - 1 - t, 1] = s_b
        return cdec_f * s_f + inc_f, cdec_b * s_b + inc_b
    zero = jnp.zeros((CHUNK, CHUNK), F32)
    s_f, s_b = lax.fori_loop(0, nc, scan_body, (sf_in[0, 0], sb_in[0, 0]) if has_state_in else (zero, zero),
                             unroll=2)
    if emit_state:
        sf_out[0, 0] = s_f.astype(sf_out.dtype)
        sb_out[0, 0] = s_b.astype(sb_out.dtype)

    def finish_body(c, _):
        rows = chunk_rows(c)
        qh = q_ref[0, rows, :].astype(BF16)
        states = jnp.concatenate([u_scr[c, 0], u_scr[c, 1]], axis=1).astype(BF16)
        r = jnp.dot(qh, states, preferred_element_type=F32)
        o = o_scr[rows, :] + cross_f * r[:, :LANES] + cross_b * r[:, LANES:]
        mu = jnp.mean(o, axis=-1, keepdims=True)
        d = o - mu
        var = jnp.mean(d * d, axis=-1, keepdims=True)
        y = d * lax.rsqrt(var + EPS) * cg_ref[0]
        y_ref[0, rows, :] = (y * gate_ref[0, rows, :]).astype(y_ref.dtype)
        return 0
    lax.fori_loop(0, nc, finish_body, 0, unroll=min(nc, RET_UNROLL))


def _retention(zq, zg, dec_f, dec_b, c_norm_g, states, emit_state):
    b, n, _ = zq.shape
    seq_spec = lambda blk: pl.BlockSpec((1, n, LANES), lambda bi, h: (bi, 0, blk + h))
    st_spec = pl.BlockSpec((1, 1, CHUNK, CHUNK), lambda bi, h: (bi, h, 0, 0))
    head_spec = pl.BlockSpec((1, 1, LANES), lambda bi, h: (h, 0, 0))
    bcast = lambda p: jnp.broadcast_to(p.astype(F32)[:, None, None], (C_HEADS, 1, LANES))
    in_specs = [seq_spec(0), seq_spec(C_HEADS), seq_spec(2 * C_HEADS), head_spec, head_spec, head_spec,
                seq_spec(0)]
    args = [zq, zq, zq, bcast(dec_f), bcast(dec_b), c_norm_g.astype(F32)[:, None, :], zg]
    if states is not None:
        in_specs += [st_spec, st_spec]
        args += list(states)
    out_specs = [seq_spec(0)]
    out_shape = [jax.ShapeDtypeStruct((b, n, C_HEADS * LANES), BF16)]
    if emit_state:
        out_specs += [st_spec, st_spec]
        out_shape += [jax.ShapeDtypeStruct((b, C_HEADS, CHUNK, CHUNK), F32)] * 2
    return pl.pallas_call(
        functools.partial(_retention_kernel, has_state_in=states is not None, emit_state=emit_state),
        grid=(b, C_HEADS),
        in_specs=in_specs,
        out_specs=out_specs,
        out_shape=out_shape,
        scratch_shapes=[pltpu.VMEM((n, LANES), F32),
                        pltpu.VMEM((n // CHUNK, 2, CHUNK, CHUNK), F32),
                        pltpu.VMEM((CHUNK, CHUNK), F32)],
        compiler_params=pltpu.CompilerParams(vmem_limit_bytes=VMEM_LIMIT),
        name="retention",
    )(*args)


def _out_kernel(y1_ref, y2_ref, x_ref, mod_ref, w_ref, fg_ref, o_ref, *, final_norm):
    half = y1_ref.shape[1]
    y = (jnp.dot(y1_ref[...], w_ref[:half, :], preferred_element_type=F32)
         + jnp.dot(y2_ref[...], w_ref[half:, :], preferred_element_type=F32))
    x = x_ref[...] + mod_ref[0, 2:3, :] * y
    if final_norm:
        x = x * lax.rsqrt(jnp.mean(x * x, axis=-1, keepdims=True) + EPS) * fg_ref[...]
    o_ref[...] = x


def _out_project(y1, y2, x2d, mod, mod_row_of_tile, w_bf16, final_g, *, tm, final_norm):
    t, d = x2d.shape
    half = y1.shape[1]
    return pl.pallas_call(
        functools.partial(_out_kernel, final_norm=final_norm),
        grid=(t // tm,),
        in_specs=[pl.BlockSpec((tm, half), lambda i: (i, 0)),
                  pl.BlockSpec((tm, half), lambda i: (i, 0)),
                  pl.BlockSpec((tm, d), lambda i: (i, 0)),
                  pl.BlockSpec((1, 3, d), lambda i: (mod_row_of_tile(i), 0, 0)),
                  pl.BlockSpec(w_bf16.shape, lambda i: (0, 0)),
                  pl.BlockSpec((1, d), lambda i: (0, 0))],
        out_specs=pl.BlockSpec((tm, d), lambda i: (i, 0)),
        out_shape=jax.ShapeDtypeStruct((t, d), F32),
        compiler_params=pltpu.CompilerParams(vmem_limit_bytes=VMEM_LIMIT),
        name="out_project",
    )(y1, y2, x2d, mod, w_bf16, final_g.reshape(1, d))


def _pair_perm():
    return np.concatenate([np.arange(h * HEAD, (h + 1) * HEAD) for h in PAIR_ORDER])


def _rope_tables(n):
    rows = n // GRID_W
    row = jnp.repeat(jnp.arange(rows, dtype=F32), GRID_W)
    col = jnp.tile(jnp.arange(GRID_W, dtype=F32), rows)
    nf = HEAD // 4
    inv = ROPE_THETA ** (-jnp.arange(nf, dtype=F32) / nf)
    ang = jnp.concatenate([row[:, None] * inv, col[:, None] * inv], axis=-1)
    cos, sin = jnp.cos(ang), jnp.sin(ang)
    return jnp.tile(cos, (1, 4)), jnp.tile(jnp.concatenate([-sin, sin], axis=-1), (1, 2))


def _pair_kv(cache):
    b, g, p, d = cache.shape
    return cache.transpose(0, 2, 1, 3).reshape(b, p, g * d)


def _unpair_kv(z, blk, n_blk, heads):
    b, p, _ = z.shape
    t = z[:, :, blk * LANES:(blk + n_blk) * LANES]
    return t.reshape(b, p, heads, t.shape[-1] // heads).transpose(0, 2, 1, 3)


def kernel(x_prompt, x_sample, cache_a_k, cache_a_v, cache_b_k, cache_b_v, state_c_fwd, state_c_bwd, cache_d_k, cache_d_v, c, c_ctx, norm_g, mod_w, mod_b, ab_w_in, ab_w_out, a_sink, b_lq1, b_lk1, b_lq2, b_lk2, b_norm_g, cd_w_in, cd_w_out, c_decay_f, c_decay_b, c_norm_g, d_q_norm_g, d_k_norm_g, final_g):
    depth = norm_g.shape[0]
    bp, sp, d = x_prompt.shape
    bs, ss, _ = x_sample.shape
    dt = x_prompt.dtype

    ctx_row = bs
    pad = (-(bs + 1)) % 8
    cond = jnp.concatenate([c, c_ctx[None, :], jnp.zeros((pad, d), c.dtype)], axis=0)
    mod = _modulation(cond, mod_w, mod_b).reshape(depth, cond.shape[0], 3, d)

    perm = _pair_perm()
    qkv_w = N_QKV_BLK * LANES
    rope_tabs = _rope_tables(ss)
    bd = jnp.asarray(np.kron(np.eye(LANES // HEAD), np.ones((HEAD, HEAD))), BF16)

    tm_s = 512
    tm_p = 512
    tiles_per_seq = ss // tm_s
    row_s = lambda i: i // tiles_per_seq
    row_p = lambda i: ctx_row

    xp = x_prompt.reshape(bp * sp, d)
    xs = x_sample.reshape(bs * ss, d)
    outs = {k: [] for k in ("a_k", "a_v", "b_k", "b_v", "c_f", "c_b", "d_k", "d_v")}

    for layer in range(depth):
        i = layer // 2
        last = layer == depth - 1
        if layer % 2 == 0:
            lam_init = 0.8 - 0.6 * math.exp(-0.3 * layer)
            cols = np.arange(ab_w_in.shape[2])
            cols[0:N_QCOL * LANES] = perm
            cols[qkv_w:qkv_w + N_QCOL * LANES] = qkv_w + perm
            w_in = ab_w_in[i][:, cols].astype(BF16)
            rows = np.arange(ab_w_out.shape[1])
            rows[0:N_QCOL * LANES] = perm
            w_out = ab_w_out[i][rows, :].astype(BF16)
            diff_params = (b_lq1[i][None], b_lk1[i][None], b_lq2[i][None], b_lk2[i][None], b_norm_g[i][None])

            zq, zg = _project(xp, mod[layer], row_p, norm_g[layer], w_in, tm=tm_p, kind="ab",
                              rope_tabs=None, norm_params=None, out_dtype=F32)
            zq3, zg3 = zq.reshape(bp, sp, -1), zg.reshape(bp, sp, -1)
            ya = _dense_attention(zq3, zg3, q_blk=0, k_blk=4, v_blk=5, kv_per_col=False, gate_blk=0,
                                  mode="pair", tq=sp, tk=sp, sink=a_sink[i])
            yb = _dense_attention(zq3, zg3, q_blk=6, k_blk=10, v_blk=14, kv_per_col=True, gate_blk=4,
                                  mode="diff", tq=sp, tk=sp, diff_params=diff_params, lam_init=lam_init)
            xp = _out_project(ya.reshape(bp * sp, -1), yb.reshape(bp * sp, -1), xp, mod[layer], row_p,
                              w_out, final_g, tm=tm_p, final_norm=last)
            outs["a_k"].append(_unpair_kv(zq3, 4, 1, 2))
            outs["a_v"].append(_unpair_kv(zq3, 5, 1, 2))
            outs["b_k"].append(_unpair_kv(zq3, 10, 4, 4))
            outs["b_v"].append(_unpair_kv(zq3, 14, 4, 4))

            zq, zg, vt = _project(xs, mod[layer], row_s, norm_g[layer], w_in, tm=tm_s, kind="ab",
                                  rope_tabs=rope_tabs, norm_params=None, out_dtype=BF16,
                                  vt_blocks=(14, 15, 16, 17))
            zq3, zg3 = zq.reshape(bs, ss, -1), zg.reshape(bs, ss, -1)
            ya = _window_attention(zq3, zg3, _pair_kv(cache_a_k[:, i]), _pair_kv(cache_a_v[:, i]), a_sink[i])
            yb = _keymajor_attention(zq3, zg3, vt, cache_b_k[:, i], cache_b_v[:, i].transpose(0, 1, 3, 2),
                                     q_blk=6, k_blk=10, kv_per_col=True, gate_blk=4, mode="diff",
                                     tq=TQ_DENSE, tk=TK_DENSE, diff_params=diff_params, lam_init=lam_init)
            xs = _out_project(ya.reshape(bs * ss, -1), yb.reshape(bs * ss, -1), xs, mod[layer], row_s,
                              w_out, final_g, tm=tm_s, final_norm=last)
        else:
            cols = np.arange(cd_w_in.shape[2])
            cols[12 * LANES:16 * LANES] = 12 * LANES + perm
            cols[qkv_w + N_QCOL * LANES:qkv_w + 2 * N_QCOL * LANES] = qkv_w + N_QCOL * LANES + perm
            w_in = cd_w_in[i][:, cols].astype(BF16)
            rows = np.arange(cd_w_out.shape[1])
            rows[N_QCOL * LANES:] = N_QCOL * LANES + perm
            w_out = cd_w_out[i][rows, :].astype(BF16)
            norm_params = (jnp.tile(d_q_norm_g[i], 2)[None], jnp.tile(d_k_norm_g[i], 2)[None], bd)

            zq, zg = _project(xp, mod[layer], row_p, norm_g[layer], w_in, tm=tm_p, kind="cd",
                              rope_tabs=None, norm_params=norm_params, out_dtype=F32)
            zq3, zg3 = zq.reshape(bp, sp, -1), zg.reshape(bp, sp, -1)
            yc, s_f, s_b = _retention(zq3, zg3, c_decay_f[i], c_decay_b[i], c_norm_g[i], None, True)
            yd = _dense_attention(zq3, zg3, q_blk=12, k_blk=16, v_blk=17, kv_per_col=False, gate_blk=4,
                                  mode="pair", tq=sp, tk=sp)
            xp = _out_project(yc.reshape(bp * sp, -1), yd.reshape(bp * sp, -1), xp, mod[layer], row_p,
                              w_out, final_g, tm=tm_p, final_norm=last)
            outs["c_f"].append(s_f.astype(dt))
            outs["c_b"].append(s_b.astype(dt))
            outs["d_k"].append(_unpair_kv(zq3, 16, 1, 2))
            outs["d_v"].append(_unpair_kv(zq3, 17, 1, 2))

            zq, zg, vt = _project(xs, mod[layer], row_s, norm_g[layer], w_in, tm=tm_s, kind="cd",
                                  rope_tabs=rope_tabs, norm_params=norm_params, out_dtype=BF16, vt_blocks=(17,))
            zq3, zg3 = zq.reshape(bs, ss, -1), zg.reshape(bs, ss, -1)
            yc = _retention(zq3, zg3, c_decay_f[i], c_decay_b[i], c_norm_g[i],
                            (state_c_fwd[:, i], state_c_bwd[:, i]), False)[0]
            yd = _keymajor_attention(zq3, zg3, vt, _pair_kv(cache_d_k[:, i]),
                                     _pair_kv(cache_d_v[:, i]).transpose(0, 2, 1),
                                     q_blk=12, k_blk=16, kv_per_col=False, gate_blk=4, mode="pair",
                                     tq=TQ_DENSE, tk=TK_DENSE)
            xs = _out_project(yc.reshape(bs * ss, -1), yd.reshape(bs * ss, -1), xs, mod[layer], row_s,
                              w_out, final_g, tm=tm_s, final_norm=last)

    stack = lambda k: jnp.stack(outs[k], axis=1)
    return (xp.reshape(bp, sp, d), xs.reshape(bs, ss, d), stack("a_k"), stack("a_v"), stack("b_k"), stack("b_v"),
            stack("c_f"), stack("c_b"), stack("d_k"), stack("d_v"))
```

```python
import functools
import math

import numpy as np
import jax
import jax.numpy as jnp
from jax import lax
from jax.experimental import pallas as pl
from jax.experimental.pallas import tpu as pltpu

F32 = jnp.float32
BF16 = jnp.bfloat16

LANES = 128
HEAD = 64
GRID_W = 64
CHUNK = 128
WINDOW = 128
ROPE_THETA = 10000.0
EPS = 1e-6
NEG = -1e30
VMEM_LIMIT = 56 * 1024 * 1024
TQ_DENSE = 1024
TK_DENSE = 1024
SCORE_SLOTS = 4
WINDOW_QBLOCKS = 2
RET_UNROLL = 8
ONES_ROWS = 16
LOG2E = 1.4426950408889634

GQA_HEADS = 8
PAIR_ORDER = (0, 4, 1, 5, 2, 6, 3, 7)
N_QCOL = 4
C_HEADS = 4
C_DK = 128

N_QKV_BLK = 18
N_GATE_BLK = 8
PAIR_K_BLK, PAIR_V_BLK = 16, 17
AB_OPS = ("qrope",) * 8 + ("rope",) * 4 + ("plain",) * 4 + ("rope", "plain")
CD_OPS = ("plain",) * 4 + ("kscale",) * 4 + ("plain",) * 4 + ("qnorm",) * 4 + ("knorm", "plain")
PROJ_GROUPS = ((0, 4), (4, 8), (8, 12), (12, 16), (16, 18))


def _silu(x):
    return x / (1.0 + jnp.exp(-x))


def _lane_lo(shape):
    return lax.broadcasted_iota(jnp.int32, shape, len(shape) - 1) % LANES < HEAD


def _mod_kernel(cond_ref, w_ref, b_ref, o_ref):
    s = _silu(cond_ref[...])
    o_ref[0] = jnp.dot(s.astype(BF16), w_ref[0].astype(BF16), preferred_element_type=F32) + b_ref[0]


def _modulation(cond, mod_w, mod_b):
    depth, d, d3 = mod_w.shape
    rows = cond.shape[0]
    return pl.pallas_call(
        _mod_kernel,
        grid=(depth, d3 // d),
        in_specs=[pl.BlockSpec((rows, d), lambda l, j: (0, 0)),
                  pl.BlockSpec((1, d, d), lambda l, j: (l, 0, j)),
                  pl.BlockSpec((1, 1, d), lambda l, j: (l, 0, j))],
        out_specs=pl.BlockSpec((1, rows, d), lambda l, j: (l, 0, j)),
        out_shape=jax.ShapeDtypeStruct((depth, rows, d3), F32),
        compiler_params=pltpu.CompilerParams(vmem_limit_bytes=VMEM_LIMIT),
        name="modulation",
    )(cond, mod_w, mod_b.reshape(depth, 1, d3))


def _rot_half(x, first_half):
    return jnp.where(first_half, pltpu.roll(x, LANES - HEAD // 2, 1), pltpu.roll(x, HEAD // 2, 1))


def _head_sumsq(x, bd):
    sq = x * x
    hi = sq.astype(BF16)
    lo = (sq - hi.astype(F32)).astype(BF16)
    return jnp.dot(hi, bd, preferred_element_type=F32) + jnp.dot(lo, bd, preferred_element_type=F32)


def _proj_kernel(*refs, ops, groups, use_rope, has_norm, vt_blocks):
    it = iter(refs)
    x_ref, mod_ref, g_ref, w_ref = next(it), next(it), next(it), next(it)
    cos_ref = sin_ref = qg_ref = kg_ref = bd_ref = vt_ref = None
    if use_rope:
        cos_ref, sin_ref = next(it), next(it)
    if has_norm:
        qg_ref, kg_ref, bd_ref = next(it), next(it), next(it)
    zq_ref, zg_ref = next(it), next(it)
    if vt_blocks:
        vt_ref = next(it)

    x = x_ref[...]
    h = x * lax.rsqrt(jnp.mean(x * x, axis=-1, keepdims=True) + EPS) * g_ref[...]
    h = h * (1.0 + mod_ref[0, 1:2, :]) + mod_ref[0, 0:1, :]
    hb = h.astype(BF16)

    tm = x.shape[0]
    if use_rope:
        cos, sin = cos_ref[...], sin_ref[...]
        first_half = lax.broadcasted_iota(jnp.int32, (tm, LANES), 1) % HEAD < HEAD // 2

    def rope(z):
        if not use_rope:
            return z
        return z * cos + _rot_half(z, first_half) * sin

    def head_norm(z, gain_ref):
        ss = _head_sumsq(z, bd_ref[...])
        return z * lax.rsqrt(ss * (1.0 / HEAD) + EPS) * gain_ref[...]

    q_scale = HEAD ** -0.5 * LOG2E
    for b0, b1 in groups:
        z = jnp.dot(hb, w_ref[:, b0 * LANES:b1 * LANES], preferred_element_type=F32)
        for j in range(b0, b1):
            zz = z[:, (j - b0) * LANES:(j - b0 + 1) * LANES]
            op = ops[j]
            if op == "qrope":
                zz = rope(zz * q_scale)
            elif op == "rope":
                zz = rope(zz)
            elif op == "kscale":
                zz = zz * (C_DK ** -0.5)
            elif op == "qnorm":
                zz = rope(head_norm(zz, qg_ref)) * q_scale
            elif op == "knorm":
                zz = rope(head_norm(zz, kg_ref))
            zq_ref[:, j * LANES:(j + 1) * LANES] = zz.astype(zq_ref.dtype)
            if j in vt_blocks:
                vt_ref[vt_blocks.index(j)] = zz.T.astype(vt_ref.dtype)
    half = N_GATE_BLK // 2
    for g0 in (0, half):
        c0 = (N_QKV_BLK + g0) * LANES
        z = jnp.dot(hb, w_ref[:, c0:c0 + half * LANES], preferred_element_type=F32)
        zg_ref[:, g0 * LANES:(g0 + half) * LANES] = _silu(z)


def _project(x2d, mod, mod_row_of_tile, norm_g, w_bf16, *, tm, kind, rope_tabs, norm_params, out_dtype,
             vt_blocks=()):
    t, d = x2d.shape
    out_specs = [pl.BlockSpec((tm, N_QKV_BLK * LANES), lambda i: (i, 0)),
                 pl.BlockSpec((tm, N_GATE_BLK * LANES), lambda i: (i, 0))]
    out_shape = [jax.ShapeDtypeStruct((t, N_QKV_BLK * LANES), out_dtype),
                 jax.ShapeDtypeStruct((t, N_GATE_BLK * LANES), F32)]
    if vt_blocks:
        out_specs.append(pl.BlockSpec((len(vt_blocks), LANES, tm), lambda i: (0, 0, i)))
        out_shape.append(jax.ShapeDtypeStruct((len(vt_blocks), LANES, t), out_dtype))
    use_rope = rope_tabs is not None
    has_norm = kind == "cd"
    ops, groups = (AB_OPS if kind == "ab" else CD_OPS), PROJ_GROUPS
    in_specs = [pl.BlockSpec((tm, d), lambda i: (i, 0)),
                pl.BlockSpec((1, 3, d), lambda i: (mod_row_of_tile(i), 0, 0)),
                pl.BlockSpec((1, d), lambda i: (0, 0)),
                pl.BlockSpec(w_bf16.shape, lambda i: (0, 0))]
    args = [x2d, mod, norm_g.reshape(1, d), w_bf16]
    if use_rope:
        n_seq_tiles = rope_tabs[0].shape[0] // tm
        in_specs += [pl.BlockSpec((tm, LANES), lambda i: (i % n_seq_tiles, 0))] * 2
        args += list(rope_tabs)
    if has_norm:
        in_specs += [pl.BlockSpec((1, LANES), lambda i: (0, 0))] * 2 + [pl.BlockSpec((LANES, LANES), lambda i: (0, 0))]
        args += list(norm_params)
    return pl.pallas_call(
        functools.partial(_proj_kernel, ops=ops, groups=groups, use_rope=use_rope, has_norm=has_norm,
                          vt_blocks=tuple(vt_blocks)),
        grid=(t // tm,),
        in_specs=in_specs,
        out_specs=out_specs,
        out_shape=out_shape,
        compiler_params=pltpu.CompilerParams(vmem_limit_bytes=VMEM_LIMIT),
        name="project_" + kind,
    )(*args)


def _pair_rows(q):
    lo = jnp.where(_lane_lo((1, LANES)), 1.0, 0.0).astype(BF16)
    qb = q.astype(BF16)
    return jnp.concatenate([qb * lo, qb * (1.0 - lo).astype(BF16)], axis=0)


def _ctx_attn_kernel(*refs, mode, has_sink, kv_per_col, lam_init):
    it = iter(refs)
    q_ref, k_ref, v_ref = next(it), next(it), next(it)
    sink_ref = next(it) if has_sink else None
    if mode == "diff":
        lq1_ref, lk1_ref, lq2_ref, lk2_ref, bg_ref = (next(it) for _ in range(5))
    gate_ref, y_ref = next(it), next(it)

    tq = q_ref.shape[1]
    lo = _lane_lo((tq, LANES))
    for c in range(N_QCOL):
        cols = slice(c * LANES, (c + 1) * LANES)
        kv_cols = cols if kv_per_col else slice(0, LANES)
        qrows = _pair_rows(q_ref[0, :, cols])
        s = lax.dot_general(qrows, k_ref[0, :, kv_cols].astype(BF16), (((1,), (1,)), ((), ())),
                            preferred_element_type=F32)
        m = jnp.max(s, axis=-1, keepdims=True)
        if has_sink:
            sk = jnp.concatenate([jnp.full((tq, 1), sink_ref[c] * LOG2E, F32),
                                  jnp.full((tq, 1), sink_ref[c + N_QCOL] * LOG2E, F32)], axis=0)
            m = jnp.maximum(m, sk)
        p = jnp.exp2(s - m)
        l = jnp.sum(p, axis=-1, keepdims=True)
        if has_sink:
            l = l + jnp.exp2(sk - m)
        o2 = jnp.dot(p.astype(BF16), v_ref[0, :, kv_cols].astype(BF16), preferred_element_type=F32) / l
        if mode == "pair":
            o = jnp.where(lo, o2[:tq], o2[tq:])
        else:
            lam = (jnp.exp(jnp.sum(lq1_ref[...] * lk1_ref[...], axis=-1, keepdims=True))
                   - jnp.exp(jnp.sum(lq2_ref[...] * lk2_ref[...], axis=-1, keepdims=True)) + lam_init)
            o = o2[:tq] - lam * o2[tq:]
            o = o * lax.rsqrt(jnp.mean(o * o, axis=-1, keepdims=True) + EPS) * bg_ref[...] * (1.0 - lam_init)
        y_ref[0, :, cols] = (o * gate_ref[0, :, cols]).astype(y_ref.dtype)


def _ctx_attention(zq, zg, *, q_blk, k_blk, v_blk, kv_per_col, gate_blk, mode,
                   sink=None, diff_params=None, lam_init=0.0):
    b, n, _ = zq.shape
    w4 = N_QCOL * LANES
    kv_w = w4 if kv_per_col else LANES
    assert (q_blk * LANES) % w4 == 0 and (gate_blk * LANES) % w4 == 0
    assert (k_blk * LANES) % kv_w == 0 and (v_blk * LANES) % kv_w == 0
    in_specs = [pl.BlockSpec((1, n, w4), lambda bi: (bi, 0, q_blk * LANES // w4)),
                pl.BlockSpec((1, n, kv_w), lambda bi: (bi, 0, k_blk * LANES // kv_w)),
                pl.BlockSpec((1, n, kv_w), lambda bi: (bi, 0, v_blk * LANES // kv_w))]
    args = [zq, zq, zq]
    if sink is not None:
        in_specs.append(pl.BlockSpec(memory_space=pltpu.SMEM))
        args.append(sink)
    if mode == "diff":
        in_specs += [pl.BlockSpec((1, HEAD), lambda bi: (0, 0))] * 4
        in_specs.append(pl.BlockSpec((1, LANES), lambda bi: (0, 0)))
        args += list(diff_params)
    in_specs.append(pl.BlockSpec((1, n, w4), lambda bi: (bi, 0, gate_blk * LANES // w4)))
    args.append(zg)
    return pl.pallas_call(
        functools.partial(_ctx_attn_kernel, mode=mode, has_sink=sink is not None, kv_per_col=kv_per_col,
                          lam_init=lam_init),
        grid=(b,),
        in_specs=in_specs,
        out_specs=pl.BlockSpec((1, n, w4), lambda bi: (bi, 0, 0)),
        out_shape=jax.ShapeDtypeStruct((b, n, w4), BF16),
        compiler_params=pltpu.CompilerParams(vmem_limit_bytes=VMEM_LIMIT),
        name="ctx_attention_" + mode,
    )(*args)


def _keymajor_attn_kernel(*refs, mode, tk, lam_init):
    it = iter(refs)
    q_ref, k_ref, vt_ref, kx_ref, vxt_ref = (next(it) for _ in range(5))
    if mode == "diff":
        lq1_ref, lk1_ref, lq2_ref, lk2_ref, bg_ref = (next(it) for _ in range(5))
    gate_ref, y_ref, s_scr, sx_scr = next(it), next(it), next(it), next(it)

    tq = q_ref.shape[1]
    qrows = _pair_rows(q_ref[0])

    def scores(k, slot):
        s = lax.dot_general(k, qrows, (((1,), (1,)), ((), ())), preferred_element_type=F32)
        slot[...] = s
        return jnp.max(s, axis=0, keepdims=True)

    vd = HEAD if mode == "pair" else LANES

    def with_ones(vt):
        return jnp.concatenate([vt, jnp.ones((ONES_ROWS, vt.shape[1]), BF16)], axis=0)

    def consume(slot, cmax, vt, m, acc_lo, acc_hi):
        m_new = jnp.maximum(m, cmax)
        alpha = jnp.exp2(m - m_new)
        p = jnp.exp2(slot[...] - m_new).astype(BF16)
        if mode == "pair":
            v_lo, v_hi = with_ones(vt[:HEAD]), with_ones(vt[HEAD:])
        else:
            v_lo = v_hi = with_ones(vt)
        acc_lo = alpha[:, :tq] * acc_lo + jnp.dot(v_lo, p[:, :tq], preferred_element_type=F32)
        acc_hi = alpha[:, tq:] * acc_hi + jnp.dot(v_hi, p[:, tq:], preferred_element_type=F32)
        return m_new, acc_lo, acc_hi

    n_chunks = k_ref.shape[1] // tk
    carry = (jnp.full((1, 2 * tq), NEG, F32), jnp.zeros((vd + ONES_ROWS, tq), F32),
             jnp.zeros((vd + ONES_ROWS, tq), F32))
    n_slots = s_scr.shape[0]
    ahead = n_slots - 1
    chunk_scores = lambda j: scores(k_ref[0, j * tk:(j + 1) * tk, :], s_scr.at[j % n_slots])
    cmax = [chunk_scores(j) for j in range(min(ahead, n_chunks))]
    cmax_x = scores(kx_ref[0].astype(BF16), sx_scr)
    for j in range(n_chunks):
        if j + ahead < n_chunks:
            cmax.append(chunk_scores(j + ahead))
        carry = consume(s_scr.at[j % n_slots], cmax[j], vt_ref[0, :, j * tk:(j + 1) * tk], *carry)
    _, acc_lo, acc_hi = consume(sx_scr, cmax_x, vxt_ref[0].astype(BF16), *carry)

    ot_lo = acc_lo[:vd] / acc_lo[vd:vd + 1]
    ot_hi = acc_hi[:vd] / acc_hi[vd:vd + 1]
    if mode == "pair":
        o = jnp.concatenate([ot_lo, ot_hi], axis=0).T
    else:
        lam = (jnp.exp(jnp.sum(lq1_ref[...] * lk1_ref[...], axis=-1, keepdims=True))
               - jnp.exp(jnp.sum(lq2_ref[...] * lk2_ref[...], axis=-1, keepdims=True)) + lam_init)
        o = (ot_lo - lam * ot_hi).T
        o = o * lax.rsqrt(jnp.mean(o * o, axis=-1, keepdims=True) + EPS) * bg_ref[...] * (1.0 - lam_init)
    y_ref[0] = (o * gate_ref[0]).astype(y_ref.dtype)


def _keymajor_attention(zq, zg, vt, kx, vxt, *, q_blk, k_blk, vt_blk, kv_per_col, gate_blk, mode, tq, tk,
                        diff_params=None, lam_init=0.0):
    b, n, _ = zq.shape
    per = (lambda c: c) if kv_per_col else (lambda c: 0)
    if kx.ndim == 4:
        kx_spec = pl.BlockSpec((None, 1) + kx.shape[2:], lambda bi, c, qi: (bi, c, 0, 0))
        vx_spec = pl.BlockSpec((None, 1) + vxt.shape[2:], lambda bi, c, qi: (bi, c, 0, 0))
    else:
        kx_spec = pl.BlockSpec((1,) + kx.shape[1:], lambda bi, c, qi: (bi, 0, 0))
        vx_spec = pl.BlockSpec((1,) + vxt.shape[1:], lambda bi, c, qi: (bi, 0, 0))
    in_specs = [pl.BlockSpec((1, tq, LANES), lambda bi, c, qi: (bi, qi, q_blk + c)),
                pl.BlockSpec((1, n, LANES), lambda bi, c, qi: (bi, 0, k_blk + per(c))),
                pl.BlockSpec((1, LANES, n), lambda bi, c, qi: (vt_blk + per(c), 0, bi)),
                kx_spec, vx_spec]
    args = [zq, zq, vt, kx, vxt]
    if mode == "diff":
        in_specs += [pl.BlockSpec((1, HEAD), lambda bi, c, qi: (0, 0))] * 4
        in_specs.append(pl.BlockSpec((1, LANES), lambda bi, c, qi: (0, 0)))
        args += list(diff_params)
    in_specs.append(pl.BlockSpec((1, tq, LANES), lambda bi, c, qi: (bi, qi, gate_blk + c)))
    args.append(zg)
    return pl.pallas_call(
        functools.partial(_keymajor_attn_kernel, mode=mode, tk=tk, lam_init=lam_init),
        grid=(b, N_QCOL, n // tq),
        in_specs=in_specs,
        out_specs=pl.BlockSpec((1, tq, LANES), lambda bi, c, qi: (bi, qi, c)),
        out_shape=jax.ShapeDtypeStruct((b, n, N_QCOL * LANES), BF16),
        scratch_shapes=[pltpu.VMEM((SCORE_SLOTS, tk, 2 * tq), F32), pltpu.VMEM((kx.shape[-2], 2 * tq), F32)],
        compiler_params=pltpu.CompilerParams(vmem_limit_bytes=VMEM_LIMIT),
        name="attention_keymajor_" + mode,
    )(*args)


def _window_attn_kernel(q_ref, k_ref, vt_ref, kx_ref, vxt_ref, sink_ref, gate_ref, y_ref):
    n = k_ref.shape[1]
    span = 3 * CHUNK
    lo = jnp.where(_lane_lo((1, LANES)), 1.0, 0.0).astype(BF16)
    hi = (1.0 - lo).astype(BF16)
    sk = jnp.concatenate([jnp.full((1, CHUNK), sink_ref[h] * LOG2E, F32) for h in range(GQA_HEADS)], axis=1)
    kx = kx_ref[0].astype(BF16)
    vxt = vxt_ref[0].astype(BF16)
    half = N_QCOL * CHUNK

    def with_ones(vt):
        return jnp.concatenate([vt, jnp.ones((ONES_ROWS, vt.shape[1]), BF16)], axis=0)

    for sub in range(WINDOW_QBLOCKS):
        blk = pl.program_id(1) * WINDOW_QBLOCKS + sub
        q = q_ref[0, sub * CHUNK:(sub + 1) * CHUNK, :].astype(BF16)
        cols = [q[:, c * LANES:(c + 1) * LANES] for c in range(N_QCOL)]
        qrows = jnp.concatenate([c * lo for c in cols] + [c * hi for c in cols], axis=0)
        w0 = pl.multiple_of(jnp.clip((blk - 1) * CHUNK, 0, n - span), CHUNK)
        s_w = lax.dot_general(k_ref[0, pl.ds(w0, span), :], qrows, (((1,), (1,)), ((), ())),
                              preferred_element_type=F32)
        s_x = lax.dot_general(kx, qrows, (((1,), (1,)), ((), ())), preferred_element_type=F32)
        dist = (w0 - blk * CHUNK + lax.broadcasted_iota(jnp.int32, (span, CHUNK), 0)
                - lax.broadcasted_iota(jnp.int32, (span, CHUNK), 1))
        bias = jnp.where((dist >= -WINDOW) & (dist <= WINDOW), 0.0, NEG)
        s_w = s_w + jnp.concatenate([bias] * GQA_HEADS, axis=1)
        m = jnp.maximum(jnp.maximum(jnp.max(s_w, axis=0, keepdims=True), jnp.max(s_x, axis=0, keepdims=True)), sk)
        p = jnp.concatenate([jnp.exp2(s_w - m), jnp.exp2(s_x - m)], axis=0).astype(BF16)
        vt = jnp.concatenate([vt_ref[0, :, pl.ds(w0, span)], vxt], axis=1)
        acc_lo = jnp.dot(with_ones(vt[:HEAD]), p[:, :half], preferred_element_type=F32)
        acc_hi = jnp.dot(with_ones(vt[HEAD:]), p[:, half:], preferred_element_type=F32)
        sink_p = jnp.exp2(sk - m)
        ot_lo = acc_lo[:HEAD] / (acc_lo[HEAD:HEAD + 1] + sink_p[:, :half])
        ot_hi = acc_hi[:HEAD] / (acc_hi[HEAD:HEAD + 1] + sink_p[:, half:])
        o = jnp.concatenate(
            [jnp.concatenate([ot_lo[:, c * CHUNK:(c + 1) * CHUNK], ot_hi[:, c * CHUNK:(c + 1) * CHUNK]], axis=0).T
             for c in range(N_QCOL)], axis=1)
        rows = slice(sub * CHUNK, (sub + 1) * CHUNK)
        y_ref[0, rows, :] = (o * gate_ref[0, rows, :]).astype(y_ref.dtype)


def _window_attention(zq, zg, vt, kx, vxt, sink):
    b, n, _ = zq.shape
    w4 = N_QCOL * LANES
    tq = WINDOW_QBLOCKS * CHUNK
    return pl.pallas_call(
        _window_attn_kernel,
        grid=(b, n // tq),
        in_specs=[pl.BlockSpec((1, tq, w4), lambda bi, i: (bi, i, 0)),
                  pl.BlockSpec((1, n, LANES), lambda bi, i: (bi, 0, PAIR_K_BLK)),
                  pl.BlockSpec((1, LANES, n), lambda bi, i: (0, 0, bi)),
                  pl.BlockSpec((1,) + kx.shape[1:], lambda bi, i: (bi, 0, 0)),
                  pl.BlockSpec((1,) + vxt.shape[1:], lambda bi, i: (bi, 0, 0)),
                  pl.BlockSpec(memory_space=pltpu.SMEM),
                  pl.BlockSpec((1, tq, w4), lambda bi, i: (bi, i, 0))],
        out_specs=pl.BlockSpec((1, tq, w4), lambda bi, i: (bi, i, 0)),
        out_shape=jax.ShapeDtypeStruct((b, n, w4), BF16),
        compiler_params=pltpu.CompilerParams(vmem_limit_bytes=VMEM_LIMIT),
        name="attention_window",
    )(zq, zq, vt, kx, vxt, sink, zg)


def _retention_kernel(*refs, has_state_in, emit_state):
    it = iter(refs)
    q_ref, k_ref, v_ref, decf_ref, decb_ref, cg_ref, gate_ref = (next(it) for _ in range(7))
    sf_in = sb_in = sf_out = sb_out = None
    if has_state_in:
        sf_in, sb_in = next(it), next(it)
    y_ref = next(it)
    if emit_state:
        sf_out, sb_out = next(it), next(it)
    o_scr, u_scr, dm_scr = next(it), next(it), next(it)

    nc = q_ref.shape[1] // CHUNK
    ri = lax.broadcasted_iota(jnp.int32, (CHUNK, CHUNK), 0).astype(F32)
    ci = lax.broadcasted_iota(jnp.int32, (CHUNK, CHUNK), 1).astype(F32)
    tok_col = lax.broadcasted_iota(jnp.int32, (CHUNK, 1), 0).astype(F32)
    tok_row = lax.broadcasted_iota(jnp.int32, (1, CHUNK), 1).astype(F32)

    lg_f = -jnp.exp(decf_ref[0][:, :1])
    lg_b = -jnp.exp(decb_ref[0][:, :1])
    cross_f, cross_b = jnp.exp(lg_f * (tok_col + 1.0)), jnp.exp(lg_b * (CHUNK - 1.0 - tok_col))
    kdec_f, kdec_b = jnp.exp(lg_f * (CHUNK - 1.0 - tok_row)), jnp.exp(lg_b * tok_row)
    cdec_f, cdec_b = jnp.exp(lg_f * CHUNK), jnp.exp(lg_b * CHUNK)
    rel_f, rel_b = ri - ci, ci - ri - 1.0
    dm_scr[...] = jnp.where(rel_f >= 0.0, jnp.exp(lg_f * jnp.maximum(rel_f, 0.0)),
                            jnp.exp(lg_b * jnp.maximum(rel_b, 0.0)))

    def chunk_rows(c):
        return pl.ds(pl.multiple_of(c * CHUNK, CHUNK), CHUNK)

    def local_body(c, _):
        rows = chunk_rows(c)
        qh = q_ref[0, rows, :].astype(BF16)
        kf = k_ref[0, rows, :].astype(F32)
        vh = v_ref[0, rows, :].astype(BF16)
        att = lax.dot_general(qh, kf.astype(BF16), (((1,), (1,)), ((), ())),
                              preferred_element_type=F32) * dm_scr[...]
        kt = kf.T
        lhs = jnp.concatenate([att.astype(BF16), (kt * kdec_f).astype(BF16), (kt * kdec_b).astype(BF16)], axis=0)
        r = jnp.dot(lhs, vh, preferred_element_type=F32)
        o_scr[rows, :] = r[:CHUNK]
        u_scr[c, 0] = r[CHUNK:2 * CHUNK]
        u_scr[c, 1] = r[2 * CHUNK:]
        return 0
    lax.fori_loop(0, nc, local_body, 0, unroll=min(nc, RET_UNROLL))

    def scan_body(t, states):
        s_f, s_b = states
        inc_f, inc_b = u_scr[t, 0], u_scr[nc - 1 - t, 1]
        u_scr[t, 0] = s_f
        u_scr[nc - 1 - t, 1] = s_b
        return cdec_f * s_f + inc_f, cdec_b * s_b + inc_b
    zero = jnp.zeros((CHUNK, CHUNK), F32)
    s_f, s_b = lax.fori_loop(0, nc, scan_body, (sf_in[0, 0], sb_in[0, 0]) if has_state_in else (zero, zero),
                             unroll=2)
    if emit_state:
        sf_out[0, 0] = s_f.astype(sf_out.dtype)
        sb_out[0, 0] = s_b.astype(sb_out.dtype)

    def finish_body(c, _):
        rows = chunk_rows(c)
        qh = q_ref[0, rows, :].astype(BF16)
        states = jnp.concatenate([u_scr[c, 0], u_scr[c, 1]], axis=1).astype(BF16)
        r = jnp.dot(qh, states, preferred_element_type=F32)
        o = o_scr[rows, :] + cross_f * r[:, :LANES] + cross_b * r[:, LANES:]
        mu = jnp.mean(o, axis=-1, keepdims=True)
        d = o - mu
        var = jnp.mean(d * d, axis=-1, keepdims=True)
        y = d * lax.rsqrt(var + EPS) * cg_ref[0]
        y_ref[0, rows, :] = (y * gate_ref[0, rows, :]).astype(y_ref.dtype)
        return 0
    lax.fori_loop(0, nc, finish_body, 0, unroll=min(nc, RET_UNROLL))


def _retention(zq, zg, dec_f, dec_b, c_norm_g, states, emit_state):
    b, n, _ = zq.shape
    seq_spec = lambda blk: pl.BlockSpec((1, n, LANES), lambda bi, h: (bi, 0, blk + h))
    st_spec = pl.BlockSpec((1, 1, CHUNK, CHUNK), lambda bi, h: (bi, h, 0, 0))
    head_spec = pl.BlockSpec((1, 1, LANES), lambda bi, h: (h, 0, 0))
    bcast = lambda p: jnp.broadcast_to(p.astype(F32)[:, None, None], (C_HEADS, 1, LANES))
    in_specs = [seq_spec(0), seq_spec(C_HEADS), seq_spec(2 * C_HEADS), head_spec, head_spec, head_spec,
                seq_spec(0)]
    args = [zq, zq, zq, bcast(dec_f), bcast(dec_b), c_norm_g.astype(F32)[:, None, :], zg]
    if states is not None:
        in_specs += [st_spec, st_spec]
        args += list(states)
    out_specs = [seq_spec(0)]
    out_shape = [jax.ShapeDtypeStruct((b, n, C_HEADS * LANES), BF16)]
    if emit_state:
        out_specs += [st_spec, st_spec]
        out_shape += [jax.ShapeDtypeStruct((b, C_HEADS, CHUNK, CHUNK), F32)] * 2
    return pl.pallas_call(
        functools.partial(_retention_kernel, has_state_in=states is not None, emit_state=emit_state),
        grid=(b, C_HEADS),
        in_specs=in_specs,
        out_specs=out_specs,
        out_shape=out_shape,
        scratch_shapes=[pltpu.VMEM((n, LANES), F32),
                        pltpu.VMEM((n // CHUNK, 2, CHUNK, CHUNK), F32),
                        pltpu.VMEM((CHUNK, CHUNK), F32)],
        compiler_params=pltpu.CompilerParams(vmem_limit_bytes=VMEM_LIMIT),
        name="retention",
    )(*args)


def _out_kernel(y1_ref, y2_ref, x_ref, mod_ref, w_ref, fg_ref, o_ref, *, final_norm):
    half = y1_ref.shape[1]
    y = (jnp.dot(y1_ref[...], w_ref[:half, :], preferred_element_type=F32)
         + jnp.dot(y2_ref[...], w_ref[half:, :], preferred_element_type=F32))
    x = x_ref[...] + mod_ref[0, 2:3, :] * y
    if final_norm:
        x = x * lax.rsqrt(jnp.mean(x * x, axis=-1, keepdims=True) + EPS) * fg_ref[...]
    o_ref[...] = x


def _out_project(y1, y2, x2d, mod, mod_row_of_tile, w_bf16, final_g, *, tm, final_norm):
    t, d = x2d.shape
    half = y1.shape[1]
    return pl.pallas_call(
        functools.partial(_out_kernel, final_norm=final_norm),
        grid=(t // tm,),
        in_specs=[pl.BlockSpec((tm, half), lambda i: (i, 0)),
                  pl.BlockSpec((tm, half), lambda i: (i, 0)),
                  pl.BlockSpec((tm, d), lambda i: (i, 0)),
                  pl.BlockSpec((1, 3, d), lambda i: (mod_row_of_tile(i), 0, 0)),
                  pl.BlockSpec(w_bf16.shape, lambda i: (0, 0)),
                  pl.BlockSpec((1, d), lambda i: (0, 0))],
        out_specs=pl.BlockSpec((tm, d), lambda i: (i, 0)),
        out_shape=jax.ShapeDtypeStruct((t, d), F32),
        compiler_params=pltpu.CompilerParams(vmem_limit_bytes=VMEM_LIMIT),
        name="out_project",
    )(y1, y2, x2d, mod, w_bf16, final_g.reshape(1, d))


def _pair_perm():
    return np.concatenate([np.arange(h * HEAD, (h + 1) * HEAD) for h in PAIR_ORDER])


def _rope_tables(n):
    rows = n // GRID_W
    row = jnp.repeat(jnp.arange(rows, dtype=F32), GRID_W)
    col = jnp.tile(jnp.arange(GRID_W, dtype=F32), rows)
    nf = HEAD // 4
    inv = ROPE_THETA ** (-jnp.arange(nf, dtype=F32) / nf)
    ang = jnp.concatenate([row[:, None] * inv, col[:, None] * inv], axis=-1)
    cos, sin = jnp.cos(ang), jnp.sin(ang)
    return jnp.tile(cos, (1, 4)), jnp.tile(jnp.concatenate([-sin, sin], axis=-1), (1, 2))


def _pair_kv(cache):
    b, g, p, d = cache.shape
    return cache.transpose(0, 2, 1, 3).reshape(b, p, g * d)


def _unpair_kv(z, blk, n_blk, heads):
    b, p, _ = z.shape
    t = z[:, :, blk * LANES:(blk + n_blk) * LANES]
    return t.reshape(b, p, heads, t.shape[-1] // heads).transpose(0, 2, 1, 3)


def kernel(x_prompt, x_sample, cache_a_k, cache_a_v, cache_b_k, cache_b_v, state_c_fwd, state_c_bwd, cache_d_k, cache_d_v, c, c_ctx, norm_g, mod_w, mod_b, ab_w_in, ab_w_out, a_sink, b_lq1, b_lk1, b_lq2, b_lk2, b_norm_g, cd_w_in, cd_w_out, c_decay_f, c_decay_b, c_norm_g, d_q_norm_g, d_k_norm_g, final_g):
    depth = norm_g.shape[0]
    bp, sp, d = x_prompt.shape
    bs, ss, _ = x_sample.shape
    dt = x_prompt.dtype

    ctx_row = bs
    pad = (-(bs + 1)) % 8
    cond = jnp.concatenate([c, c_ctx[None, :], jnp.zeros((pad, d), c.dtype)], axis=0)
    mod = _modulation(cond, mod_w, mod_b).reshape(depth, cond.shape[0], 3, d)

    perm = _pair_perm()
    qkv_w = N_QKV_BLK * LANES
    rope_tabs = _rope_tables(ss)
    bd = jnp.asarray(np.kron(np.eye(LANES // HEAD), np.ones((HEAD, HEAD))), BF16)

    tm_s = 512
    tm_p = 512
    tiles_per_seq = ss // tm_s
    row_s = lambda i: i // tiles_per_seq
    row_p = lambda i: ctx_row

    xp = x_prompt.reshape(bp * sp, d)
    xs = x_sample.reshape(bs * ss, d)
    outs = {k: [] for k in ("a_k", "a_v", "b_k", "b_v", "c_f", "c_b", "d_k", "d_v")}

    for layer in range(depth):
        i = layer // 2
        last = layer == depth - 1
        if layer % 2 == 0:
            lam_init = 0.8 - 0.6 * math.exp(-0.3 * layer)
            w4 = N_QCOL * LANES
            ar = np.arange
            cols = np.concatenate([perm, ar(w4 + 2 * LANES, 4 * w4 + 2 * LANES), ar(w4, w4 + 2 * LANES),
                                   qkv_w + perm, ar(qkv_w + w4, qkv_w + 2 * w4)])
            w_in = ab_w_in[i][:, cols].astype(BF16)
            rows = np.arange(ab_w_out.shape[1])
            rows[0:N_QCOL * LANES] = perm
            w_out = ab_w_out[i][rows, :].astype(BF16)
            diff_params = (b_lq1[i][None], b_lk1[i][None], b_lq2[i][None], b_lk2[i][None], b_norm_g[i][None])

            zq, zg = _project(xp, mod[layer], row_p, norm_g[layer], w_in, tm=tm_p, kind="ab",
                              rope_tabs=None, norm_params=None, out_dtype=F32)
            zq3, zg3 = zq.reshape(bp, sp, -1), zg.reshape(bp, sp, -1)
            ya = _ctx_attention(zq3, zg3, q_blk=0, k_blk=PAIR_K_BLK, v_blk=PAIR_V_BLK, kv_per_col=False,
                                gate_blk=0, mode="pair", sink=a_sink[i])
            yb = _ctx_attention(zq3, zg3, q_blk=4, k_blk=8, v_blk=12, kv_per_col=True, gate_blk=4,
                                mode="diff", diff_params=diff_params, lam_init=lam_init)
            xp = _out_project(ya.reshape(bp * sp, -1), yb.reshape(bp * sp, -1), xp, mod[layer], row_p,
                              w_out, final_g, tm=tm_p, final_norm=last)
            outs["a_k"].append(_unpair_kv(zq3, PAIR_K_BLK, 1, 2))
            outs["a_v"].append(_unpair_kv(zq3, PAIR_V_BLK, 1, 2))
            outs["b_k"].append(_unpair_kv(zq3, 8, 4, 4))
            outs["b_v"].append(_unpair_kv(zq3, 12, 4, 4))

            zq, zg, vt = _project(xs, mod[layer], row_s, norm_g[layer], w_in, tm=tm_s, kind="ab",
                                  rope_tabs=rope_tabs, norm_params=None, out_dtype=BF16,
                                  vt_blocks=(PAIR_V_BLK, 12, 13, 14, 15))
            zq3, zg3 = zq.reshape(bs, ss, -1), zg.reshape(bs, ss, -1)
            ya = _window_attention(zq3, zg3, vt, _pair_kv(cache_a_k[:, i]),
                                   _pair_kv(cache_a_v[:, i]).transpose(0, 2, 1), a_sink[i])
            yb = _keymajor_attention(zq3, zg3, vt, cache_b_k[:, i], cache_b_v[:, i].transpose(0, 1, 3, 2),
                                     q_blk=4, k_blk=8, vt_blk=1, kv_per_col=True, gate_blk=4, mode="diff",
                                     tq=TQ_DENSE, tk=TK_DENSE, diff_params=diff_params, lam_init=lam_init)
            xs = _out_project(ya.reshape(bs * ss, -1), yb.reshape(bs * ss, -1), xs, mod[layer], row_s,
                              w_out, final_g, tm=tm_s, final_norm=last)
        else:
            cols = np.arange(cd_w_in.shape[2])
            cols[12 * LANES:16 * LANES] = 12 * LANES + perm
            cols[qkv_w + N_QCOL * LANES:qkv_w + 2 * N_QCOL * LANES] = qkv_w + N_QCOL * LANES + perm
            w_in = cd_w_in[i][:, cols].astype(BF16)
            rows = np.arange(cd_w_out.shape[1])
            rows[N_QCOL * LANES:] = N_QCOL * LANES + perm
            w_out = cd_w_out[i][rows, :].astype(BF16)
            norm_params = (jnp.tile(d_q_norm_g[i], 2)[None], jnp.tile(d_k_norm_g[i], 2)[None], bd)

            zq, zg = _project(xp, mod[layer], row_p, norm_g[layer], w_in, tm=tm_p, kind="cd",
                              rope_tabs=None, norm_params=norm_params, out_dtype=F32)
            zq3, zg3 = zq.reshape(bp, sp, -1), zg.reshape(bp, sp, -1)
            yc, s_f, s_b = _retention(zq3, zg3, c_decay_f[i], c_decay_b[i], c_norm_g[i], None, True)
            yd = _ctx_attention(zq3, zg3, q_blk=12, k_blk=PAIR_K_BLK, v_blk=PAIR_V_BLK, kv_per_col=False,
                                gate_blk=4, mode="pair")
            xp = _out_project(yc.reshape(bp * sp, -1), yd.reshape(bp * sp, -1), xp, mod[layer], row_p,
                              w_out, final_g, tm=tm_p, final_norm=last)
            outs["c_f"].append(s_f.astype(dt))
            outs["c_b"].append(s_b.astype(dt))
            outs["d_k"].append(_unpair_kv(zq3, 16, 1, 2))
            outs["d_v"].append(_unpair_kv(zq3, 17, 1, 2))

            zq, zg, vt = _project(xs, mod[layer], row_s, norm_g[layer], w_in, tm=tm_s, kind="cd",
                                  rope_tabs=rope_tabs, norm_params=norm_params, out_dtype=BF16,
                                  vt_blocks=(PAIR_V_BLK,))
            zq3, zg3 = zq.reshape(bs, ss, -1), zg.reshape(bs, ss, -1)
            yc = _retention(zq3, zg3, c_decay_f[i], c_decay_b[i], c_norm_g[i],
                            (state_c_fwd[:, i], state_c_bwd[:, i]), False)[0]
            yd = _keymajor_attention(zq3, zg3, vt, _pair_kv(cache_d_k[:, i]),
                                     _pair_kv(cache_d_v[:, i]).transpose(0, 2, 1),
                                     q_blk=12, k_blk=PAIR_K_BLK, vt_blk=0, kv_per_col=False, gate_blk=4,
                                     mode="pair",
                                     tq=TQ_DENSE, tk=TK_DENSE)
            xs = _out_project(yc.reshape(bs * ss, -1), yd.reshape(bs * ss, -1), xs, mod[layer], row_s,
                              w_out, final_g, tm=tm_s, final_norm=last)

    stack = lambda k: jnp.stack(outs[k], axis=1)
    return (xp.reshape(bp, sp, d), xs.reshape(bs, ss, d), stack("a_k"), stack("a_v"), stack("b_k"), stack("b_v"),
            stack("c_f"), stack("c_b"), stack("d_k"), stack("d_v"))
```

```python
import functools
import math

import numpy as np
import jax
import jax.numpy as jnp
from jax import lax
from jax.experimental import pallas as pl
from jax.experimental.pallas import tpu as pltpu

F32 = jnp.float32
BF16 = jnp.bfloat16

LANES = 128
HEAD = 64
GRID_W = 64
CHUNK = 128
WINDOW = 128
ROPE_THETA = 10000.0
EPS = 1e-6
NEG = -1e30
VMEM_LIMIT = 56 * 1024 * 1024
TQ_DENSE = 2048
TK_DENSE = 512
SCORE_SLOTS = 4
WINDOW_QBLOCKS = 2
RET_UNROLL = 8
ONES_ROWS = 16
LOG2E = 1.4426950408889634

GQA_HEADS = 8
PAIR_ORDER = (0, 4, 1, 5, 2, 6, 3, 7)
N_QCOL = 4
C_HEADS = 4
C_DK = 128

N_QKV_BLK = 18
N_GATE_BLK = 8
PAIR_K_BLK, PAIR_V_BLK = 16, 17
AB_OPS = ("qrope",) * 8 + ("rope",) * 4 + ("plain",) * 4 + ("rope", "plain")
CD_OPS = ("plain",) * 4 + ("kscale",) * 4 + ("plain",) * 4 + ("qnorm",) * 4 + ("knorm", "plain")
PROJ_GROUPS = ((0, 4), (4, 8), (8, 12), (12, 16), (16, 18))


def _silu(x):
    return x / (1.0 + jnp.exp(-x))


def _lane_lo(shape):
    return lax.broadcasted_iota(jnp.int32, shape, len(shape) - 1) % LANES < HEAD


def _mod_kernel(cond_ref, w_ref, b_ref, o_ref):
    s = _silu(cond_ref[...])
    o_ref[0] = jnp.dot(s.astype(BF16), w_ref[0].astype(BF16), preferred_element_type=F32) + b_ref[0]


def _modulation(cond, mod_w, mod_b):
    depth, d, d3 = mod_w.shape
    rows = cond.shape[0]
    return pl.pallas_call(
        _mod_kernel,
        grid=(depth, d3 // d),
        in_specs=[pl.BlockSpec((rows, d), lambda l, j: (0, 0)),
                  pl.BlockSpec((1, d, d), lambda l, j: (l, 0, j)),
                  pl.BlockSpec((1, 1, d), lambda l, j: (l, 0, j))],
        out_specs=pl.BlockSpec((1, rows, d), lambda l, j: (l, 0, j)),
        out_shape=jax.ShapeDtypeStruct((depth, rows, d3), F32),
        compiler_params=pltpu.CompilerParams(vmem_limit_bytes=VMEM_LIMIT),
        name="modulation",
    )(cond, mod_w, mod_b.reshape(depth, 1, d3))


def _rot_half(x, first_half):
    return jnp.where(first_half, pltpu.roll(x, LANES - HEAD // 2, 1), pltpu.roll(x, HEAD // 2, 1))


def _head_sumsq(x, bd):
    sq = x * x
    hi = sq.astype(BF16)
    lo = (sq - hi.astype(F32)).astype(BF16)
    return jnp.dot(hi, bd, preferred_element_type=F32) + jnp.dot(lo, bd, preferred_element_type=F32)


def _proj_kernel(*refs, ops, groups, use_rope, has_norm, vt_blocks):
    it = iter(refs)
    x_ref, mod_ref, g_ref, w_ref = next(it), next(it), next(it), next(it)
    cos_ref = sin_ref = qg_ref = kg_ref = bd_ref = vt_ref = None
    if use_rope:
        cos_ref, sin_ref = next(it), next(it)
    if has_norm:
        qg_ref, kg_ref, bd_ref = next(it), next(it), next(it)
    zq_ref, zg_ref = next(it), next(it)
    if vt_blocks:
        vt_ref = next(it)

    x = x_ref[...]
    h = x * lax.rsqrt(jnp.mean(x * x, axis=-1, keepdims=True) + EPS) * g_ref[...]
    h = h * (1.0 + mod_ref[0, 1:2, :]) + mod_ref[0, 0:1, :]
    hb = h.astype(BF16)

    tm = x.shape[0]
    if use_rope:
        cos, sin = cos_ref[...], sin_ref[...]
        first_half = lax.broadcasted_iota(jnp.int32, (tm, LANES), 1) % HEAD < HEAD // 2

    def rope(z):
        if not use_rope:
            return z
        return z * cos + _rot_half(z, first_half) * sin

    def head_norm(z, gain_ref):
        ss = _head_sumsq(z, bd_ref[...])
        return z * lax.rsqrt(ss * (1.0 / HEAD) + EPS) * gain_ref[...]

    q_scale = HEAD ** -0.5 * LOG2E
    for b0, b1 in groups:
        z = jnp.dot(hb, w_ref[:, b0 * LANES:b1 * LANES], preferred_element_type=F32)
        for j in range(b0, b1):
            zz = z[:, (j - b0) * LANES:(j - b0 + 1) * LANES]
            op = ops[j]
            if op == "qrope":
                zz = rope(zz * q_scale)
            elif op == "rope":
                zz = rope(zz)
            elif op == "kscale":
                zz = zz * (C_DK ** -0.5)
            elif op == "qnorm":
                zz = rope(head_norm(zz, qg_ref)) * q_scale
            elif op == "knorm":
                zz = rope(head_norm(zz, kg_ref))
            zq_ref[:, j * LANES:(j + 1) * LANES] = zz.astype(zq_ref.dtype)
            if j in vt_blocks:
                vt_ref[vt_blocks.index(j)] = zz.T.astype(vt_ref.dtype)
    half = N_GATE_BLK // 2
    for g0 in (0, half):
        c0 = (N_QKV_BLK + g0) * LANES
        z = jnp.dot(hb, w_ref[:, c0:c0 + half * LANES], preferred_element_type=F32)
        zg_ref[:, g0 * LANES:(g0 + half) * LANES] = _silu(z)


def _project(x2d, mod, mod_row_of_tile, norm_g, w_bf16, *, tm, kind, rope_tabs, norm_params, out_dtype,
             vt_blocks=()):
    t, d = x2d.shape
    out_specs = [pl.BlockSpec((tm, N_QKV_BLK * LANES), lambda i: (i, 0)),
                 pl.BlockSpec((tm, N_GATE_BLK * LANES), lambda i: (i, 0))]
    out_shape = [jax.ShapeDtypeStruct((t, N_QKV_BLK * LANES), out_dtype),
                 jax.ShapeDtypeStruct((t, N_GATE_BLK * LANES), F32)]
    if vt_blocks:
        out_specs.append(pl.BlockSpec((len(vt_blocks), LANES, tm), lambda i: (0, 0, i)))
        out_shape.append(jax.ShapeDtypeStruct((len(vt_blocks), LANES, t), out_dtype))
    use_rope = rope_tabs is not None
    has_norm = kind == "cd"
    ops, groups = (AB_OPS if kind == "ab" else CD_OPS), PROJ_GROUPS
    in_specs = [pl.BlockSpec((tm, d), lambda i: (i, 0)),
                pl.BlockSpec((1, 3, d), lambda i: (mod_row_of_tile(i), 0, 0)),
                pl.BlockSpec((1, d), lambda i: (0, 0)),
                pl.BlockSpec(w_bf16.shape, lambda i: (0, 0))]
    args = [x2d, mod, norm_g.reshape(1, d), w_bf16]
    if use_rope:
        n_seq_tiles = rope_tabs[0].shape[0] // tm
        in_specs += [pl.BlockSpec((tm, LANES), lambda i: (i % n_seq_tiles, 0))] * 2
        args += list(rope_tabs)
    if has_norm:
        in_specs += [pl.BlockSpec((1, LANES), lambda i: (0, 0))] * 2 + [pl.BlockSpec((LANES, LANES), lambda i: (0, 0))]
        args += list(norm_params)
    return pl.pallas_call(
        functools.partial(_proj_kernel, ops=ops, groups=groups, use_rope=use_rope, has_norm=has_norm,
                          vt_blocks=tuple(vt_blocks)),
        grid=(t // tm,),
        in_specs=in_specs,
        out_specs=out_specs,
        out_shape=out_shape,
        compiler_params=pltpu.CompilerParams(vmem_limit_bytes=VMEM_LIMIT),
        name="project_" + kind,
    )(*args)


def _pair_rows(q):
    lo = jnp.where(_lane_lo((1, LANES)), 1.0, 0.0).astype(BF16)
    qb = q.astype(BF16)
    return jnp.concatenate([qb * lo, qb * (1.0 - lo).astype(BF16)], axis=0)


def _pair_cols(q):
    qt = q.astype(F32).T
    top = lax.broadcasted_iota(jnp.int32, qt.shape, 0) < HEAD
    return jnp.concatenate([jnp.where(top, qt, 0.0), jnp.where(top, 0.0, qt)], axis=1).astype(BF16)


def _ctx_attn_kernel(*refs, mode, has_sink, kv_per_col, lam_init):
    it = iter(refs)
    q_ref, k_ref, v_ref = next(it), next(it), next(it)
    sink_ref = next(it) if has_sink else None
    if mode == "diff":
        lq1_ref, lk1_ref, lq2_ref, lk2_ref, bg_ref = (next(it) for _ in range(5))
    gate_ref, y_ref = next(it), next(it)

    tq = q_ref.shape[1]
    lo = _lane_lo((tq, LANES))
    for c in range(N_QCOL):
        cols = slice(c * LANES, (c + 1) * LANES)
        kv_cols = cols if kv_per_col else slice(0, LANES)
        qrows = _pair_rows(q_ref[0, :, cols])
        s = lax.dot_general(qrows, k_ref[0, :, kv_cols].astype(BF16), (((1,), (1,)), ((), ())),
                            preferred_element_type=F32)
        v = v_ref[0, :, kv_cols].astype(BF16)
        halves = []
        for half, head in ((s[:tq], c), (s[tq:], c + N_QCOL)):
            m = jnp.max(half, axis=-1, keepdims=True)
            if has_sink:
                sk = sink_ref[head] * LOG2E
                m = jnp.maximum(m, sk)
            p = jnp.exp2(half - m)
            l = jnp.sum(p, axis=-1, keepdims=True)
            if has_sink:
                l = l + jnp.exp2(sk - m)
            halves.append(jnp.dot(p.astype(BF16), v, preferred_element_type=F32) / l)
        o2 = jnp.concatenate(halves, axis=0)
        if mode == "pair":
            o = jnp.where(lo, o2[:tq], o2[tq:])
        else:
            lam = (jnp.exp(jnp.sum(lq1_ref[...] * lk1_ref[...], axis=-1, keepdims=True))
                   - jnp.exp(jnp.sum(lq2_ref[...] * lk2_ref[...], axis=-1, keepdims=True)) + lam_init)
            o = o2[:tq] - lam * o2[tq:]
            o = o * lax.rsqrt(jnp.mean(o * o, axis=-1, keepdims=True) + EPS) * bg_ref[...] * (1.0 - lam_init)
        y_ref[0, :, cols] = (o * gate_ref[0, :, cols]).astype(y_ref.dtype)


def _ctx_attention(zq, zg, *, q_blk, k_blk, v_blk, kv_per_col, gate_blk, mode,
                   sink=None, diff_params=None, lam_init=0.0):
    b, n, _ = zq.shape
    w4 = N_QCOL * LANES
    kv_w = w4 if kv_per_col else LANES
    assert (q_blk * LANES) % w4 == 0 and (gate_blk * LANES) % w4 == 0
    assert (k_blk * LANES) % kv_w == 0 and (v_blk * LANES) % kv_w == 0
    in_specs = [pl.BlockSpec((1, n, w4), lambda bi: (bi, 0, q_blk * LANES // w4)),
                pl.BlockSpec((1, n, kv_w), lambda bi: (bi, 0, k_blk * LANES // kv_w)),
                pl.BlockSpec((1, n, kv_w), lambda bi: (bi, 0, v_blk * LANES // kv_w))]
    args = [zq, zq, zq]
    if sink is not None:
        in_specs.append(pl.BlockSpec(memory_space=pltpu.SMEM))
        args.append(sink)
    if mode == "diff":
        in_specs += [pl.BlockSpec((1, HEAD), lambda bi: (0, 0))] * 4
        in_specs.append(pl.BlockSpec((1, LANES), lambda bi: (0, 0)))
        args += list(diff_params)
    in_specs.append(pl.BlockSpec((1, n, w4), lambda bi: (bi, 0, gate_blk * LANES // w4)))
    args.append(zg)
    return pl.pallas_call(
        functools.partial(_ctx_attn_kernel, mode=mode, has_sink=sink is not None, kv_per_col=kv_per_col,
                          lam_init=lam_init),
        grid=(b,),
        in_specs=in_specs,
        out_specs=pl.BlockSpec((1, n, w4), lambda bi: (bi, 0, 0)),
        out_shape=jax.ShapeDtypeStruct((b, n, w4), BF16),
        compiler_params=pltpu.CompilerParams(vmem_limit_bytes=VMEM_LIMIT),
        name="ctx_attention_" + mode,
    )(*args)


def _keymajor_attn_kernel(*refs, mode, tk, lam_init):
    it = iter(refs)
    q_ref, k_ref, vt_ref, kx_ref, vxt_ref = (next(it) for _ in range(5))
    if mode == "diff":
        lq1_ref, lk1_ref, lq2_ref, lk2_ref, bg_ref = (next(it) for _ in range(5))
    gate_ref, y_ref, s_scr, sx_scr = next(it), next(it), next(it), next(it)

    tq = q_ref.shape[1]
    qt = _pair_cols(q_ref[0])

    def scores(k, slot):
        s = jnp.dot(k, qt, preferred_element_type=F32)
        slot[...] = s
        return jnp.max(s, axis=0, keepdims=True)

    vd = HEAD if mode == "pair" else LANES

    def with_ones(vt):
        return jnp.concatenate([vt, jnp.ones((ONES_ROWS, vt.shape[1]), BF16)], axis=0)

    def consume(slot, cmax, vt, m, acc_lo, acc_hi):
        m_new = jnp.maximum(m, cmax)
        alpha = jnp.exp2(m - m_new)
        p = jnp.exp2(slot[...] - m_new).astype(BF16)
        if mode == "pair":
            v_lo, v_hi = with_ones(vt[:HEAD]), with_ones(vt[HEAD:])
        else:
            v_lo = v_hi = with_ones(vt)
        acc_lo = alpha[:, :tq] * acc_lo + jnp.dot(v_lo, p[:, :tq], preferred_element_type=F32)
        acc_hi = alpha[:, tq:] * acc_hi + jnp.dot(v_hi, p[:, tq:], preferred_element_type=F32)
        return m_new, acc_lo, acc_hi

    n_chunks = k_ref.shape[1] // tk
    carry = (jnp.full((1, 2 * tq), NEG, F32), jnp.zeros((vd + ONES_ROWS, tq), F32),
             jnp.zeros((vd + ONES_ROWS, tq), F32))
    n_slots = s_scr.shape[0]
    ahead = n_slots - 1
    chunk_scores = lambda j: scores(k_ref[0, j * tk:(j + 1) * tk, :], s_scr.at[j % n_slots])
    cmax = [chunk_scores(j) for j in range(min(ahead, n_chunks))]
    cmax_x = scores(kx_ref[0].astype(BF16), sx_scr)
    for j in range(n_chunks):
        if j + ahead < n_chunks:
            cmax.append(chunk_scores(j + ahead))
        carry = consume(s_scr.at[j % n_slots], cmax[j], vt_ref[0, :, j * tk:(j + 1) * tk], *carry)
    _, acc_lo, acc_hi = consume(sx_scr, cmax_x, vxt_ref[0].astype(BF16), *carry)

    ot_lo = acc_lo[:vd] / acc_lo[vd:vd + 1]
    ot_hi = acc_hi[:vd] / acc_hi[vd:vd + 1]
    if mode == "pair":
        o = jnp.concatenate([ot_lo, ot_hi], axis=0).T
    else:
        lam = (jnp.exp(jnp.sum(lq1_ref[...] * lk1_ref[...], axis=-1, keepdims=True))
               - jnp.exp(jnp.sum(lq2_ref[...] * lk2_ref[...], axis=-1, keepdims=True)) + lam_init)
        o = (ot_lo - lam * ot_hi).T
        o = o * lax.rsqrt(jnp.mean(o * o, axis=-1, keepdims=True) + EPS) * bg_ref[...] * (1.0 - lam_init)
    y_ref[0] = (o * gate_ref[0]).astype(y_ref.dtype)


def _keymajor_attention(zq, zg, vt, kx, vxt, *, q_blk, k_blk, vt_blk, kv_per_col, gate_blk, mode, tq, tk,
                        diff_params=None, lam_init=0.0):
    b, n, _ = zq.shape
    per = (lambda c: c) if kv_per_col else (lambda c: 0)
    if kx.ndim == 4:
        kx_spec = pl.BlockSpec((None, 1) + kx.shape[2:], lambda bi, c, qi: (bi, c, 0, 0))
        vx_spec = pl.BlockSpec((None, 1) + vxt.shape[2:], lambda bi, c, qi: (bi, c, 0, 0))
    else:
        kx_spec = pl.BlockSpec((1,) + kx.shape[1:], lambda bi, c, qi: (bi, 0, 0))
        vx_spec = pl.BlockSpec((1,) + vxt.shape[1:], lambda bi, c, qi: (bi, 0, 0))
    in_specs = [pl.BlockSpec((1, tq, LANES), lambda bi, c, qi: (bi, qi, q_blk + c)),
                pl.BlockSpec((1, n, LANES), lambda bi, c, qi: (bi, 0, k_blk + per(c))),
                pl.BlockSpec((1, LANES, n), lambda bi, c, qi: (vt_blk + per(c), 0, bi)),
                kx_spec, vx_spec]
    args = [zq, zq, vt, kx, vxt]
    if mode == "diff":
        in_specs += [pl.BlockSpec((1, HEAD), lambda bi, c, qi: (0, 0))] * 4
        in_specs.append(pl.BlockSpec((1, LANES), lambda bi, c, qi: (0, 0)))
        args += list(diff_params)
    in_specs.append(pl.BlockSpec((1, tq, LANES), lambda bi, c, qi: (bi, qi, gate_blk + c)))
    args.append(zg)
    return pl.pallas_call(
        functools.partial(_keymajor_attn_kernel, mode=mode, tk=tk, lam_init=lam_init),
        grid=(b, N_QCOL, n // tq),
        in_specs=in_specs,
        out_specs=pl.BlockSpec((1, tq, LANES), lambda bi, c, qi: (bi, qi, c)),
        out_shape=jax.ShapeDtypeStruct((b, n, N_QCOL * LANES), BF16),
        scratch_shapes=[pltpu.VMEM((SCORE_SLOTS, tk, 2 * tq), F32), pltpu.VMEM((kx.shape[-2], 2 * tq), F32)],
        compiler_params=pltpu.CompilerParams(vmem_limit_bytes=VMEM_LIMIT),
        name="attention_keymajor_" + mode,
    )(*args)


def _window_attn_kernel(q_ref, k_ref, vt_ref, kx_ref, vxt_ref, sink_ref, gate_ref, y_ref):
    n = k_ref.shape[1]
    span = 3 * CHUNK
    sk = jnp.concatenate([jnp.full((1, CHUNK), sink_ref[h] * LOG2E, F32) for h in range(GQA_HEADS)], axis=1)
    kx = kx_ref[0].astype(BF16)
    vxt = vxt_ref[0].astype(BF16)
    half = N_QCOL * CHUNK

    def with_ones(vt):
        return jnp.concatenate([vt, jnp.ones((ONES_ROWS, vt.shape[1]), BF16)], axis=0)

    for sub in range(WINDOW_QBLOCKS):
        blk = pl.program_id(1) * WINDOW_QBLOCKS + sub
        qts = [_pair_cols(q_ref[0, sub * CHUNK:(sub + 1) * CHUNK, c * LANES:(c + 1) * LANES])
               for c in range(N_QCOL)]
        qt = jnp.concatenate([t[:, :CHUNK] for t in qts] + [t[:, CHUNK:] for t in qts], axis=1)
        w0 = pl.multiple_of(jnp.clip((blk - 1) * CHUNK, 0, n - span), CHUNK)
        s_w = jnp.dot(k_ref[0, pl.ds(w0, span), :], qt, preferred_element_type=F32)
        s_x = jnp.dot(kx, qt, preferred_element_type=F32)
        dist = (w0 - blk * CHUNK + lax.broadcasted_iota(jnp.int32, (span, CHUNK), 0)
                - lax.broadcasted_iota(jnp.int32, (span, CHUNK), 1))
        bias = jnp.where((dist >= -WINDOW) & (dist <= WINDOW), 0.0, NEG)
        s_w = s_w + jnp.concatenate([bias] * GQA_HEADS, axis=1)
        m = jnp.maximum(jnp.maximum(jnp.max(s_w, axis=0, keepdims=True), jnp.max(s_x, axis=0, keepdims=True)), sk)
        p = jnp.concatenate([jnp.exp2(s_w - m), jnp.exp2(s_x - m)], axis=0).astype(BF16)
        vt = jnp.concatenate([vt_ref[0, :, pl.ds(w0, span)], vxt], axis=1)
        acc_lo = jnp.dot(with_ones(vt[:HEAD]), p[:, :half], preferred_element_type=F32)
        acc_hi = jnp.dot(with_ones(vt[HEAD:]), p[:, half:], preferred_element_type=F32)
        sink_p = jnp.exp2(sk - m)
        ot_lo = acc_lo[:HEAD] / (acc_lo[HEAD:HEAD + 1] + sink_p[:, :half])
        ot_hi = acc_hi[:HEAD] / (acc_hi[HEAD:HEAD + 1] + sink_p[:, half:])
        o = jnp.concatenate(
            [jnp.concatenate([ot_lo[:, c * CHUNK:(c + 1) * CHUNK], ot_hi[:, c * CHUNK:(c + 1) * CHUNK]], axis=0).T
             for c in range(N_QCOL)], axis=1)
        rows = slice(sub * CHUNK, (sub + 1) * CHUNK)
        y_ref[0, rows, :] = (o * gate_ref[0, rows, :]).astype(y_ref.dtype)


def _window_attention(zq, zg, vt, kx, vxt, sink):
    b, n, _ = zq.shape
    w4 = N_QCOL * LANES
    tq = WINDOW_QBLOCKS * CHUNK
    return pl.pallas_call(
        _window_attn_kernel,
        grid=(b, n // tq),
        in_specs=[pl.BlockSpec((1, tq, w4), lambda bi, i: (bi, i, 0)),
                  pl.BlockSpec((1, n, LANES), lambda bi, i: (bi, 0, PAIR_K_BLK)),
                  pl.BlockSpec((1, LANES, n), lambda bi, i: (0, 0, bi)),
                  pl.BlockSpec((1,) + kx.shape[1:], lambda bi, i: (bi, 0, 0)),
                  pl.BlockSpec((1,) + vxt.shape[1:], lambda bi, i: (bi, 0, 0)),
                  pl.BlockSpec(memory_space=pltpu.SMEM),
                  pl.BlockSpec((1, tq, w4), lambda bi, i: (bi, i, 0))],
        out_specs=pl.BlockSpec((1, tq, w4), lambda bi, i: (bi, i, 0)),
        out_shape=jax.ShapeDtypeStruct((b, n, w4), BF16),
        compiler_params=pltpu.CompilerParams(vmem_limit_bytes=VMEM_LIMIT),
        name="attention_window",
    )(zq, zq, vt, kx, vxt, sink, zg)


def _retention_kernel(*refs, has_state_in, emit_state):
    it = iter(refs)
    q_ref, k_ref, v_ref, decf_ref, decb_ref, cg_ref, gate_ref = (next(it) for _ in range(7))
    sf_in = sb_in = sf_out = sb_out = None
    if has_state_in:
        sf_in, sb_in = next(it), next(it)
    y_ref = next(it)
    if emit_state:
        sf_out, sb_out = next(it), next(it)
    o_scr, u_scr, dm_scr = next(it), next(it), next(it)

    nc = q_ref.shape[1] // CHUNK
    ri = lax.broadcasted_iota(jnp.int32, (CHUNK, CHUNK), 0).astype(F32)
    ci = lax.broadcasted_iota(jnp.int32, (CHUNK, CHUNK), 1).astype(F32)
    tok_col = lax.broadcasted_iota(jnp.int32, (CHUNK, 1), 0).astype(F32)
    tok_row = lax.broadcasted_iota(jnp.int32, (1, CHUNK), 1).astype(F32)

    lg_f = -jnp.exp(decf_ref[0][:, :1])
    lg_b = -jnp.exp(decb_ref[0][:, :1])
    cross_f, cross_b = jnp.exp(lg_f * (tok_col + 1.0)), jnp.exp(lg_b * (CHUNK - 1.0 - tok_col))
    kdec_f, kdec_b = jnp.exp(lg_f * (CHUNK - 1.0 - tok_row)), jnp.exp(lg_b * tok_row)
    cdec_f, cdec_b = jnp.exp(lg_f * CHUNK), jnp.exp(lg_b * CHUNK)
    rel_f, rel_b = ri - ci, ci - ri - 1.0
    dm_scr[...] = jnp.where(rel_f >= 0.0, jnp.exp(lg_f * jnp.maximum(rel_f, 0.0)),
                            jnp.exp(lg_b * jnp.maximum(rel_b, 0.0)))

    def chunk_rows(c):
        return pl.ds(pl.multiple_of(c * CHUNK, CHUNK), CHUNK)

    def local_body(c, _):
        rows = chunk_rows(c)
        qh = q_ref[0, rows, :].astype(BF16)
        kf = k_ref[0, rows, :].astype(F32)
        vh = v_ref[0, rows, :].astype(BF16)
        kt = kf.T
        att = jnp.dot(qh, kt.astype(BF16), preferred_element_type=F32) * dm_scr[...]
        lhs = jnp.concatenate([att.astype(BF16), (kt * kdec_f).astype(BF16), (kt * kdec_b).astype(BF16)], axis=0)
        r = jnp.dot(lhs, vh, preferred_element_type=F32)
        o_scr[rows, :] = r[:CHUNK]
        u_scr[c, 0] = r[CHUNK:2 * CHUNK]
        u_scr[c, 1] = r[2 * CHUNK:]
        return 0
    lax.fori_loop(0, nc, local_body, 0, unroll=min(nc, RET_UNROLL))

    def scan_body(t, states):
        s_f, s_b = states
        inc_f, inc_b = u_scr[t, 0], u_scr[nc - 1 - t, 1]
        u_scr[t, 0] = s_f
        u_scr[nc - 1 - t, 1] = s_b
        return cdec_f * s_f + inc_f, cdec_b * s_b + inc_b
    zero = jnp.zeros((CHUNK, CHUNK), F32)
    s_f, s_b = lax.fori_loop(0, nc, scan_body, (sf_in[0, 0], sb_in[0, 0]) if has_state_in else (zero, zero),
                             unroll=2)
    if emit_state:
        sf_out[0, 0] = s_f.astype(sf_out.dtype)
        sb_out[0, 0] = s_b.astype(sb_out.dtype)

    def finish_body(c, _):
        rows = chunk_rows(c)
        qh = q_ref[0, rows, :].astype(BF16)
        states = jnp.concatenate([u_scr[c, 0], u_scr[c, 1]], axis=1).astype(BF16)
        r = jnp.dot(qh, states, preferred_element_type=F32)
        o = o_scr[rows, :] + cross_f * r[:, :LANES] + cross_b * r[:, LANES:]
        mu = jnp.mean(o, axis=-1, keepdims=True)
        d = o - mu
        var = jnp.mean(d * d, axis=-1, keepdims=True)
        y = d * lax.rsqrt(var + EPS) * cg_ref[0]
        y_ref[0, rows, :] = (y * gate_ref[0, rows, :]).astype(y_ref.dtype)
        return 0
    lax.fori_loop(0, nc, finish_body, 0, unroll=min(nc, RET_UNROLL))


def _retention(zq, zg, dec_f, dec_b, c_norm_g, states, emit_state):
    b, n, _ = zq.shape
    seq_spec = lambda blk: pl.BlockSpec((1, n, LANES), lambda bi, h: (bi, 0, blk + h))
    st_spec = pl.BlockSpec((1, 1, CHUNK, CHUNK), lambda bi, h: (bi, h, 0, 0))
    head_spec = pl.BlockSpec((1, 1, LANES), lambda bi, h: (h, 0, 0))
    bcast = lambda p: jnp.broadcast_to(p.astype(F32)[:, None, None], (C_HEADS, 1, LANES))
    in_specs = [seq_spec(0), seq_spec(C_HEADS), seq_spec(2 * C_HEADS), head_spec, head_spec, head_spec,
                seq_spec(0)]
    args = [zq, zq, zq, bcast(dec_f), bcast(dec_b), c_norm_g.astype(F32)[:, None, :], zg]
    if states is not None:
        in_specs += [st_spec, st_spec]
        args += list(states)
    out_specs = [seq_spec(0)]
    out_shape = [jax.ShapeDtypeStruct((b, n, C_HEADS * LANES), BF16)]
    if emit_state:
        out_specs += [st_spec, st_spec]
        out_shape += [jax.ShapeDtypeStruct((b, C_HEADS, CHUNK, CHUNK), F32)] * 2
    return pl.pallas_call(
        functools.partial(_retention_kernel, has_state_in=states is not None, emit_state=emit_state),
        grid=(b, C_HEADS),
        in_specs=in_specs,
        out_specs=out_specs,
        out_shape=out_shape,
        scratch_shapes=[pltpu.VMEM((n, LANES), F32),
                        pltpu.VMEM((n // CHUNK, 2, CHUNK, CHUNK), F32),
                        pltpu.VMEM((CHUNK, CHUNK), F32)],
        compiler_params=pltpu.CompilerParams(vmem_limit_bytes=VMEM_LIMIT),
        name="retention",
    )(*args)


def _out_kernel(y1_ref, y2_ref, x_ref, mod_ref, w_ref, fg_ref, o_ref, *, final_norm):
    half = y1_ref.shape[1]
    y = (jnp.dot(y1_ref[...], w_ref[:half, :], preferred_element_type=F32)
         + jnp.dot(y2_ref[...], w_ref[half:, :], preferred_element_type=F32))
    x = x_ref[...] + mod_ref[0, 2:3, :] * y
    if final_norm:
        x = x * lax.rsqrt(jnp.mean(x * x, axis=-1, keepdims=True) + EPS) * fg_ref[...]
    o_ref[...] = x


def _out_project(y1, y2, x2d, mod, mod_row_of_tile, w_bf16, final_g, *, tm, final_norm):
    t, d = x2d.shape
    half = y1.shape[1]
    return pl.pallas_call(
        functools.partial(_out_kernel, final_norm=final_norm),
        grid=(t // tm,),
        in_specs=[pl.BlockSpec((tm, half), lambda i: (i, 0)),
                  pl.BlockSpec((tm, half), lambda i: (i, 0)),
                  pl.BlockSpec((tm, d), lambda i: (i, 0)),
                  pl.BlockSpec((1, 3, d), lambda i: (mod_row_of_tile(i), 0, 0)),
                  pl.BlockSpec(w_bf16.shape, lambda i: (0, 0)),
                  pl.BlockSpec((1, d), lambda i: (0, 0))],
        out_specs=pl.BlockSpec((tm, d), lambda i: (i, 0)),
        out_shape=jax.ShapeDtypeStruct((t, d), F32),
        compiler_params=pltpu.CompilerParams(vmem_limit_bytes=VMEM_LIMIT),
        name="out_project",
    )(y1, y2, x2d, mod, w_bf16, final_g.reshape(1, d))


def _pair_perm():
    return np.concatenate([np.arange(h * HEAD, (h + 1) * HEAD) for h in PAIR_ORDER])


def _reorder(w, idx, axis):
    idx = np.asarray(idx)
    runs = np.split(idx, np.flatnonzero(np.diff(idx) != 1) + 1)
    return jnp.concatenate([lax.slice_in_dim(w, int(r[0]), int(r[-1]) + 1, axis=axis) for r in runs], axis=axis)


def _rope_tables(n):
    rows = n // GRID_W
    row = jnp.repeat(jnp.arange(rows, dtype=F32), GRID_W)
    col = jnp.tile(jnp.arange(GRID_W, dtype=F32), rows)
    nf = HEAD // 4
    inv = ROPE_THETA ** (-jnp.arange(nf, dtype=F32) / nf)
    ang = jnp.concatenate([row[:, None] * inv, col[:, None] * inv], axis=-1)
    cos, sin = jnp.cos(ang), jnp.sin(ang)
    return jnp.tile(cos, (1, 4)), jnp.tile(jnp.concatenate([-sin, sin], axis=-1), (1, 2))


def _pair_kv(cache):
    b, g, p, d = cache.shape
    return cache.transpose(0, 2, 1, 3).reshape(b, p, g * d)


def _unpair_kv(z, blk, n_blk, heads):
    b, p, _ = z.shape
    t = z[:, :, blk * LANES:(blk + n_blk) * LANES]
    return t.reshape(b, p, heads, t.shape[-1] // heads).transpose(0, 2, 1, 3)


def kernel(x_prompt, x_sample, cache_a_k, cache_a_v, cache_b_k, cache_b_v, state_c_fwd, state_c_bwd, cache_d_k, cache_d_v, c, c_ctx, norm_g, mod_w, mod_b, ab_w_in, ab_w_out, a_sink, b_lq1, b_lk1, b_lq2, b_lk2, b_norm_g, cd_w_in, cd_w_out, c_decay_f, c_decay_b, c_norm_g, d_q_norm_g, d_k_norm_g, final_g):
    depth = norm_g.shape[0]
    bp, sp, d = x_prompt.shape
    bs, ss, _ = x_sample.shape
    dt = x_prompt.dtype

    ctx_row = bs
    pad = (-(bs + 1)) % 8
    cond = jnp.concatenate([c, c_ctx[None, :], jnp.zeros((pad, d), c.dtype)], axis=0)
    mod = _modulation(cond, mod_w, mod_b).reshape(depth, cond.shape[0], 3, d)

    perm = _pair_perm()
    qkv_w = N_QKV_BLK * LANES
    rope_tabs = _rope_tables(ss)
    bd = jnp.asarray(np.kron(np.eye(LANES // HEAD), np.ones((HEAD, HEAD))), BF16)

    tm_s = 512
    tm_p = 512
    tiles_per_seq = ss // tm_s
    row_s = lambda i: i // tiles_per_seq
    row_p = lambda i: ctx_row

    xp = x_prompt.reshape(bp * sp, d)
    xs = x_sample.reshape(bs * ss, d)
    outs = {k: [] for k in ("a_k", "a_v", "b_k", "b_v", "c_f", "c_b", "d_k", "d_v")}

    for layer in range(depth):
        i = layer // 2
        last = layer == depth - 1
        if layer % 2 == 0:
            lam_init = 0.8 - 0.6 * math.exp(-0.3 * layer)
            w4 = N_QCOL * LANES
            ar = np.arange
            cols = np.concatenate([perm, ar(w4 + 2 * LANES, 4 * w4 + 2 * LANES), ar(w4, w4 + 2 * LANES),
                                   qkv_w + perm, ar(qkv_w + w4, qkv_w + 2 * w4)])
            w_in = _reorder(ab_w_in[i], cols, 1).astype(BF16)
            rows = np.arange(ab_w_out.shape[1])
            rows[0:N_QCOL * LANES] = perm
            w_out = _reorder(ab_w_out[i], rows, 0).astype(BF16)
            diff_params = (b_lq1[i][None], b_lk1[i][None], b_lq2[i][None], b_lk2[i][None], b_norm_g[i][None])

            zq, zg = _project(xp, mod[layer], row_p, norm_g[layer], w_in, tm=tm_p, kind="ab",
                              rope_tabs=None, norm_params=None, out_dtype=F32)
            zq3, zg3 = zq.reshape(bp, sp, -1), zg.reshape(bp, sp, -1)
            ya = _ctx_attention(zq3, zg3, q_blk=0, k_blk=PAIR_K_BLK, v_blk=PAIR_V_BLK, kv_per_col=False,
                                gate_blk=0, mode="pair", sink=a_sink[i])
            yb = _ctx_attention(zq3, zg3, q_blk=4, k_blk=8, v_blk=12, kv_per_col=True, gate_blk=4,
                                mode="diff", diff_params=diff_params, lam_init=lam_init)
            xp = _out_project(ya.reshape(bp * sp, -1), yb.reshape(bp * sp, -1), xp, mod[layer], row_p,
                              w_out, final_g, tm=tm_p, final_norm=last)
            outs["a_k"].append(_unpair_kv(zq3, PAIR_K_BLK, 1, 2))
            outs["a_v"].append(_unpair_kv(zq3, PAIR_V_BLK, 1, 2))
            outs["b_k"].append(_unpair_kv(zq3, 8, 4, 4))
            outs["b_v"].append(_unpair_kv(zq3, 12, 4, 4))

            zq, zg, vt = _project(xs, mod[layer], row_s, norm_g[layer], w_in, tm=tm_s, kind="ab",
                                  rope_tabs=rope_tabs, norm_params=None, out_dtype=BF16,
                                  vt_blocks=(PAIR_V_BLK, 12, 13, 14, 15))
            zq3, zg3 = zq.reshape(bs, ss, -1), zg.reshape(bs, ss, -1)
            ya = _window_attention(zq3, zg3, vt, _pair_kv(cache_a_k[:, i]),
                                   _pair_kv(cache_a_v[:, i]).transpose(0, 2, 1), a_sink[i])
            yb = _keymajor_attention(zq3, zg3, vt, cache_b_k[:, i], cache_b_v[:, i].transpose(0, 1, 3, 2),
                                     q_blk=4, k_blk=8, vt_blk=1, kv_per_col=True, gate_blk=4, mode="diff",
                                     tq=TQ_DENSE, tk=TK_DENSE, diff_params=diff_params, lam_init=lam_init)
            xs = _out_project(ya.reshape(bs * ss, -1), yb.reshape(bs * ss, -1), xs, mod[layer], row_s,
                              w_out, final_g, tm=tm_s, final_norm=last)
        else:
            cols = np.arange(cd_w_in.shape[2])
            cols[12 * LANES:16 * LANES] = 12 * LANES + perm
            cols[qkv_w + N_QCOL * LANES:qkv_w + 2 * N_QCOL * LANES] = qkv_w + N_QCOL * LANES + perm
            w_in = _reorder(cd_w_in[i], cols, 1).astype(BF16)
            rows = np.arange(cd_w_out.shape[1])
            rows[N_QCOL * LANES:] = N_QCOL * LANES + perm
            w_out = _reorder(cd_w_out[i], rows, 0).astype(BF16)
            norm_params = (jnp.tile(d_q_norm_g[i], 2)[None], jnp.tile(d_k_norm_g[i], 2)[None], bd)

            zq, zg = _project(xp, mod[layer], row_p, norm_g[layer], w_in, tm=tm_p, kind="cd",
                              rope_tabs=None, norm_params=norm_params, out_dtype=F32)
            zq3, zg3 = zq.reshape(bp, sp, -1), zg.reshape(bp, sp, -1)
            yc, s_f, s_b = _retention(zq3, zg3, c_decay_f[i], c_decay_b[i], c_norm_g[i], None, True)
            yd = _ctx_attention(zq3, zg3, q_blk=12, k_blk=PAIR_K_BLK, v_blk=PAIR_V_BLK, kv_per_col=False,
                                gate_blk=4, mode="pair")
            xp = _out_project(yc.reshape(bp * sp, -1), yd.reshape(bp * sp, -1), xp, mod[layer], row_p,
                              w_out, final_g, tm=tm_p, final_norm=last)
            outs["c_f"].append(s_f.astype(dt))
            outs["c_b"].append(s_b.astype(dt))
            outs["d_k"].append(_unpair_kv(zq3, 16, 1, 2))
            outs["d_v"].append(_unpair_kv(zq3, 17, 1, 2))

            zq, zg, vt = _project(xs, mod[layer], row_s, norm_g[layer], w_in, tm=tm_s, kind="cd",
                                  rope_tabs=rope_tabs, norm_params=norm_params, out_dtype=BF16,
                                  vt_blocks=(PAIR_V_BLK,))
            zq3, zg3 = zq.reshape(bs, ss, -1), zg.reshape(bs, ss, -1)
            yc = _retention(zq3, zg3, c_decay_f[i], c_decay_b[i], c_norm_g[i],
                            (state_c_fwd[:, i], state_c_bwd[:, i]), False)[0]
            yd = _keymajor_attention(zq3, zg3, vt, _pair_kv(cache_d_k[:, i]),
                                     _pair_kv(cache_d_v[:, i]).transpose(0, 2, 1),
                                     q_blk=12, k_blk=PAIR_K_BLK, vt_blk=0, kv_per_col=False, gate_blk=4,
                                     mode="pair",
                                     tq=TQ_DENSE, tk=TK_DENSE)
            xs = _out_project(yc.reshape(bs * ss, -1), yd.reshape(bs * ss, -1), xs, mod[layer], row_s,
                              w_out, final_g, tm=tm_s, final_norm=last)

    stack = lambda k: jnp.stack(outs[k], axis=1)
    return (xp.reshape(bp, sp, d), xs.reshape(bs, ss, d), stack("a_k"), stack("a_v"), stack("b_k"), stack("b_v"),
            stack("c_f"), stack("c_b"), stack("d_k"), stack("d_v"))
```

```python
import functools
import math

import numpy as np
import jax
import jax.numpy as jnp
from jax import lax
from jax.experimental import pallas as pl
from jax.experimental.pallas import tpu as pltpu

F32 = jnp.float32
BF16 = jnp.bfloat16

LANES = 128
HEAD = 64
GRID_W = 64
CHUNK = 128
WINDOW = 128
ROPE_THETA = 10000.0
EPS = 1e-6
NEG = -1e30
VMEM_LIMIT = 56 * 1024 * 1024
TQ_DENSE = 2048
TK_DENSE = 512
SCORE_SLOTS = 4
WINDOW_QBLOCKS = 4
RET_UNROLL = 8
ONES_ROWS = 16
LOG2E = 1.4426950408889634

GQA_HEADS = 8
PAIR_ORDER = (0, 4, 1, 5, 2, 6, 3, 7)
N_QCOL = 4
C_HEADS = 4
C_DK = 128

N_QKV_BLK = 18
N_GATE_BLK = 8
PAIR_K_BLK, PAIR_V_BLK = 16, 17
AB_OPS = ("qrope",) * 8 + ("rope",) * 4 + ("plain",) * 4 + ("rope", "plain")
CD_OPS = ("plain",) * 4 + ("kscale",) * 4 + ("plain",) * 4 + ("qnorm",) * 4 + ("knorm", "plain")
PROJ_GROUPS = ((0, 4), (4, 8), (8, 12), (12, 16), (16, 18))


def _silu(x):
    return x / (1.0 + jnp.exp(-x))


def _lane_lo(shape):
    return lax.broadcasted_iota(jnp.int32, shape, len(shape) - 1) % LANES < HEAD


def _mod_kernel(cond_ref, w_ref, b_ref, o_ref):
    s = _silu(cond_ref[...])
    o_ref[0] = jnp.dot(s.astype(BF16), w_ref[0].astype(BF16), preferred_element_type=F32) + b_ref[0]


def _modulation(cond, mod_w, mod_b):
    depth, d, d3 = mod_w.shape
    rows = cond.shape[0]
    return pl.pallas_call(
        _mod_kernel,
        grid=(depth, d3 // d),
        in_specs=[pl.BlockSpec((rows, d), lambda l, j: (0, 0)),
                  pl.BlockSpec((1, d, d), lambda l, j: (l, 0, j)),
                  pl.BlockSpec((1, 1, d), lambda l, j: (l, 0, j))],
        out_specs=pl.BlockSpec((1, rows, d), lambda l, j: (l, 0, j)),
        out_shape=jax.ShapeDtypeStruct((depth, rows, d3), F32),
        compiler_params=pltpu.CompilerParams(vmem_limit_bytes=VMEM_LIMIT),
        name="modulation",
    )(cond, mod_w, mod_b.reshape(depth, 1, d3))


def _rot_half(x, first_half):
    return jnp.where(first_half, pltpu.roll(x, LANES - HEAD // 2, 1), pltpu.roll(x, HEAD // 2, 1))


def _head_sumsq(x, bd):
    sq = x * x
    hi = sq.astype(BF16)
    lo = (sq - hi.astype(F32)).astype(BF16)
    return jnp.dot(hi, bd, preferred_element_type=F32) + jnp.dot(lo, bd, preferred_element_type=F32)


def _proj_kernel(*refs, ops, groups, use_rope, has_norm, vt_blocks, has_prev):
    it = iter(refs)
    x_ref, mod_ref, g_ref, w_ref = next(it), next(it), next(it), next(it)
    cos_ref = sin_ref = qg_ref = kg_ref = bd_ref = vt_ref = None
    if has_prev:
        y1_ref, y2_ref, pmod_ref, pw_ref = next(it), next(it), next(it), next(it)
    if use_rope:
        cos_ref, sin_ref = next(it), next(it)
    if has_norm:
        qg_ref, kg_ref, bd_ref = next(it), next(it), next(it)
    zq_ref, zg_ref = next(it), next(it)
    if vt_blocks:
        vt_ref = next(it)
    if has_prev:
        xnew_ref = next(it)

    x = x_ref[...]
    if has_prev:
        half = y1_ref.shape[1]
        y = (jnp.dot(y1_ref[...], pw_ref[:half, :], preferred_element_type=F32)
             + jnp.dot(y2_ref[...], pw_ref[half:, :], preferred_element_type=F32))
        x = x + pmod_ref[0, 2:3, :] * y
        xnew_ref[...] = x
    h = x * lax.rsqrt(jnp.mean(x * x, axis=-1, keepdims=True) + EPS) * g_ref[...]
    h = h * (1.0 + mod_ref[0, 1:2, :]) + mod_ref[0, 0:1, :]
    hb = h.astype(BF16)

    tm = x.shape[0]
    if use_rope:
        cos, sin = cos_ref[...], sin_ref[...]
        first_half = lax.broadcasted_iota(jnp.int32, (tm, LANES), 1) % HEAD < HEAD // 2

    def rope(z):
        if not use_rope:
            return z
        return z * cos + _rot_half(z, first_half) * sin

    def head_norm(z, gain_ref):
        ss = _head_sumsq(z, bd_ref[...])
        return z * lax.rsqrt(ss * (1.0 / HEAD) + EPS) * gain_ref[...]

    q_scale = HEAD ** -0.5 * LOG2E
    for b0, b1 in groups:
        z = jnp.dot(hb, w_ref[:, b0 * LANES:b1 * LANES], preferred_element_type=F32)
        for j in range(b0, b1):
            zz = z[:, (j - b0) * LANES:(j - b0 + 1) * LANES]
            op = ops[j]
            if op == "qrope":
                zz = rope(zz * q_scale)
            elif op == "rope":
                zz = rope(zz)
            elif op == "kscale":
                zz = zz * (C_DK ** -0.5)
            elif op == "qnorm":
                zz = rope(head_norm(zz, qg_ref)) * q_scale
            elif op == "knorm":
                zz = rope(head_norm(zz, kg_ref))
            zq_ref[:, j * LANES:(j + 1) * LANES] = zz.astype(zq_ref.dtype)
            if j in vt_blocks:
                vt_ref[vt_blocks.index(j)] = zz.T.astype(vt_ref.dtype)
    half = N_GATE_BLK // 2
    for g0 in (0, half):
        c0 = (N_QKV_BLK + g0) * LANES
        z = jnp.dot(hb, w_ref[:, c0:c0 + half * LANES], preferred_element_type=F32)
        zg_ref[:, g0 * LANES:(g0 + half) * LANES] = _silu(z)


def _project(x2d, mod, mod_row_of_tile, norm_g, w_bf16, *, tm, kind, rope_tabs, norm_params, out_dtype,
             vt_blocks=(), prev=None):
    t, d = x2d.shape
    out_specs = [pl.BlockSpec((tm, N_QKV_BLK * LANES), lambda i: (i, 0)),
                 pl.BlockSpec((tm, N_GATE_BLK * LANES), lambda i: (i, 0))]
    out_shape = [jax.ShapeDtypeStruct((t, N_QKV_BLK * LANES), out_dtype),
                 jax.ShapeDtypeStruct((t, N_GATE_BLK * LANES), F32)]
    if vt_blocks:
        out_specs.append(pl.BlockSpec((len(vt_blocks), LANES, tm), lambda i: (0, 0, i)))
        out_shape.append(jax.ShapeDtypeStruct((len(vt_blocks), LANES, t), out_dtype))
    if prev is not None:
        out_specs.append(pl.BlockSpec((tm, d), lambda i: (i, 0)))
        out_shape.append(jax.ShapeDtypeStruct((t, d), F32))
    use_rope = rope_tabs is not None
    has_norm = kind == "cd"
    ops, groups = (AB_OPS if kind == "ab" else CD_OPS), PROJ_GROUPS
    in_specs = [pl.BlockSpec((tm, d), lambda i: (i, 0)),
                pl.BlockSpec((1, 3, d), lambda i: (mod_row_of_tile(i), 0, 0)),
                pl.BlockSpec((1, d), lambda i: (0, 0)),
                pl.BlockSpec(w_bf16.shape, lambda i: (0, 0))]
    args = [x2d, mod, norm_g.reshape(1, d), w_bf16]
    if prev is not None:
        y1, y2, prev_mod, prev_w = prev
        in_specs += [pl.BlockSpec((tm, y1.shape[1]), lambda i: (i, 0)),
                     pl.BlockSpec((tm, y2.shape[1]), lambda i: (i, 0)),
                     pl.BlockSpec((1, 3, d), lambda i: (mod_row_of_tile(i), 0, 0)),
                     pl.BlockSpec(prev_w.shape, lambda i: (0, 0))]
        args += [y1, y2, prev_mod, prev_w]
    if use_rope:
        n_seq_tiles = rope_tabs[0].shape[0] // tm
        in_specs += [pl.BlockSpec((tm, LANES), lambda i: (i % n_seq_tiles, 0))] * 2
        args += list(rope_tabs)
    if has_norm:
        in_specs += [pl.BlockSpec((1, LANES), lambda i: (0, 0))] * 2 + [pl.BlockSpec((LANES, LANES), lambda i: (0, 0))]
        args += list(norm_params)
    return pl.pallas_call(
        functools.partial(_proj_kernel, ops=ops, groups=groups, use_rope=use_rope, has_norm=has_norm,
                          vt_blocks=tuple(vt_blocks), has_prev=prev is not None),
        grid=(t // tm,),
        in_specs=in_specs,
        out_specs=out_specs,
        out_shape=out_shape,
        compiler_params=pltpu.CompilerParams(vmem_limit_bytes=VMEM_LIMIT),
        name="project_" + kind,
    )(*args)


def _pair_rows(q):
    lo = jnp.where(_lane_lo((1, LANES)), 1.0, 0.0).astype(BF16)
    qb = q.astype(BF16)
    return jnp.concatenate([qb * lo, qb * (1.0 - lo).astype(BF16)], axis=0)


def _pair_cols(q):
    qt = q.astype(F32).T
    top = lax.broadcasted_iota(jnp.int32, qt.shape, 0) < HEAD
    return jnp.concatenate([jnp.where(top, qt, 0.0), jnp.where(top, 0.0, qt)], axis=1).astype(BF16)


def _ctx_attn_kernel(*refs, mode, has_sink, kv_per_col, lam_init):
    it = iter(refs)
    q_ref, k_ref, v_ref = next(it), next(it), next(it)
    sink_ref = next(it) if has_sink else None
    if mode == "diff":
        lq1_ref, lk1_ref, lq2_ref, lk2_ref, bg_ref = (next(it) for _ in range(5))
    gate_ref, y_ref = next(it), next(it)

    tq = q_ref.shape[1]
    lo = _lane_lo((tq, LANES))
    for c in range(N_QCOL):
        cols = slice(c * LANES, (c + 1) * LANES)
        kv_cols = cols if kv_per_col else slice(0, LANES)
        qrows = _pair_rows(q_ref[0, :, cols])
        s = lax.dot_general(qrows, k_ref[0, :, kv_cols].astype(BF16), (((1,), (1,)), ((), ())),
                            preferred_element_type=F32)
        v = v_ref[0, :, kv_cols].astype(BF16)
        halves = []
        for half, head in ((s[:tq], c), (s[tq:], c + N_QCOL)):
            m = jnp.max(half, axis=-1, keepdims=True)
            if has_sink:
                sk = sink_ref[head] * LOG2E
                m = jnp.maximum(m, sk)
            p = jnp.exp2(half - m)
            l = jnp.sum(p, axis=-1, keepdims=True)
            if has_sink:
                l = l + jnp.exp2(sk - m)
            halves.append(jnp.dot(p.astype(BF16), v, preferred_element_type=F32) / l)
        o2 = jnp.concatenate(halves, axis=0)
        if mode == "pair":
            o = jnp.where(lo, o2[:tq], o2[tq:])
        else:
            lam = (jnp.exp(jnp.sum(lq1_ref[...] * lk1_ref[...], axis=-1, keepdims=True))
                   - jnp.exp(jnp.sum(lq2_ref[...] * lk2_ref[...], axis=-1, keepdims=True)) + lam_init)
            o = o2[:tq] - lam * o2[tq:]
            o = o * lax.rsqrt(jnp.mean(o * o, axis=-1, keepdims=True) + EPS) * bg_ref[...] * (1.0 - lam_init)
        y_ref[0, :, cols] = (o * gate_ref[0, :, cols]).astype(y_ref.dtype)


def _ctx_attention(zq, zg, *, q_blk, k_blk, v_blk, kv_per_col, gate_blk, mode,
                   sink=None, diff_params=None, lam_init=0.0):
    b, n, _ = zq.shape
    w4 = N_QCOL * LANES
    kv_w = w4 if kv_per_col else LANES
    assert (q_blk * LANES) % w4 == 0 and (gate_blk * LANES) % w4 == 0
    assert (k_blk * LANES) % kv_w == 0 and (v_blk * LANES) % kv_w == 0
    in_specs = [pl.BlockSpec((1, n, w4), lambda bi: (bi, 0, q_blk * LANES // w4)),
                pl.BlockSpec((1, n, kv_w), lambda bi: (bi, 0, k_blk * LANES // kv_w)),
                pl.BlockSpec((1, n, kv_w), lambda bi: (bi, 0, v_blk * LANES // kv_w))]
    args = [zq, zq, zq]
    if sink is not None:
        in_specs.append(pl.BlockSpec(memory_space=pltpu.SMEM))
        args.append(sink)
    if mode == "diff":
        in_specs += [pl.BlockSpec((1, HEAD), lambda bi: (0, 0))] * 4
        in_specs.append(pl.BlockSpec((1, LANES), lambda bi: (0, 0)))
        args += list(diff_params)
    in_specs.append(pl.BlockSpec((1, n, w4), lambda bi: (bi, 0, gate_blk * LANES // w4)))
    args.append(zg)
    return pl.pallas_call(
        functools.partial(_ctx_attn_kernel, mode=mode, has_sink=sink is not None, kv_per_col=kv_per_col,
                          lam_init=lam_init),
        grid=(b,),
        in_specs=in_specs,
        out_specs=pl.BlockSpec((1, n, w4), lambda bi: (bi, 0, 0)),
        out_shape=jax.ShapeDtypeStruct((b, n, w4), BF16),
        compiler_params=pltpu.CompilerParams(vmem_limit_bytes=VMEM_LIMIT),
        name="ctx_attention_" + mode,
    )(*args)


def _keymajor_attn_kernel(*refs, mode, tk, lam_init):
    it = iter(refs)
    q_ref, k_ref, vt_ref, kx_ref, vxt_ref = (next(it) for _ in range(5))
    if mode == "diff":
        lq1_ref, lk1_ref, lq2_ref, lk2_ref, bg_ref = (next(it) for _ in range(5))
    gate_ref, y_ref, s_scr, sx_scr = next(it), next(it), next(it), next(it)

    tq = q_ref.shape[1]
    qt = _pair_cols(q_ref[0])

    def scores(k, slot):
        s = jnp.dot(k, qt, preferred_element_type=F32)
        slot[...] = s
        return jnp.max(s, axis=0, keepdims=True)

    vd = HEAD if mode == "pair" else LANES

    def with_ones(vt):
        return jnp.concatenate([vt, jnp.ones((ONES_ROWS, vt.shape[1]), BF16)], axis=0)

    def consume(slot, cmax, vt, m, acc_lo, acc_hi):
        m_new = jnp.maximum(m, cmax)
        alpha = jnp.exp2(m - m_new)
        p = jnp.exp2(slot[...] - m_new).astype(BF16)
        if mode == "pair":
            v_lo, v_hi = with_ones(vt[:HEAD]), with_ones(vt[HEAD:])
        else:
            v_lo = v_hi = with_ones(vt)
        acc_lo = alpha[:, :tq] * acc_lo + jnp.dot(v_lo, p[:, :tq], preferred_element_type=F32)
        acc_hi = alpha[:, tq:] * acc_hi + jnp.dot(v_hi, p[:, tq:], preferred_element_type=F32)
        return m_new, acc_lo, acc_hi

    n_chunks = k_ref.shape[1] // tk
    carry = (jnp.full((1, 2 * tq), NEG, F32), jnp.zeros((vd + ONES_ROWS, tq), F32),
             jnp.zeros((vd + ONES_ROWS, tq), F32))
    n_slots = s_scr.shape[0]
    ahead = n_slots - 1
    chunk_scores = lambda j: scores(k_ref[0, j * tk:(j + 1) * tk, :], s_scr.at[j % n_slots])
    cmax = [chunk_scores(j) for j in range(min(ahead, n_chunks))]
    cmax_x = scores(kx_ref[0].astype(BF16), sx_scr)
    for j in range(n_chunks):
        if j + ahead < n_chunks:
            cmax.append(chunk_scores(j + ahead))
        carry = consume(s_scr.at[j % n_slots], cmax[j], vt_ref[0, :, j * tk:(j + 1) * tk], *carry)
    _, acc_lo, acc_hi = consume(sx_scr, cmax_x, vxt_ref[0].astype(BF16), *carry)

    ot_lo = acc_lo[:vd] / acc_lo[vd:vd + 1]
    ot_hi = acc_hi[:vd] / acc_hi[vd:vd + 1]
    if mode == "pair":
        o = jnp.concatenate([ot_lo, ot_hi], axis=0).T
    else:
        lam = (jnp.exp(jnp.sum(lq1_ref[...] * lk1_ref[...], axis=-1, keepdims=True))
               - jnp.exp(jnp.sum(lq2_ref[...] * lk2_ref[...], axis=-1, keepdims=True)) + lam_init)
        o = (ot_lo - lam * ot_hi).T
        o = o * lax.rsqrt(jnp.mean(o * o, axis=-1, keepdims=True) + EPS) * bg_ref[...] * (1.0 - lam_init)
    y_ref[0] = (o * gate_ref[0]).astype(y_ref.dtype)


def _keymajor_attention(zq, zg, vt, kx, vxt, *, q_blk, k_blk, vt_blk, kv_per_col, gate_blk, mode, tq, tk,
                        diff_params=None, lam_init=0.0):
    b, n, _ = zq.shape
    per = (lambda c: c) if kv_per_col else (lambda c: 0)
    if kx.ndim == 4:
        kx_spec = pl.BlockSpec((None, 1) + kx.shape[2:], lambda bi, c, qi: (bi, c, 0, 0))
        vx_spec = pl.BlockSpec((None, 1) + vxt.shape[2:], lambda bi, c, qi: (bi, c, 0, 0))
    else:
        kx_spec = pl.BlockSpec((1,) + kx.shape[1:], lambda bi, c, qi: (bi, 0, 0))
        vx_spec = pl.BlockSpec((1,) + vxt.shape[1:], lambda bi, c, qi: (bi, 0, 0))
    in_specs = [pl.BlockSpec((1, tq, LANES), lambda bi, c, qi: (bi, qi, q_blk + c)),
                pl.BlockSpec((1, n, LANES), lambda bi, c, qi: (bi, 0, k_blk + per(c))),
                pl.BlockSpec((1, LANES, n), lambda bi, c, qi: (vt_blk + per(c), 0, bi)),
                kx_spec, vx_spec]
    args = [zq, zq, vt, kx, vxt]
    if mode == "diff":
        in_specs += [pl.BlockSpec((1, HEAD), lambda bi, c, qi: (0, 0))] * 4
        in_specs.append(pl.BlockSpec((1, LANES), lambda bi, c, qi: (0, 0)))
        args += list(diff_params)
    in_specs.append(pl.BlockSpec((1, tq, LANES), lambda bi, c, qi: (bi, qi, gate_blk + c)))
    args.append(zg)
    return pl.pallas_call(
        functools.partial(_keymajor_attn_kernel, mode=mode, tk=tk, lam_init=lam_init),
        grid=(b, N_QCOL, n // tq),
        in_specs=in_specs,
        out_specs=pl.BlockSpec((1, tq, LANES), lambda bi, c, qi: (bi, qi, c)),
        out_shape=jax.ShapeDtypeStruct((b, n, N_QCOL * LANES), BF16),
        scratch_shapes=[pltpu.VMEM((SCORE_SLOTS, tk, 2 * tq), F32), pltpu.VMEM((kx.shape[-2], 2 * tq), F32)],
        compiler_params=pltpu.CompilerParams(vmem_limit_bytes=VMEM_LIMIT),
        name="attention_keymajor_" + mode,
    )(*args)


def _window_attn_kernel(q_ref, k_ref, vt_ref, kx_ref, vxt_ref, sink_ref, gate_ref, y_ref, s_scr):
    n = k_ref.shape[1]
    span = 3 * CHUNK
    sk = jnp.concatenate([jnp.full((1, CHUNK), sink_ref[h] * LOG2E, F32) for h in range(GQA_HEADS)], axis=1)
    kx = kx_ref[0].astype(BF16)
    vxt = vxt_ref[0].astype(BF16)
    half = N_QCOL * CHUNK

    def with_ones(vt):
        return jnp.concatenate([vt, jnp.ones((ONES_ROWS, vt.shape[1]), BF16)], axis=0)

    starts, maxes = [], []
    for sub in range(WINDOW_QBLOCKS):
        blk = pl.program_id(1) * WINDOW_QBLOCKS + sub
        qts = [_pair_cols(q_ref[0, sub * CHUNK:(sub + 1) * CHUNK, c * LANES:(c + 1) * LANES])
               for c in range(N_QCOL)]
        qt = jnp.concatenate([t[:, :CHUNK] for t in qts] + [t[:, CHUNK:] for t in qts], axis=1)
        w0 = pl.multiple_of(jnp.clip((blk - 1) * CHUNK, 0, n - span), CHUNK)
        s_w = jnp.dot(k_ref[0, pl.ds(w0, span), :], qt, preferred_element_type=F32)
        s_x = jnp.dot(kx, qt, preferred_element_type=F32)
        dist = (w0 - blk * CHUNK + lax.broadcasted_iota(jnp.int32, (span, CHUNK), 0)
                - lax.broadcasted_iota(jnp.int32, (span, CHUNK), 1))
        bias = jnp.where((dist >= -WINDOW) & (dist <= WINDOW), 0.0, NEG)
        s_w = s_w + jnp.concatenate([bias] * GQA_HEADS, axis=1)
        s_scr[sub, :span] = s_w
        s_scr[sub, span:] = s_x
        starts.append(w0)
        maxes.append(jnp.maximum(jnp.maximum(jnp.max(s_w, axis=0, keepdims=True),
                                             jnp.max(s_x, axis=0, keepdims=True)), sk))

    for sub in range(WINDOW_QBLOCKS):
        w0, m = starts[sub], maxes[sub]
        p = jnp.exp2(s_scr[sub] - m).astype(BF16)
        vt = jnp.concatenate([vt_ref[0, :, pl.ds(w0, span)], vxt], axis=1)
        acc_lo = jnp.dot(with_ones(vt[:HEAD]), p[:, :half], preferred_element_type=F32)
        acc_hi = jnp.dot(with_ones(vt[HEAD:]), p[:, half:], preferred_element_type=F32)
        sink_p = jnp.exp2(sk - m)
        ot_lo = acc_lo[:HEAD] / (acc_lo[HEAD:HEAD + 1] + sink_p[:, :half])
        ot_hi = acc_hi[:HEAD] / (acc_hi[HEAD:HEAD + 1] + sink_p[:, half:])
        o = jnp.concatenate(
            [jnp.concatenate([ot_lo[:, c * CHUNK:(c + 1) * CHUNK], ot_hi[:, c * CHUNK:(c + 1) * CHUNK]], axis=0).T
             for c in range(N_QCOL)], axis=1)
        rows = slice(sub * CHUNK, (sub + 1) * CHUNK)
        y_ref[0, rows, :] = (o * gate_ref[0, rows, :]).astype(y_ref.dtype)


def _window_attention(zq, zg, vt, kx, vxt, sink):
    b, n, _ = zq.shape
    w4 = N_QCOL * LANES
    tq = WINDOW_QBLOCKS * CHUNK
    return pl.pallas_call(
        _window_attn_kernel,
        grid=(b, n // tq),
        in_specs=[pl.BlockSpec((1, tq, w4), lambda bi, i: (bi, i, 0)),
                  pl.BlockSpec((1, n, LANES), lambda bi, i: (bi, 0, PAIR_K_BLK)),
                  pl.BlockSpec((1, LANES, n), lambda bi, i: (0, 0, bi)),
                  pl.BlockSpec((1,) + kx.shape[1:], lambda bi, i: (bi, 0, 0)),
                  pl.BlockSpec((1,) + vxt.shape[1:], lambda bi, i: (bi, 0, 0)),
                  pl.BlockSpec(memory_space=pltpu.SMEM),
                  pl.BlockSpec((1, tq, w4), lambda bi, i: (bi, i, 0))],
        out_specs=pl.BlockSpec((1, tq, w4), lambda bi, i: (bi, i, 0)),
        out_shape=jax.ShapeDtypeStruct((b, n, w4), BF16),
        scratch_shapes=[pltpu.VMEM((WINDOW_QBLOCKS, 3 * CHUNK + kx.shape[1], GQA_HEADS * CHUNK), F32)],
        compiler_params=pltpu.CompilerParams(vmem_limit_bytes=VMEM_LIMIT),
        name="attention_window",
    )(zq, zq, vt, kx, vxt, sink, zg)


def _retention_kernel(*refs, has_state_in, emit_state):
    it = iter(refs)
    q_ref, k_ref, v_ref, decf_ref, decb_ref, cg_ref, gate_ref = (next(it) for _ in range(7))
    sf_in = sb_in = sf_out = sb_out = None
    if has_state_in:
        sf_in, sb_in = next(it), next(it)
    y_ref = next(it)
    if emit_state:
        sf_out, sb_out = next(it), next(it)
    o_scr, u_scr, dm_scr = next(it), next(it), next(it)

    nc = q_ref.shape[1] // CHUNK
    ri = lax.broadcasted_iota(jnp.int32, (CHUNK, CHUNK), 0).astype(F32)
    ci = lax.broadcasted_iota(jnp.int32, (CHUNK, CHUNK), 1).astype(F32)
    tok_col = lax.broadcasted_iota(jnp.int32, (CHUNK, 1), 0).astype(F32)
    tok_row = lax.broadcasted_iota(jnp.int32, (1, CHUNK), 1).astype(F32)

    lg_f = -jnp.exp(decf_ref[0][:, :1])
    lg_b = -jnp.exp(decb_ref[0][:, :1])
    cross_f, cross_b = jnp.exp(lg_f * (tok_col + 1.0)), jnp.exp(lg_b * (CHUNK - 1.0 - tok_col))
    kdec_f, kdec_b = jnp.exp(lg_f * (CHUNK - 1.0 - tok_row)), jnp.exp(lg_b * tok_row)
    cdec_f, cdec_b = jnp.exp(lg_f * CHUNK), jnp.exp(lg_b * CHUNK)
    rel_f, rel_b = ri - ci, ci - ri - 1.0
    dm_scr[...] = jnp.where(rel_f >= 0.0, jnp.exp(lg_f * jnp.maximum(rel_f, 0.0)),
                            jnp.exp(lg_b * jnp.maximum(rel_b, 0.0)))

    def chunk_rows(c):
        return pl.ds(pl.multiple_of(c * CHUNK, CHUNK), CHUNK)

    def local_body(c, _):
        rows = chunk_rows(c)
        qh = q_ref[0, rows, :].astype(BF16)
        kf = k_ref[0, rows, :].astype(F32)
        vh = v_ref[0, rows, :].astype(BF16)
        kt = kf.T
        att = jnp.dot(qh, kt.astype(BF16), preferred_element_type=F32) * dm_scr[...]
        lhs = jnp.concatenate([att.astype(BF16), (kt * kdec_f).astype(BF16), (kt * kdec_b).astype(BF16)], axis=0)
        r = jnp.dot(lhs, vh, preferred_element_type=F32)
        o_scr[rows, :] = r[:CHUNK]
        u_scr[c, 0] = r[CHUNK:2 * CHUNK]
        u_scr[c, 1] = r[2 * CHUNK:]
        return 0
    lax.fori_loop(0, nc, local_body, 0, unroll=min(nc, RET_UNROLL))

    def scan_body(t, states):
        s_f, s_b = states
        inc_f, inc_b = u_scr[t, 0], u_scr[nc - 1 - t, 1]
        u_scr[t, 0] = s_f
        u_scr[nc - 1 - t, 1] = s_b
        return cdec_f * s_f + inc_f, cdec_b * s_b + inc_b
    zero = jnp.zeros((CHUNK, CHUNK), F32)
    s_f, s_b = lax.fori_loop(0, nc, scan_body, (sf_in[0, 0], sb_in[0, 0]) if has_state_in else (zero, zero),
                             unroll=2)
    if emit_state:
        sf_out[0, 0] = s_f.astype(sf_out.dtype)
        sb_out[0, 0] = s_b.astype(sb_out.dtype)

    def finish_body(c, _):
        rows = chunk_rows(c)
        qh = q_ref[0, rows, :].astype(BF16)
        states = jnp.concatenate([u_scr[c, 0], u_scr[c, 1]], axis=1).astype(BF16)
        r = jnp.dot(qh, states, preferred_element_type=F32)
        o = o_scr[rows, :] + cross_f * r[:, :LANES] + cross_b * r[:, LANES:]
        mu = jnp.mean(o, axis=-1, keepdims=True)
        d = o - mu
        var = jnp.mean(d * d, axis=-1, keepdims=True)
        y = d * lax.rsqrt(var + EPS) * cg_ref[0]
        y_ref[0, rows, :] = (y * gate_ref[0, rows, :]).astype(y_ref.dtype)
        return 0
    lax.fori_loop(0, nc, finish_body, 0, unroll=min(nc, RET_UNROLL))


def _retention(zq, zg, dec_f, dec_b, c_norm_g, states, emit_state):
    b, n, _ = zq.shape
    seq_spec = lambda blk: pl.BlockSpec((1, n, LANES), lambda bi, h: (bi, 0, blk + h))
    st_spec = pl.BlockSpec((1, 1, CHUNK, CHUNK), lambda bi, h: (bi, h, 0, 0))
    head_spec = pl.BlockSpec((1, 1, LANES), lambda bi, h: (h, 0, 0))
    bcast = lambda p: jnp.broadcast_to(p.astype(F32)[:, None, None], (C_HEADS, 1, LANES))
    in_specs = [seq_spec(0), seq_spec(C_HEADS), seq_spec(2 * C_HEADS), head_spec, head_spec, head_spec,
                seq_spec(0)]
    args = [zq, zq, zq, bcast(dec_f), bcast(dec_b), c_norm_g.astype(F32)[:, None, :], zg]
    if states is not None:
        in_specs += [st_spec, st_spec]
        args += list(states)
    out_specs = [seq_spec(0)]
    out_shape = [jax.ShapeDtypeStruct((b, n, C_HEADS * LANES), BF16)]
    if emit_state:
        out_specs += [st_spec, st_spec]
        out_shape += [jax.ShapeDtypeStruct((b, C_HEADS, CHUNK, CHUNK), F32)] * 2
    return pl.pallas_call(
        functools.partial(_retention_kernel, has_state_in=states is not None, emit_state=emit_state),
        grid=(b, C_HEADS),
        in_specs=in_specs,
        out_specs=out_specs,
        out_shape=out_shape,
        scratch_shapes=[pltpu.VMEM((n, LANES), F32),
                        pltpu.VMEM((n // CHUNK, 2, CHUNK, CHUNK), F32),
                        pltpu.VMEM((CHUNK, CHUNK), F32)],
        compiler_params=pltpu.CompilerParams(vmem_limit_bytes=VMEM_LIMIT),
        name="retention",
    )(*args)


def _out_kernel(y1_ref, y2_ref, x_ref, mod_ref, w_ref, fg_ref, o_ref, *, final_norm):
    half = y1_ref.shape[1]
    y = (jnp.dot(y1_ref[...], w_ref[:half, :], preferred_element_type=F32)
         + jnp.dot(y2_ref[...], w_ref[half:, :], preferred_element_type=F32))
    x = x_ref[...] + mod_ref[0, 2:3, :] * y
    if final_norm:
        x = x * lax.rsqrt(jnp.mean(x * x, axis=-1, keepdims=True) + EPS) * fg_ref[...]
    o_ref[...] = x


def _out_project(y1, y2, x2d, mod, mod_row_of_tile, w_bf16, final_g, *, tm, final_norm):
    t, d = x2d.shape
    half = y1.shape[1]
    return pl.pallas_call(
        functools.partial(_out_kernel, final_norm=final_norm),
        grid=(t // tm,),
        in_specs=[pl.BlockSpec((tm, half), lambda i: (i, 0)),
                  pl.BlockSpec((tm, half), lambda i: (i, 0)),
                  pl.BlockSpec((tm, d), lambda i: (i, 0)),
                  pl.BlockSpec((1, 3, d), lambda i: (mod_row_of_tile(i), 0, 0)),
                  pl.BlockSpec(w_bf16.shape, lambda i: (0, 0)),
                  pl.BlockSpec((1, d), lambda i: (0, 0))],
        out_specs=pl.BlockSpec((tm, d), lambda i: (i, 0)),
        out_shape=jax.ShapeDtypeStruct((t, d), F32),
        compiler_params=pltpu.CompilerParams(vmem_limit_bytes=VMEM_LIMIT),
        name="out_project",
    )(y1, y2, x2d, mod, w_bf16, final_g.reshape(1, d))


def _pair_perm():
    return np.concatenate([np.arange(h * HEAD, (h + 1) * HEAD) for h in PAIR_ORDER])


def _reorder(w, idx, axis):
    idx = np.asarray(idx)
    runs = np.split(idx, np.flatnonzero(np.diff(idx) != 1) + 1)
    return jnp.concatenate([lax.slice_in_dim(w, int(r[0]), int(r[-1]) + 1, axis=axis) for r in runs], axis=axis)


def _rope_tables(n):
    rows = n // GRID_W
    row = jnp.repeat(jnp.arange(rows, dtype=F32), GRID_W)
    col = jnp.tile(jnp.arange(GRID_W, dtype=F32), rows)
    nf = HEAD // 4
    inv = ROPE_THETA ** (-jnp.arange(nf, dtype=F32) / nf)
    ang = jnp.concatenate([row[:, None] * inv, col[:, None] * inv], axis=-1)
    cos, sin = jnp.cos(ang), jnp.sin(ang)
    return jnp.tile(cos, (1, 4)), jnp.tile(jnp.concatenate([-sin, sin], axis=-1), (1, 2))


def _pair_kv(cache):
    b, g, p, d = cache.shape
    return cache.transpose(0, 2, 1, 3).reshape(b, p, g * d)


def _unpair_kv(z, blk, n_blk, heads):
    b, p, _ = z.shape
    t = z[:, :, blk * LANES:(blk + n_blk) * LANES]
    return t.reshape(b, p, heads, t.shape[-1] // heads).transpose(0, 2, 1, 3)


def kernel(x_prompt, x_sample, cache_a_k, cache_a_v, cache_b_k, cache_b_v, state_c_fwd, state_c_bwd, cache_d_k, cache_d_v, c, c_ctx, norm_g, mod_w, mod_b, ab_w_in, ab_w_out, a_sink, b_lq1, b_lk1, b_lq2, b_lk2, b_norm_g, cd_w_in, cd_w_out, c_decay_f, c_decay_b, c_norm_g, d_q_norm_g, d_k_norm_g, final_g):
    depth = norm_g.shape[0]
    bp, sp, d = x_prompt.shape
    bs, ss, _ = x_sample.shape
    dt = x_prompt.dtype

    ctx_row = bs
    pad = (-(bs + 1)) % 8
    cond = jnp.concatenate([c, c_ctx[None, :], jnp.zeros((pad, d), c.dtype)], axis=0)
    mod = _modulation(cond, mod_w, mod_b).reshape(depth, cond.shape[0], 3, d)

    perm = _pair_perm()
    qkv_w = N_QKV_BLK * LANES
    rope_tabs = _rope_tables(ss)
    bd = jnp.asarray(np.kron(np.eye(LANES // HEAD), np.ones((HEAD, HEAD))), BF16)

    tm_s = 512
    tm_p = 512
    tiles_per_seq = ss // tm_s
    row_s = lambda i: i // tiles_per_seq
    row_p = lambda i: ctx_row

    xp = x_prompt.reshape(bp * sp, d)
    xs = x_sample.reshape(bs * ss, d)
    outs = {k: [] for k in ("a_k", "a_v", "b_k", "b_v", "c_f", "c_b", "d_k", "d_v")}

    pend_p = pend_s = None

    def project(x2d, pending, row_fn, tm, **kw):
        res = list(_project(x2d, mod[layer], row_fn, norm_g[layer], w_in, tm=tm, prev=pending, **kw))
        if pending is not None:
            x2d = res.pop()
        return x2d, res

    for layer in range(depth):
        i = layer // 2
        if layer % 2 == 0:
            lam_init = 0.8 - 0.6 * math.exp(-0.3 * layer)
            w4 = N_QCOL * LANES
            ar = np.arange
            cols = np.concatenate([perm, ar(w4 + 2 * LANES, 4 * w4 + 2 * LANES), ar(w4, w4 + 2 * LANES),
                                   qkv_w + perm, ar(qkv_w + w4, qkv_w + 2 * w4)])
            w_in = _reorder(ab_w_in[i], cols, 1).astype(BF16)
            rows = np.arange(ab_w_out.shape[1])
            rows[0:N_QCOL * LANES] = perm
            w_out = _reorder(ab_w_out[i], rows, 0).astype(BF16)
            diff_params = (b_lq1[i][None], b_lk1[i][None], b_lq2[i][None], b_lk2[i][None], b_norm_g[i][None])

            xp, (zq, zg) = project(xp, pend_p, row_p, tm_p, kind="ab", rope_tabs=None, norm_params=None,
                                   out_dtype=F32)
            zq3, zg3 = zq.reshape(bp, sp, -1), zg.reshape(bp, sp, -1)
            ya = _ctx_attention(zq3, zg3, q_blk=0, k_blk=PAIR_K_BLK, v_blk=PAIR_V_BLK, kv_per_col=False,
                                gate_blk=0, mode="pair", sink=a_sink[i])
            yb = _ctx_attention(zq3, zg3, q_blk=4, k_blk=8, v_blk=12, kv_per_col=True, gate_blk=4,
                                mode="diff", diff_params=diff_params, lam_init=lam_init)
            pend_p = (ya.reshape(bp * sp, -1), yb.reshape(bp * sp, -1), mod[layer], w_out)
            outs["a_k"].append(_unpair_kv(zq3, PAIR_K_BLK, 1, 2))
            outs["a_v"].append(_unpair_kv(zq3, PAIR_V_BLK, 1, 2))
            outs["b_k"].append(_unpair_kv(zq3, 8, 4, 4))
            outs["b_v"].append(_unpair_kv(zq3, 12, 4, 4))

            xs, (zq, zg, vt) = project(xs, pend_s, row_s, tm_s, kind="ab", rope_tabs=rope_tabs, norm_params=None,
                                       out_dtype=BF16, vt_blocks=(PAIR_V_BLK, 12, 13, 14, 15))
            zq3, zg3 = zq.reshape(bs, ss, -1), zg.reshape(bs, ss, -1)
            ya = _window_attention(zq3, zg3, vt, _pair_kv(cache_a_k[:, i]),
                                   _pair_kv(cache_a_v[:, i]).transpose(0, 2, 1), a_sink[i])
            yb = _keymajor_attention(zq3, zg3, vt, cache_b_k[:, i], cache_b_v[:, i].transpose(0, 1, 3, 2),
                                     q_blk=4, k_blk=8, vt_blk=1, kv_per_col=True, gate_blk=4, mode="diff",
                                     tq=TQ_DENSE, tk=TK_DENSE, diff_params=diff_params, lam_init=lam_init)
            pend_s = (ya.reshape(bs * ss, -1), yb.reshape(bs * ss, -1), mod[layer], w_out)
        else:
            cols = np.arange(cd_w_in.shape[2])
            cols[12 * LANES:16 * LANES] = 12 * LANES + perm
            cols[qkv_w + N_QCOL * LANES:qkv_w + 2 * N_QCOL * LANES] = qkv_w + N_QCOL * LANES + perm
            w_in = _reorder(cd_w_in[i], cols, 1).astype(BF16)
            rows = np.arange(cd_w_out.shape[1])
            rows[N_QCOL * LANES:] = N_QCOL * LANES + perm
            w_out = _reorder(cd_w_out[i], rows, 0).astype(BF16)
            norm_params = (jnp.tile(d_q_norm_g[i], 2)[None], jnp.tile(d_k_norm_g[i], 2)[None], bd)

            xp, (zq, zg) = project(xp, pend_p, row_p, tm_p, kind="cd", rope_tabs=None, norm_params=norm_params,
                                   out_dtype=F32)
            zq3, zg3 = zq.reshape(bp, sp, -1), zg.reshape(bp, sp, -1)
            yc, s_f, s_b = _retention(zq3, zg3, c_decay_f[i], c_decay_b[i], c_norm_g[i], None, True)
            yd = _ctx_attention(zq3, zg3, q_blk=12, k_blk=PAIR_K_BLK, v_blk=PAIR_V_BLK, kv_per_col=False,
                                gate_blk=4, mode="pair")
            pend_p = (yc.reshape(bp * sp, -1), yd.reshape(bp * sp, -1), mod[layer], w_out)
            outs["c_f"].append(s_f.astype(dt))
            outs["c_b"].append(s_b.astype(dt))
            outs["d_k"].append(_unpair_kv(zq3, 16, 1, 2))
            outs["d_v"].append(_unpair_kv(zq3, 17, 1, 2))

            xs, (zq, zg, vt) = project(xs, pend_s, row_s, tm_s, kind="cd", rope_tabs=rope_tabs,
                                       norm_params=norm_params, out_dtype=BF16, vt_blocks=(PAIR_V_BLK,))
            zq3, zg3 = zq.reshape(bs, ss, -1), zg.reshape(bs, ss, -1)
            yc = _retention(zq3, zg3, c_decay_f[i], c_decay_b[i], c_norm_g[i],
                            (state_c_fwd[:, i], state_c_bwd[:, i]), False)[0]
            yd = _keymajor_attention(zq3, zg3, vt, _pair_kv(cache_d_k[:, i]),
                                     _pair_kv(cache_d_v[:, i]).transpose(0, 2, 1),
                                     q_blk=12, k_blk=PAIR_K_BLK, vt_blk=0, kv_per_col=False, gate_blk=4,
                                     mode="pair",
                                     tq=TQ_DENSE, tk=TK_DENSE)
            pend_s = (yc.reshape(bs * ss, -1), yd.reshape(bs * ss, -1), mod[layer], w_out)

    xp = _out_project(pend_p[0], pend_p[1], xp, pend_p[2], row_p, pend_p[3], final_g, tm=tm_p, final_norm=True)
    xs = _out_project(pend_s[0], pend_s[1], xs, pend_s[2], row_s, pend_s[3], final_g, tm=tm_s, final_norm=True)

    stack = lambda k: jnp.stack(outs[k], axis=1)
    return (xp.reshape(bp, sp, d), xs.reshape(bs, ss, d), stack("a_k"), stack("a_v"), stack("b_k"), stack("b_v"),
            stack("c_f"), stack("c_b"), stack("d_k"), stack("d_v"))
```

```python
import functools
import math

import numpy as np
import jax
import jax.numpy as jnp
from jax import lax
from jax.experimental import pallas as pl
from jax.experimental.pallas import tpu as pltpu

F32 = jnp.float32
BF16 = jnp.bfloat16

LANES = 128
HEAD = 64
GRID_W = 64
CHUNK = 128
WINDOW = 128
ROPE_THETA = 10000.0
EPS = 1e-6
NEG = -1e30
VMEM_LIMIT = 56 * 1024 * 1024
TQ_DENSE = 2048
TK_DENSE = 512
SCORE_SLOTS = 4
WINDOW_QBLOCKS = 4
RET_HEADS_CTX = 4
RET_HEADS_LATENT = 1
RET_UNROLL = 8
ONES_ROWS = 16
LOG2E = 1.4426950408889634

GQA_HEADS = 8
PAIR_ORDER = (0, 4, 1, 5, 2, 6, 3, 7)
N_QCOL = 4
C_HEADS = 4
C_DK = 128

N_QKV_BLK = 18
N_GATE_BLK = 8
PAIR_K_BLK, PAIR_V_BLK = 16, 17
AB_OPS = ("qrope",) * 8 + ("rope",) * 4 + ("plain",) * 4 + ("rope", "plain")
CD_OPS = ("plain",) * 4 + ("kscale",) * 4 + ("plain",) * 4 + ("qnorm",) * 4 + ("knorm", "plain")
PROJ_GROUPS = ((0, 4), (4, 8), (8, 12), (12, 16), (16, 18))


def _silu(x):
    return x / (1.0 + jnp.exp(-x))


def _lane_lo(shape):
    return lax.broadcasted_iota(jnp.int32, shape, len(shape) - 1) % LANES < HEAD


def _mod_kernel(cond_ref, w_ref, b_ref, o_ref):
    s = _silu(cond_ref[...])
    o_ref[0] = jnp.dot(s.astype(BF16), w_ref[0].astype(BF16), preferred_element_type=F32) + b_ref[0]


def _modulation(cond, mod_w, mod_b):
    depth, d, d3 = mod_w.shape
    rows = cond.shape[0]
    return pl.pallas_call(
        _mod_kernel,
        grid=(depth, d3 // d),
        in_specs=[pl.BlockSpec((rows, d), lambda l, j: (0, 0)),
                  pl.BlockSpec((1, d, d), lambda l, j: (l, 0, j)),
                  pl.BlockSpec((1, 1, d), lambda l, j: (l, 0, j))],
        out_specs=pl.BlockSpec((1, rows, d), lambda l, j: (l, 0, j)),
        out_shape=jax.ShapeDtypeStruct((depth, rows, d3), F32),
        compiler_params=pltpu.CompilerParams(vmem_limit_bytes=VMEM_LIMIT),
        name="modulation",
    )(cond, mod_w, mod_b.reshape(depth, 1, d3))


def _rot_half(x, first_half):
    return jnp.where(first_half, pltpu.roll(x, LANES - HEAD // 2, 1), pltpu.roll(x, HEAD // 2, 1))


def _head_sumsq(x, bd):
    sq = x * x
    hi = sq.astype(BF16)
    lo = (sq - hi.astype(F32)).astype(BF16)
    return jnp.dot(hi, bd, preferred_element_type=F32) + jnp.dot(lo, bd, preferred_element_type=F32)


def _proj_kernel(*refs, ops, groups, use_rope, has_norm, vt_blocks, has_prev):
    it = iter(refs)
    x_ref, mod_ref, g_ref, w_ref = next(it), next(it), next(it), next(it)
    cos_ref = sin_ref = qg_ref = kg_ref = bd_ref = vt_ref = None
    if has_prev:
        y1_ref, y2_ref, pmod_ref, pw_ref = next(it), next(it), next(it), next(it)
    if use_rope:
        cos_ref, sin_ref = next(it), next(it)
    if has_norm:
        qg_ref, kg_ref, bd_ref = next(it), next(it), next(it)
    zq_ref, zg_ref = next(it), next(it)
    if vt_blocks:
        vt_ref = next(it)
    if has_prev:
        xnew_ref = next(it)

    x = x_ref[...]
    if has_prev:
        half = y1_ref.shape[1]
        y = (jnp.dot(y1_ref[...], pw_ref[:half, :], preferred_element_type=F32)
             + jnp.dot(y2_ref[...], pw_ref[half:, :], preferred_element_type=F32))
        x = x + pmod_ref[0, 2:3, :] * y
        xnew_ref[...] = x
    h = x * lax.rsqrt(jnp.mean(x * x, axis=-1, keepdims=True) + EPS) * g_ref[...]
    h = h * (1.0 + mod_ref[0, 1:2, :]) + mod_ref[0, 0:1, :]
    hb = h.astype(BF16)

    tm = x.shape[0]
    if use_rope:
        cos, sin = cos_ref[...], sin_ref[...]
        first_half = lax.broadcasted_iota(jnp.int32, (tm, LANES), 1) % HEAD < HEAD // 2

    def rope(z):
        if not use_rope:
            return z
        return z * cos + _rot_half(z, first_half) * sin

    def head_norm(z, gain_ref):
        ss = _head_sumsq(z, bd_ref[...])
        return z * lax.rsqrt(ss * (1.0 / HEAD) + EPS) * gain_ref[...]

    q_scale = HEAD ** -0.5 * LOG2E
    for b0, b1 in groups:
        z = jnp.dot(hb, w_ref[:, b0 * LANES:b1 * LANES], preferred_element_type=F32)
        for j in range(b0, b1):
            zz = z[:, (j - b0) * LANES:(j - b0 + 1) * LANES]
            op = ops[j]
            if op == "qrope":
                zz = rope(zz * q_scale)
            elif op == "rope":
                zz = rope(zz)
            elif op == "kscale":
                zz = zz * (C_DK ** -0.5)
            elif op == "qnorm":
                zz = rope(head_norm(zz, qg_ref)) * q_scale
            elif op == "knorm":
                zz = rope(head_norm(zz, kg_ref))
            zq_ref[:, j * LANES:(j + 1) * LANES] = zz.astype(zq_ref.dtype)
            if j in vt_blocks:
                vt_ref[vt_blocks.index(j)] = zz.T.astype(vt_ref.dtype)
    half = N_GATE_BLK // 2
    for g0 in (0, half):
        c0 = (N_QKV_BLK + g0) * LANES
        z = jnp.dot(hb, w_ref[:, c0:c0 + half * LANES], preferred_element_type=F32)
        zg_ref[:, g0 * LANES:(g0 + half) * LANES] = _silu(z)


def _project(x2d, mod, mod_row_of_tile, norm_g, w_bf16, *, tm, kind, rope_tabs, norm_params, out_dtype,
             vt_blocks=(), prev=None):
    t, d = x2d.shape
    out_specs = [pl.BlockSpec((tm, N_QKV_BLK * LANES), lambda i: (i, 0)),
                 pl.BlockSpec((tm, N_GATE_BLK * LANES), lambda i: (i, 0))]
    out_shape = [jax.ShapeDtypeStruct((t, N_QKV_BLK * LANES), out_dtype),
                 jax.ShapeDtypeStruct((t, N_GATE_BLK * LANES), F32)]
    if vt_blocks:
        out_specs.append(pl.BlockSpec((len(vt_blocks), LANES, tm), lambda i: (0, 0, i)))
        out_shape.append(jax.ShapeDtypeStruct((len(vt_blocks), LANES, t), out_dtype))
    if prev is not None:
        out_specs.append(pl.BlockSpec((tm, d), lambda i: (i, 0)))
        out_shape.append(jax.ShapeDtypeStruct((t, d), F32))
    use_rope = rope_tabs is not None
    has_norm = kind == "cd"
    ops, groups = (AB_OPS if kind == "ab" else CD_OPS), PROJ_GROUPS
    in_specs = [pl.BlockSpec((tm, d), lambda i: (i, 0)),
                pl.BlockSpec((1, 3, d), lambda i: (mod_row_of_tile(i), 0, 0)),
                pl.BlockSpec((1, d), lambda i: (0, 0)),
                pl.BlockSpec(w_bf16.shape, lambda i: (0, 0))]
    args = [x2d, mod, norm_g.reshape(1, d), w_bf16]
    if prev is not None:
        y1, y2, prev_mod, prev_w = prev
        in_specs += [pl.BlockSpec((tm, y1.shape[1]), lambda i: (i, 0)),
                     pl.BlockSpec((tm, y2.shape[1]), lambda i: (i, 0)),
                     pl.BlockSpec((1, 3, d), lambda i: (mod_row_of_tile(i), 0, 0)),
                     pl.BlockSpec(prev_w.shape, lambda i: (0, 0))]
        args += [y1, y2, prev_mod, prev_w]
    if use_rope:
        n_seq_tiles = rope_tabs[0].shape[0] // tm
        in_specs += [pl.BlockSpec((tm, LANES), lambda i: (i % n_seq_tiles, 0))] * 2
        args += list(rope_tabs)
    if has_norm:
        in_specs += [pl.BlockSpec((1, LANES), lambda i: (0, 0))] * 2 + [pl.BlockSpec((LANES, LANES), lambda i: (0, 0))]
        args += list(norm_params)
    return pl.pallas_call(
        functools.partial(_proj_kernel, ops=ops, groups=groups, use_rope=use_rope, has_norm=has_norm,
                          vt_blocks=tuple(vt_blocks), has_prev=prev is not None),
        grid=(t // tm,),
        in_specs=in_specs,
        out_specs=out_specs,
        out_shape=out_shape,
        compiler_params=pltpu.CompilerParams(vmem_limit_bytes=VMEM_LIMIT),
        name="project_" + kind,
    )(*args)


def _pair_rows(q):
    lo = jnp.where(_lane_lo((1, LANES)), 1.0, 0.0).astype(BF16)
    qb = q.astype(BF16)
    return jnp.concatenate([qb * lo, qb * (1.0 - lo).astype(BF16)], axis=0)


def _pair_cols(q):
    qt = q.astype(F32).T
    top = lax.broadcasted_iota(jnp.int32, qt.shape, 0) < HEAD
    return jnp.concatenate([jnp.where(top, qt, 0.0), jnp.where(top, 0.0, qt)], axis=1).astype(BF16)


def _ctx_attn_kernel(*refs, mode, has_sink, kv_per_col, lam_init):
    it = iter(refs)
    q_ref, k_ref, v_ref = next(it), next(it), next(it)
    sink_ref = next(it) if has_sink else None
    if mode == "diff":
        lq1_ref, lk1_ref, lq2_ref, lk2_ref, bg_ref = (next(it) for _ in range(5))
    gate_ref, y_ref = next(it), next(it)

    tq = q_ref.shape[1]
    lo = _lane_lo((tq, LANES))
    for c in range(N_QCOL):
        cols = slice(c * LANES, (c + 1) * LANES)
        kv_cols = cols if kv_per_col else slice(0, LANES)
        qrows = _pair_rows(q_ref[0, :, cols])
        s = lax.dot_general(qrows, k_ref[0, :, kv_cols].astype(BF16), (((1,), (1,)), ((), ())),
                            preferred_element_type=F32)
        v = v_ref[0, :, kv_cols].astype(BF16)
        halves = []
        for half, head in ((s[:tq], c), (s[tq:], c + N_QCOL)):
            m = jnp.max(half, axis=-1, keepdims=True)
            if has_sink:
                sk = sink_ref[head] * LOG2E
                m = jnp.maximum(m, sk)
            p = jnp.exp2(half - m)
            l = jnp.sum(p, axis=-1, keepdims=True)
            if has_sink:
                l = l + jnp.exp2(sk - m)
            halves.append(jnp.dot(p.astype(BF16), v, preferred_element_type=F32) / l)
        o2 = jnp.concatenate(halves, axis=0)
        if mode == "pair":
            o = jnp.where(lo, o2[:tq], o2[tq:])
        else:
            lam = (jnp.exp(jnp.sum(lq1_ref[...] * lk1_ref[...], axis=-1, keepdims=True))
                   - jnp.exp(jnp.sum(lq2_ref[...] * lk2_ref[...], axis=-1, keepdims=True)) + lam_init)
            o = o2[:tq] - lam * o2[tq:]
            o = o * lax.rsqrt(jnp.mean(o * o, axis=-1, keepdims=True) + EPS) * bg_ref[...] * (1.0 - lam_init)
        y_ref[0, :, cols] = (o * gate_ref[0, :, cols]).astype(y_ref.dtype)


def _ctx_attention(zq, zg, *, q_blk, k_blk, v_blk, kv_per_col, gate_blk, mode,
                   sink=None, diff_params=None, lam_init=0.0):
    b, n, _ = zq.shape
    w4 = N_QCOL * LANES
    kv_w = w4 if kv_per_col else LANES
    assert (q_blk * LANES) % w4 == 0 and (gate_blk * LANES) % w4 == 0
    assert (k_blk * LANES) % kv_w == 0 and (v_blk * LANES) % kv_w == 0
    in_specs = [pl.BlockSpec((1, n, w4), lambda bi: (bi, 0, q_blk * LANES // w4)),
                pl.BlockSpec((1, n, kv_w), lambda bi: (bi, 0, k_blk * LANES // kv_w)),
                pl.BlockSpec((1, n, kv_w), lambda bi: (bi, 0, v_blk * LANES // kv_w))]
    args = [zq, zq, zq]
    if sink is not None:
        in_specs.append(pl.BlockSpec(memory_space=pltpu.SMEM))
        args.append(sink)
    if mode == "diff":
        in_specs += [pl.BlockSpec((1, HEAD), lambda bi: (0, 0))] * 4
        in_specs.append(pl.BlockSpec((1, LANES), lambda bi: (0, 0)))
        args += list(diff_params)
    in_specs.append(pl.BlockSpec((1, n, w4), lambda bi: (bi, 0, gate_blk * LANES // w4)))
    args.append(zg)
    return pl.pallas_call(
        functools.partial(_ctx_attn_kernel, mode=mode, has_sink=sink is not None, kv_per_col=kv_per_col,
                          lam_init=lam_init),
        grid=(b,),
        in_specs=in_specs,
        out_specs=pl.BlockSpec((1, n, w4), lambda bi: (bi, 0, 0)),
        out_shape=jax.ShapeDtypeStruct((b, n, w4), BF16),
        compiler_params=pltpu.CompilerParams(vmem_limit_bytes=VMEM_LIMIT),
        name="ctx_attention_" + mode,
    )(*args)


def _keymajor_attn_kernel(*refs, mode, tk, lam_init):
    it = iter(refs)
    q_ref, k_ref, vt_ref, kx_ref, vxt_ref = (next(it) for _ in range(5))
    if mode == "diff":
        lq1_ref, lk1_ref, lq2_ref, lk2_ref, bg_ref = (next(it) for _ in range(5))
    gate_ref, y_ref, s_scr, sx_scr = next(it), next(it), next(it), next(it)

    tq = q_ref.shape[1]
    qt = _pair_cols(q_ref[0])

    def scores(k, slot):
        s = jnp.dot(k, qt, preferred_element_type=F32)
        slot[...] = s
        return jnp.max(s, axis=0, keepdims=True)

    vd = HEAD if mode == "pair" else LANES

    def with_ones(vt):
        return jnp.concatenate([vt, jnp.ones((ONES_ROWS, vt.shape[1]), BF16)], axis=0)

    def consume(slot, cmax, vt, m, acc_lo, acc_hi):
        m_new = jnp.maximum(m, cmax)
        alpha = jnp.exp2(m - m_new)
        p = jnp.exp2(slot[...] - m_new).astype(BF16)
        if mode == "pair":
            v_lo, v_hi = with_ones(vt[:HEAD]), with_ones(vt[HEAD:])
        else:
            v_lo = v_hi = with_ones(vt)
        acc_lo = alpha[:, :tq] * acc_lo + jnp.dot(v_lo, p[:, :tq], preferred_element_type=F32)
        acc_hi = alpha[:, tq:] * acc_hi + jnp.dot(v_hi, p[:, tq:], preferred_element_type=F32)
        return m_new, acc_lo, acc_hi

    n_chunks = k_ref.shape[1] // tk
    carry = (jnp.full((1, 2 * tq), NEG, F32), jnp.zeros((vd + ONES_ROWS, tq), F32),
             jnp.zeros((vd + ONES_ROWS, tq), F32))
    n_slots = s_scr.shape[0]
    ahead = n_slots - 1
    chunk_scores = lambda j: scores(k_ref[0, j * tk:(j + 1) * tk, :], s_scr.at[j % n_slots])
    cmax = [chunk_scores(j) for j in range(min(ahead, n_chunks))]
    cmax_x = scores(kx_ref[0].astype(BF16), sx_scr)
    for j in range(n_chunks):
        if j + ahead < n_chunks:
            cmax.append(chunk_scores(j + ahead))
        carry = consume(s_scr.at[j % n_slots], cmax[j], vt_ref[0, :, j * tk:(j + 1) * tk], *carry)
    _, acc_lo, acc_hi = consume(sx_scr, cmax_x, vxt_ref[0].astype(BF16), *carry)

    ot_lo = acc_lo[:vd] / acc_lo[vd:vd + 1]
    ot_hi = acc_hi[:vd] / acc_hi[vd:vd + 1]
    if mode == "pair":
        o = jnp.concatenate([ot_lo, ot_hi], axis=0).T
    else:
        lam = (jnp.exp(jnp.sum(lq1_ref[...] * lk1_ref[...], axis=-1, keepdims=True))
               - jnp.exp(jnp.sum(lq2_ref[...] * lk2_ref[...], axis=-1, keepdims=True)) + lam_init)
        o = (ot_lo - lam * ot_hi).T
        o = o * lax.rsqrt(jnp.mean(o * o, axis=-1, keepdims=True) + EPS) * bg_ref[...] * (1.0 - lam_init)
    y_ref[0] = (o * gate_ref[0]).astype(y_ref.dtype)


def _keymajor_attention(zq, zg, vt, kx, vxt, *, q_blk, k_blk, vt_blk, kv_per_col, gate_blk, mode, tq, tk,
                        diff_params=None, lam_init=0.0):
    b, n, _ = zq.shape
    per = (lambda c: c) if kv_per_col else (lambda c: 0)
    if kx.ndim == 4:
        kx_spec = pl.BlockSpec((None, 1) + kx.shape[2:], lambda bi, c, qi: (bi, c, 0, 0))
        vx_spec = pl.BlockSpec((None, 1) + vxt.shape[2:], lambda bi, c, qi: (bi, c, 0, 0))
    else:
        kx_spec = pl.BlockSpec((1,) + kx.shape[1:], lambda bi, c, qi: (bi, 0, 0))
        vx_spec = pl.BlockSpec((1,) + vxt.shape[1:], lambda bi, c, qi: (bi, 0, 0))
    in_specs = [pl.BlockSpec((1, tq, LANES), lambda bi, c, qi: (bi, qi, q_blk + c)),
                pl.BlockSpec((1, n, LANES), lambda bi, c, qi: (bi, 0, k_blk + per(c))),
                pl.BlockSpec((1, LANES, n), lambda bi, c, qi: (vt_blk + per(c), 0, bi)),
                kx_spec, vx_spec]
    args = [zq, zq, vt, kx, vxt]
    if mode == "diff":
        in_specs += [pl.BlockSpec((1, HEAD), lambda bi, c, qi: (0, 0))] * 4
        in_specs.append(pl.BlockSpec((1, LANES), lambda bi, c, qi: (0, 0)))
        args += list(diff_params)
    in_specs.append(pl.BlockSpec((1, tq, LANES), lambda bi, c, qi: (bi, qi, gate_blk + c)))
    args.append(zg)
    return pl.pallas_call(
        functools.partial(_keymajor_attn_kernel, mode=mode, tk=tk, lam_init=lam_init),
        grid=(b, N_QCOL, n // tq),
        in_specs=in_specs,
        out_specs=pl.BlockSpec((1, tq, LANES), lambda bi, c, qi: (bi, qi, c)),
        out_shape=jax.ShapeDtypeStruct((b, n, N_QCOL * LANES), BF16),
        scratch_shapes=[pltpu.VMEM((SCORE_SLOTS, tk, 2 * tq), F32), pltpu.VMEM((kx.shape[-2], 2 * tq), F32)],
        compiler_params=pltpu.CompilerParams(vmem_limit_bytes=VMEM_LIMIT),
        name="attention_keymajor_" + mode,
    )(*args)


def _window_attn_kernel(q_ref, k_ref, vt_ref, kx_ref, vxt_ref, sink_ref, gate_ref, y_ref, s_scr):
    n = k_ref.shape[1]
    span = 3 * CHUNK
    sk = jnp.concatenate([jnp.full((1, CHUNK), sink_ref[h] * LOG2E, F32) for h in range(GQA_HEADS)], axis=1)
    kx = kx_ref[0].astype(BF16)
    vxt = vxt_ref[0].astype(BF16)
    half = N_QCOL * CHUNK

    def with_ones(vt):
        return jnp.concatenate([vt, jnp.ones((ONES_ROWS, vt.shape[1]), BF16)], axis=0)

    starts, maxes = [], []
    for sub in range(WINDOW_QBLOCKS):
        blk = pl.program_id(1) * WINDOW_QBLOCKS + sub
        qts = [_pair_cols(q_ref[0, sub * CHUNK:(sub + 1) * CHUNK, c * LANES:(c + 1) * LANES])
               for c in range(N_QCOL)]
        qt = jnp.concatenate([t[:, :CHUNK] for t in qts] + [t[:, CHUNK:] for t in qts], axis=1)
        w0 = pl.multiple_of(jnp.clip((blk - 1) * CHUNK, 0, n - span), CHUNK)
        s_w = jnp.dot(k_ref[0, pl.ds(w0, span), :], qt, preferred_element_type=F32)
        s_x = jnp.dot(kx, qt, preferred_element_type=F32)
        dist = (w0 - blk * CHUNK + lax.broadcasted_iota(jnp.int32, (span, CHUNK), 0)
                - lax.broadcasted_iota(jnp.int32, (span, CHUNK), 1))
        bias = jnp.where((dist >= -WINDOW) & (dist <= WINDOW), 0.0, NEG)
        s_w = s_w + jnp.concatenate([bias] * GQA_HEADS, axis=1)
        s_scr[sub, :span] = s_w
        s_scr[sub, span:] = s_x
        starts.append(w0)
        maxes.append(jnp.maximum(jnp.maximum(jnp.max(s_w, axis=0, keepdims=True),
                                             jnp.max(s_x, axis=0, keepdims=True)), sk))

    for sub in range(WINDOW_QBLOCKS):
        w0, m = starts[sub], maxes[sub]
        p = jnp.exp2(s_scr[sub] - m).astype(BF16)
        vt = jnp.concatenate([vt_ref[0, :, pl.ds(w0, span)], vxt], axis=1)
        acc_lo = jnp.dot(with_ones(vt[:HEAD]), p[:, :half], preferred_element_type=F32)
        acc_hi = jnp.dot(with_ones(vt[HEAD:]), p[:, half:], preferred_element_type=F32)
        sink_p = jnp.exp2(sk - m)
        ot_lo = acc_lo[:HEAD] / (acc_lo[HEAD:HEAD + 1] + sink_p[:, :half])
        ot_hi = acc_hi[:HEAD] / (acc_hi[HEAD:HEAD + 1] + sink_p[:, half:])
        o = jnp.concatenate(
            [jnp.concatenate([ot_lo[:, c * CHUNK:(c + 1) * CHUNK], ot_hi[:, c * CHUNK:(c + 1) * CHUNK]], axis=0).T
             for c in range(N_QCOL)], axis=1)
        rows = slice(sub * CHUNK, (sub + 1) * CHUNK)
        y_ref[0, rows, :] = (o * gate_ref[0, rows, :]).astype(y_ref.dtype)


def _window_attention(zq, zg, vt, kx, vxt, sink):
    b, n, _ = zq.shape
    w4 = N_QCOL * LANES
    tq = WINDOW_QBLOCKS * CHUNK
    return pl.pallas_call(
        _window_attn_kernel,
        grid=(b, n // tq),
        in_specs=[pl.BlockSpec((1, tq, w4), lambda bi, i: (bi, i, 0)),
                  pl.BlockSpec((1, n, LANES), lambda bi, i: (bi, 0, PAIR_K_BLK)),
                  pl.BlockSpec((1, LANES, n), lambda bi, i: (0, 0, bi)),
                  pl.BlockSpec((1,) + kx.shape[1:], lambda bi, i: (bi, 0, 0)),
                  pl.BlockSpec((1,) + vxt.shape[1:], lambda bi, i: (bi, 0, 0)),
                  pl.BlockSpec(memory_space=pltpu.SMEM),
                  pl.BlockSpec((1, tq, w4), lambda bi, i: (bi, i, 0))],
        out_specs=pl.BlockSpec((1, tq, w4), lambda bi, i: (bi, i, 0)),
        out_shape=jax.ShapeDtypeStruct((b, n, w4), BF16),
        scratch_shapes=[pltpu.VMEM((WINDOW_QBLOCKS, 3 * CHUNK + kx.shape[1], GQA_HEADS * CHUNK), F32)],
        compiler_params=pltpu.CompilerParams(vmem_limit_bytes=VMEM_LIMIT),
        name="attention_window",
    )(zq, zq, vt, kx, vxt, sink, zg)


def _retention_kernel(*refs, has_state_in, emit_state):
    it = iter(refs)
    q_ref, k_ref, v_ref, decf_ref, decb_ref, cg_ref, gate_ref = (next(it) for _ in range(7))
    sf_in = sb_in = sf_out = sb_out = None
    if has_state_in:
        sf_in, sb_in = next(it), next(it)
    y_ref = next(it)
    if emit_state:
        sf_out, sb_out = next(it), next(it)
    o_scr, u_scr, dm_scr = next(it), next(it), next(it)

    nc = q_ref.shape[1] // CHUNK
    hps = q_ref.shape[2] // LANES
    unroll = max(1, min(nc, RET_UNROLL // hps))
    ri = lax.broadcasted_iota(jnp.int32, (CHUNK, CHUNK), 0).astype(F32)
    ci = lax.broadcasted_iota(jnp.int32, (CHUNK, CHUNK), 1).astype(F32)
    tok_col = lax.broadcasted_iota(jnp.int32, (CHUNK, 1), 0).astype(F32)
    tok_row = lax.broadcasted_iota(jnp.int32, (1, CHUNK), 1).astype(F32)
    rel_f, rel_b = ri - ci, ci - ri - 1.0

    cross_f, cross_b, kdec_f, kdec_b, cdec_f, cdec_b = [], [], [], [], [], []
    for h in range(hps):
        lg_f = -jnp.exp(decf_ref[h][:, :1])
        lg_b = -jnp.exp(decb_ref[h][:, :1])
        cross_f.append(jnp.exp(lg_f * (tok_col + 1.0)))
        cross_b.append(jnp.exp(lg_b * (CHUNK - 1.0 - tok_col)))
        kdec_f.append(jnp.exp(lg_f * (CHUNK - 1.0 - tok_row)))
        kdec_b.append(jnp.exp(lg_b * tok_row))
        cdec_f.append(jnp.exp(lg_f * CHUNK))
        cdec_b.append(jnp.exp(lg_b * CHUNK))
        dm_scr[h] = jnp.where(rel_f >= 0.0, jnp.exp(lg_f * jnp.maximum(rel_f, 0.0)),
                              jnp.exp(lg_b * jnp.maximum(rel_b, 0.0)))

    def chunk_rows(c):
        return pl.ds(pl.multiple_of(c * CHUNK, CHUNK), CHUNK)

    def head_cols(h):
        return slice(h * LANES, (h + 1) * LANES)

    def local_body(c, _):
        rows = chunk_rows(c)
        for h in range(hps):
            qh = q_ref[0, rows, head_cols(h)].astype(BF16)
            kf = k_ref[0, rows, head_cols(h)].astype(F32)
            vh = v_ref[0, rows, head_cols(h)].astype(BF16)
            kt = kf.T
            att = jnp.dot(qh, kt.astype(BF16), preferred_element_type=F32) * dm_scr[h]
            lhs = jnp.concatenate([att.astype(BF16), (kt * kdec_f[h]).astype(BF16), (kt * kdec_b[h]).astype(BF16)],
                                  axis=0)
            r = jnp.dot(lhs, vh, preferred_element_type=F32)
            o_scr[rows, head_cols(h)] = r[:CHUNK]
            u_scr[h, c, 0] = r[CHUNK:2 * CHUNK]
            u_scr[h, c, 1] = r[2 * CHUNK:]
        return 0
    lax.fori_loop(0, nc, local_body, 0, unroll=unroll)

    def scan_body(t, states):
        out = []
        for h in range(hps):
            s_f, s_b = states[2 * h], states[2 * h + 1]
            inc_f, inc_b = u_scr[h, t, 0], u_scr[h, nc - 1 - t, 1]
            u_scr[h, t, 0] = s_f
            u_scr[h, nc - 1 - t, 1] = s_b
            out += [cdec_f[h] * s_f + inc_f, cdec_b[h] * s_b + inc_b]
        return tuple(out)
    zero = jnp.zeros((CHUNK, CHUNK), F32)
    init = tuple(x for h in range(hps) for x in ((sf_in[0, h], sb_in[0, h]) if has_state_in else (zero, zero)))
    final = lax.fori_loop(0, nc, scan_body, init, unroll=2 if hps == 1 else 1)
    if emit_state:
        for h in range(hps):
            sf_out[0, h] = final[2 * h].astype(sf_out.dtype)
            sb_out[0, h] = final[2 * h + 1].astype(sb_out.dtype)

    def finish_body(c, _):
        rows = chunk_rows(c)
        for h in range(hps):
            qh = q_ref[0, rows, head_cols(h)].astype(BF16)
            states = jnp.concatenate([u_scr[h, c, 0], u_scr[h, c, 1]], axis=1).astype(BF16)
            r = jnp.dot(qh, states, preferred_element_type=F32)
            o = o_scr[rows, head_cols(h)] + cross_f[h] * r[:, :LANES] + cross_b[h] * r[:, LANES:]
            mu = jnp.mean(o, axis=-1, keepdims=True)
            d = o - mu
            var = jnp.mean(d * d, axis=-1, keepdims=True)
            y = d * lax.rsqrt(var + EPS) * cg_ref[h]
            y_ref[0, rows, head_cols(h)] = (y * gate_ref[0, rows, head_cols(h)]).astype(y_ref.dtype)
        return 0
    lax.fori_loop(0, nc, finish_body, 0, unroll=unroll)


def _retention(zq, zg, dec_f, dec_b, c_norm_g, states, emit_state, hps):
    b, n, _ = zq.shape
    w = hps * LANES
    seq_spec = lambda blk: pl.BlockSpec((1, n, w), lambda bi, g: (bi, 0, blk // hps + g))
    st_spec = pl.BlockSpec((1, hps, CHUNK, CHUNK), lambda bi, g: (bi, g, 0, 0))
    head_spec = pl.BlockSpec((hps, 1, LANES), lambda bi, g: (g, 0, 0))
    bcast = lambda p: jnp.broadcast_to(p.astype(F32)[:, None, None], (C_HEADS, 1, LANES))
    in_specs = [seq_spec(0), seq_spec(C_HEADS), seq_spec(2 * C_HEADS), head_spec, head_spec, head_spec,
                seq_spec(0)]
    args = [zq, zq, zq, bcast(dec_f), bcast(dec_b), c_norm_g.astype(F32)[:, None, :], zg]
    if states is not None:
        in_specs += [st_spec, st_spec]
        args += list(states)
    out_specs = [seq_spec(0)]
    out_shape = [jax.ShapeDtypeStruct((b, n, C_HEADS * LANES), BF16)]
    if emit_state:
        out_specs += [st_spec, st_spec]
        out_shape += [jax.ShapeDtypeStruct((b, C_HEADS, CHUNK, CHUNK), F32)] * 2
    return pl.pallas_call(
        functools.partial(_retention_kernel, has_state_in=states is not None, emit_state=emit_state),
        grid=(b, C_HEADS // hps),
        in_specs=in_specs,
        out_specs=out_specs,
        out_shape=out_shape,
        scratch_shapes=[pltpu.VMEM((n, w), F32),
                        pltpu.VMEM((hps, n // CHUNK, 2, CHUNK, CHUNK), F32),
                        pltpu.VMEM((hps, CHUNK, CHUNK), F32)],
        compiler_params=pltpu.CompilerParams(vmem_limit_bytes=VMEM_LIMIT),
        name="retention",
    )(*args)


def _out_kernel(y1_ref, y2_ref, x_ref, mod_ref, w_ref, fg_ref, o_ref, *, final_norm):
    half = y1_ref.shape[1]
    y = (jnp.dot(y1_ref[...], w_ref[:half, :], preferred_element_type=F32)
         + jnp.dot(y2_ref[...], w_ref[half:, :], preferred_element_type=F32))
    x = x_ref[...] + mod_ref[0, 2:3, :] * y
    if final_norm:
        x = x * lax.rsqrt(jnp.mean(x * x, axis=-1, keepdims=True) + EPS) * fg_ref[...]
    o_ref[...] = x


def _out_project(y1, y2, x2d, mod, mod_row_of_tile, w_bf16, final_g, *, tm, final_norm):
    t, d = x2d.shape
    half = y1.shape[1]
    return pl.pallas_call(
        functools.partial(_out_kernel, final_norm=final_norm),
        grid=(t // tm,),
        in_specs=[pl.BlockSpec((tm, half), lambda i: (i, 0)),
                  pl.BlockSpec((tm, half), lambda i: (i, 0)),
                  pl.BlockSpec((tm, d), lambda i: (i, 0)),
                  pl.BlockSpec((1, 3, d), lambda i: (mod_row_of_tile(i), 0, 0)),
                  pl.BlockSpec(w_bf16.shape, lambda i: (0, 0)),
                  pl.BlockSpec((1, d), lambda i: (0, 0))],
        out_specs=pl.BlockSpec((tm, d), lambda i: (i, 0)),
        out_shape=jax.ShapeDtypeStruct((t, d), F32),
        compiler_params=pltpu.CompilerParams(vmem_limit_bytes=VMEM_LIMIT),
        name="out_project",
    )(y1, y2, x2d, mod, w_bf16, final_g.reshape(1, d))


def _pair_perm():
    return np.concatenate([np.arange(h * HEAD, (h + 1) * HEAD) for h in PAIR_ORDER])


def _reorder(w, idx, axis):
    idx = np.asarray(idx)
    runs = np.split(idx, np.flatnonzero(np.diff(idx) != 1) + 1)
    return jnp.concatenate([lax.slice_in_dim(w, int(r[0]), int(r[-1]) + 1, axis=axis) for r in runs], axis=axis)


def _rope_tables(n):
    rows = n // GRID_W
    row = jnp.repeat(jnp.arange(rows, dtype=F32), GRID_W)
    col = jnp.tile(jnp.arange(GRID_W, dtype=F32), rows)
    nf = HEAD // 4
    inv = ROPE_THETA ** (-jnp.arange(nf, dtype=F32) / nf)
    ang = jnp.concatenate([row[:, None] * inv, col[:, None] * inv], axis=-1)
    cos, sin = jnp.cos(ang), jnp.sin(ang)
    return jnp.tile(cos, (1, 4)), jnp.tile(jnp.concatenate([-sin, sin], axis=-1), (1, 2))


def _pair_kv(cache):
    b, g, p, d = cache.shape
    return cache.transpose(0, 2, 1, 3).reshape(b, p, g * d)


def _unpair_kv(z, blk, n_blk, heads):
    b, p, _ = z.shape
    t = z[:, :, blk * LANES:(blk + n_blk) * LANES]
    return t.reshape(b, p, heads, t.shape[-1] // heads).transpose(0, 2, 1, 3)


def kernel(x_prompt, x_sample, cache_a_k, cache_a_v, cache_b_k, cache_b_v, state_c_fwd, state_c_bwd, cache_d_k, cache_d_v, c, c_ctx, norm_g, mod_w, mod_b, ab_w_in, ab_w_out, a_sink, b_lq1, b_lk1, b_lq2, b_lk2, b_norm_g, cd_w_in, cd_w_out, c_decay_f, c_decay_b, c_norm_g, d_q_norm_g, d_k_norm_g, final_g):
    depth = norm_g.shape[0]
    bp, sp, d = x_prompt.shape
    bs, ss, _ = x_sample.shape
    dt = x_prompt.dtype

    ctx_row = bs
    pad = (-(bs + 1)) % 8
    cond = jnp.concatenate([c, c_ctx[None, :], jnp.zeros((pad, d), c.dtype)], axis=0)
    mod = _modulation(cond, mod_w, mod_b).reshape(depth, cond.shape[0], 3, d)

    perm = _pair_perm()
    qkv_w = N_QKV_BLK * LANES
    rope_tabs = _rope_tables(ss)
    bd = jnp.asarray(np.kron(np.eye(LANES // HEAD), np.ones((HEAD, HEAD))), BF16)

    tm_s = 1024
    tm_p = 512
    tiles_per_seq = ss // tm_s
    row_s = lambda i: i // tiles_per_seq
    row_p = lambda i: ctx_row

    xp = x_prompt.reshape(bp * sp, d)
    xs = x_sample.reshape(bs * ss, d)
    outs = {k: [] for k in ("a_k", "a_v", "b_k", "b_v", "c_f", "c_b", "d_k", "d_v")}

    pend_p = pend_s = None

    def project(x2d, pending, row_fn, tm, **kw):
        res = list(_project(x2d, mod[layer], row_fn, norm_g[layer], w_in, tm=tm, prev=pending, **kw))
        if pending is not None:
            x2d = res.pop()
        return x2d, res

    for layer in range(depth):
        i = layer // 2
        if layer % 2 == 0:
            lam_init = 0.8 - 0.6 * math.exp(-0.3 * layer)
            w4 = N_QCOL * LANES
            ar = np.arange
            cols = np.concatenate([perm, ar(w4 + 2 * LANES, 4 * w4 + 2 * LANES), ar(w4, w4 + 2 * LANES),
                                   qkv_w + perm, ar(qkv_w + w4, qkv_w + 2 * w4)])
            w_in = _reorder(ab_w_in[i], cols, 1).astype(BF16)
            rows = np.arange(ab_w_out.shape[1])
            rows[0:N_QCOL * LANES] = perm
            w_out = _reorder(ab_w_out[i], rows, 0).astype(BF16)
            diff_params = (b_lq1[i][None], b_lk1[i][None], b_lq2[i][None], b_lk2[i][None], b_norm_g[i][None])

            xp, (zq, zg) = project(xp, pend_p, row_p, tm_p, kind="ab", rope_tabs=None, norm_params=None,
                                   out_dtype=F32)
            zq3, zg3 = zq.reshape(bp, sp, -1), zg.reshape(bp, sp, -1)
            ya = _ctx_attention(zq3, zg3, q_blk=0, k_blk=PAIR_K_BLK, v_blk=PAIR_V_BLK, kv_per_col=False,
                                gate_blk=0, mode="pair", sink=a_sink[i])
            yb = _ctx_attention(zq3, zg3, q_blk=4, k_blk=8, v_blk=12, kv_per_col=True, gate_blk=4,
                                mode="diff", diff_params=diff_params, lam_init=lam_init)
            pend_p = (ya.reshape(bp * sp, -1), yb.reshape(bp * sp, -1), mod[layer], w_out)
            outs["a_k"].append(_unpair_kv(zq3, PAIR_K_BLK, 1, 2))
            outs["a_v"].append(_unpair_kv(zq3, PAIR_V_BLK, 1, 2))
            outs["b_k"].append(_unpair_kv(zq3, 8, 4, 4))
            outs["b_v"].append(_unpair_kv(zq3, 12, 4, 4))

            xs, (zq, zg, vt) = project(xs, pend_s, row_s, tm_s, kind="ab", rope_tabs=rope_tabs, norm_params=None,
                                       out_dtype=BF16, vt_blocks=(PAIR_V_BLK, 12, 13, 14, 15))
            zq3, zg3 = zq.reshape(bs, ss, -1), zg.reshape(bs, ss, -1)
            ya = _window_attention(zq3, zg3, vt, _pair_kv(cache_a_k[:, i]),
                                   _pair_kv(cache_a_v[:, i]).transpose(0, 2, 1), a_sink[i])
            yb = _keymajor_attention(zq3, zg3, vt, cache_b_k[:, i], cache_b_v[:, i].transpose(0, 1, 3, 2),
                                     q_blk=4, k_blk=8, vt_blk=1, kv_per_col=True, gate_blk=4, mode="diff",
                                     tq=TQ_DENSE, tk=TK_DENSE, diff_params=diff_params, lam_init=lam_init)
            pend_s = (ya.reshape(bs * ss, -1), yb.reshape(bs * ss, -1), mod[layer], w_out)
        else:
            cols = np.arange(cd_w_in.shape[2])
            cols[12 * LANES:16 * LANES] = 12 * LANES + perm
            cols[qkv_w + N_QCOL * LANES:qkv_w + 2 * N_QCOL * LANES] = qkv_w + N_QCOL * LANES + perm
            w_in = _reorder(cd_w_in[i], cols, 1).astype(BF16)
            rows = np.arange(cd_w_out.shape[1])
            rows[N_QCOL * LANES:] = N_QCOL * LANES + perm
            w_out = _reorder(cd_w_out[i], rows, 0).astype(BF16)
            norm_params = (jnp.tile(d_q_norm_g[i], 2)[None], jnp.tile(d_k_norm_g[i], 2)[None], bd)

            xp, (zq, zg) = project(xp, pend_p, row_p, tm_p, kind="cd", rope_tabs=None, norm_params=norm_params,
                                   out_dtype=F32)
            zq3, zg3 = zq.reshape(bp, sp, -1), zg.reshape(bp, sp, -1)
            yc, s_f, s_b = _retention(zq3, zg3, c_decay_f[i], c_decay_b[i], c_norm_g[i], None, True,
                                      RET_HEADS_CTX)
            yd = _ctx_attention(zq3, zg3, q_blk=12, k_blk=PAIR_K_BLK, v_blk=PAIR_V_BLK, kv_per_col=False,
                                gate_blk=4, mode="pair")
            pend_p = (yc.reshape(bp * sp, -1), yd.reshape(bp * sp, -1), mod[layer], w_out)
            outs["c_f"].append(s_f.astype(dt))
            outs["c_b"].append(s_b.astype(dt))
            outs["d_k"].append(_unpair_kv(zq3, 16, 1, 2))
            outs["d_v"].append(_unpair_kv(zq3, 17, 1, 2))

            xs, (zq, zg, vt) = project(xs, pend_s, row_s, tm_s, kind="cd", rope_tabs=rope_tabs,
                                       norm_params=norm_params, out_dtype=BF16, vt_blocks=(PAIR_V_BLK,))
            zq3, zg3 = zq.reshape(bs, ss, -1), zg.reshape(bs, ss, -1)
            yc = _retention(zq3, zg3, c_decay_f[i], c_decay_b[i], c_norm_g[i],
                            (state_c_fwd[:, i], state_c_bwd[:, i]), False, RET_HEADS_LATENT)[0]
            yd = _keymajor_attention(zq3, zg3, vt, _pair_kv(cache_d_k[:, i]),
                                     _pair_kv(cache_d_v[:, i]).transpose(0, 2, 1),
                                     q_blk=12, k_blk=PAIR_K_BLK, vt_blk=0, kv_per_col=False, gate_blk=4,
                                     mode="pair",
                                     tq=TQ_DENSE, tk=TK_DENSE)
            pend_s = (yc.reshape(bs * ss, -1), yd.reshape(bs * ss, -1), mod[layer], w_out)

    xp = _out_project(pend_p[0], pend_p[1], xp, pend_p[2], row_p, pend_p[3], final_g, tm=tm_p, final_norm=True)
    xs = _out_project(pend_s[0], pend_s[1], xs, pend_s[2], row_s, pend_s[3], final_g, tm=tm_s, final_norm=True)

    stack = lambda k: jnp.stack(outs[k], axis=1)
    return (xp.reshape(bp, sp, d), xs.reshape(bs, ss, d), stack("a_k"), stack("a_v"), stack("b_k"), stack("b_v"),
            stack("c_f"), stack("c_b"), stack("d_k"), stack("d_v"))
```

```python
import functools
import math

import numpy as np
import jax
import jax.numpy as jnp
from jax import lax
from jax.experimental import pallas as pl
from jax.experimental.pallas import tpu as pltpu

F32 = jnp.float32
BF16 = jnp.bfloat16

LANES = 128
HEAD = 64
GRID_W = 64
CHUNK = 128
WINDOW = 128
ROPE_THETA = 10000.0
EPS = 1e-6
NEG = -1e30
VMEM_LIMIT = 56 * 1024 * 1024
TQ_DENSE = 2048
TK_DENSE = 512
SCORE_SLOTS = 4
WINDOW_QBLOCKS = 8
RET_HEADS_CTX = 4
RET_HEADS_LATENT = 1
RET_UNROLL_LOCAL = 16
RET_UNROLL_FINISH = 8
ONES_ROWS = 16
LOG2E = 1.4426950408889634

GQA_HEADS = 8
PAIR_ORDER = (0, 4, 1, 5, 2, 6, 3, 7)
N_QCOL = 4
C_HEADS = 4
C_DK = 128

N_QKV_BLK = 18
N_GATE_BLK = 8
PAIR_K_BLK, PAIR_V_BLK = 16, 17
AB_OPS = ("qrope",) * 8 + ("rope",) * 4 + ("plain",) * 4 + ("rope", "plain")
CD_OPS = ("plain",) * 4 + ("kscale",) * 4 + ("plain",) * 4 + ("qnorm",) * 4 + ("knorm", "plain")
PROJ_GROUPS = ((0, 4), (4, 8), (8, 12), (12, 16), (16, 18))


def _silu(x):
    return x / (1.0 + jnp.exp(-x))


def _lane_lo(shape):
    return lax.broadcasted_iota(jnp.int32, shape, len(shape) - 1) % LANES < HEAD


def _mod_kernel(cond_ref, w_ref, b_ref, o_ref):
    s = _silu(cond_ref[...])
    o_ref[0] = jnp.dot(s.astype(BF16), w_ref[0].astype(BF16), preferred_element_type=F32) + b_ref[0]


def _modulation(cond, mod_w, mod_b):
    depth, d, d3 = mod_w.shape
    rows = cond.shape[0]
    return pl.pallas_call(
        _mod_kernel,
        grid=(depth, d3 // d),
        in_specs=[pl.BlockSpec((rows, d), lambda l, j: (0, 0)),
                  pl.BlockSpec((1, d, d), lambda l, j: (l, 0, j)),
                  pl.BlockSpec((1, 1, d), lambda l, j: (l, 0, j))],
        out_specs=pl.BlockSpec((1, rows, d), lambda l, j: (l, 0, j)),
        out_shape=jax.ShapeDtypeStruct((depth, rows, d3), F32),
        compiler_params=pltpu.CompilerParams(vmem_limit_bytes=VMEM_LIMIT),
        name="modulation",
    )(cond, mod_w, mod_b.reshape(depth, 1, d3))


def _rot_half(x, first_half):
    return jnp.where(first_half, pltpu.roll(x, LANES - HEAD // 2, 1), pltpu.roll(x, HEAD // 2, 1))


def _head_sumsq(x, bd):
    sq = x * x
    hi = sq.astype(BF16)
    lo = (sq - hi.astype(F32)).astype(BF16)
    return jnp.dot(hi, bd, preferred_element_type=F32) + jnp.dot(lo, bd, preferred_element_type=F32)


def _proj_kernel(*refs, ops, groups, use_rope, has_norm, vt_blocks, has_prev):
    it = iter(refs)
    x_ref, mod_ref, g_ref, w_ref = next(it), next(it), next(it), next(it)
    cos_ref = sin_ref = qg_ref = kg_ref = bd_ref = vt_ref = None
    if has_prev:
        y1_ref, y2_ref, pmod_ref, pw_ref = next(it), next(it), next(it), next(it)
    if use_rope:
        cos_ref, sin_ref = next(it), next(it)
    if has_norm:
        qg_ref, kg_ref, bd_ref = next(it), next(it), next(it)
    zq_ref, zg_ref = next(it), next(it)
    if vt_blocks:
        vt_ref = next(it)
    if has_prev:
        xnew_ref = next(it)

    x = x_ref[...]
    if has_prev:
        half = y1_ref.shape[1]
        y = (jnp.dot(y1_ref[...], pw_ref[:half, :], preferred_element_type=F32)
             + jnp.dot(y2_ref[...], pw_ref[half:, :], preferred_element_type=F32))
        x = x + pmod_ref[0, 2:3, :] * y
        xnew_ref[...] = x
    h = x * lax.rsqrt(jnp.mean(x * x, axis=-1, keepdims=True) + EPS) * g_ref[...]
    h = h * (1.0 + mod_ref[0, 1:2, :]) + mod_ref[0, 0:1, :]
    hb = h.astype(BF16)

    tm = x.shape[0]
    if use_rope:
        cos, sin = cos_ref[...], sin_ref[...]
        first_half = lax.broadcasted_iota(jnp.int32, (tm, LANES), 1) % HEAD < HEAD // 2

    def rope(z):
        if not use_rope:
            return z
        return z * cos + _rot_half(z, first_half) * sin

    def head_norm(z, gain_ref):
        ss = _head_sumsq(z, bd_ref[...])
        return z * lax.rsqrt(ss * (1.0 / HEAD) + EPS) * gain_ref[...]

    q_scale = HEAD ** -0.5 * LOG2E
    for b0, b1 in groups:
        z = jnp.dot(hb, w_ref[:, b0 * LANES:b1 * LANES], preferred_element_type=F32)
        for j in range(b0, b1):
            zz = z[:, (j - b0) * LANES:(j - b0 + 1) * LANES]
            op = ops[j]
            if op == "qrope":
                zz = rope(zz * q_scale)
            elif op == "rope":
                zz = rope(zz)
            elif op == "kscale":
                zz = zz * (C_DK ** -0.5)
            elif op == "qnorm":
                zz = rope(head_norm(zz, qg_ref)) * q_scale
            elif op == "knorm":
                zz = rope(head_norm(zz, kg_ref))
            zq_ref[:, j * LANES:(j + 1) * LANES] = zz.astype(zq_ref.dtype)
            if j in vt_blocks:
                vt_ref[vt_blocks.index(j)] = zz.T.astype(vt_ref.dtype)
    half = N_GATE_BLK // 2
    for g0 in (0, half):
        c0 = (N_QKV_BLK + g0) * LANES
        z = jnp.dot(hb, w_ref[:, c0:c0 + half * LANES], preferred_element_type=F32)
        zg_ref[:, g0 * LANES:(g0 + half) * LANES] = _silu(z)


def _project(x2d, mod, mod_row_of_tile, norm_g, w_bf16, *, tm, kind, rope_tabs, norm_params, out_dtype,
             vt_blocks=(), prev=None):
    t, d = x2d.shape
    out_specs = [pl.BlockSpec((tm, N_QKV_BLK * LANES), lambda i: (i, 0)),
                 pl.BlockSpec((tm, N_GATE_BLK * LANES), lambda i: (i, 0))]
    out_shape = [jax.ShapeDtypeStruct((t, N_QKV_BLK * LANES), out_dtype),
                 jax.ShapeDtypeStruct((t, N_GATE_BLK * LANES), F32)]
    if vt_blocks:
        out_specs.append(pl.BlockSpec((len(vt_blocks), LANES, tm), lambda i: (0, 0, i)))
        out_shape.append(jax.ShapeDtypeStruct((len(vt_blocks), LANES, t), out_dtype))
    if prev is not None:
        out_specs.append(pl.BlockSpec((tm, d), lambda i: (i, 0)))
        out_shape.append(jax.ShapeDtypeStruct((t, d), F32))
    use_rope = rope_tabs is not None
    has_norm = kind == "cd"
    ops, groups = (AB_OPS if kind == "ab" else CD_OPS), PROJ_GROUPS
    in_specs = [pl.BlockSpec((tm, d), lambda i: (i, 0)),
                pl.BlockSpec((1, 3, d), lambda i: (mod_row_of_tile(i), 0, 0)),
                pl.BlockSpec((1, d), lambda i: (0, 0)),
                pl.BlockSpec(w_bf16.shape, lambda i: (0, 0))]
    args = [x2d, mod, norm_g.reshape(1, d), w_bf16]
    if prev is not None:
        y1, y2, prev_mod, prev_w = prev
        in_specs += [pl.BlockSpec((tm, y1.shape[1]), lambda i: (i, 0)),
                     pl.BlockSpec((tm, y2.shape[1]), lambda i: (i, 0)),
                     pl.BlockSpec((1, 3, d), lambda i: (mod_row_of_tile(i), 0, 0)),
                     pl.BlockSpec(prev_w.shape, lambda i: (0, 0))]
        args += [y1, y2, prev_mod, prev_w]
    if use_rope:
        n_seq_tiles = rope_tabs[0].shape[0] // tm
        in_specs += [pl.BlockSpec((tm, LANES), lambda i: (i % n_seq_tiles, 0))] * 2
        args += list(rope_tabs)
    if has_norm:
        in_specs += [pl.BlockSpec((1, LANES), lambda i: (0, 0))] * 2 + [pl.BlockSpec((LANES, LANES), lambda i: (0, 0))]
        args += list(norm_params)
    return pl.pallas_call(
        functools.partial(_proj_kernel, ops=ops, groups=groups, use_rope=use_rope, has_norm=has_norm,
                          vt_blocks=tuple(vt_blocks), has_prev=prev is not None),
        grid=(t // tm,),
        in_specs=in_specs,
        out_specs=out_specs,
        out_shape=out_shape,
        compiler_params=pltpu.CompilerParams(vmem_limit_bytes=VMEM_LIMIT),
        name="project_" + kind,
    )(*args)


def _pair_rows(q):
    lo = jnp.where(_lane_lo((1, LANES)), 1.0, 0.0).astype(BF16)
    qb = q.astype(BF16)
    return jnp.concatenate([qb * lo, qb * (1.0 - lo).astype(BF16)], axis=0)


def _pair_cols(q):
    qt = q.astype(F32).T
    top = lax.broadcasted_iota(jnp.int32, qt.shape, 0) < HEAD
    return jnp.concatenate([jnp.where(top, qt, 0.0), jnp.where(top, 0.0, qt)], axis=1).astype(BF16)


def _ctx_attn_kernel(*refs, mode, has_sink, kv_per_col, lam_init):
    it = iter(refs)
    q_ref, k_ref, v_ref = next(it), next(it), next(it)
    sink_ref = next(it) if has_sink else None
    if mode == "diff":
        lq1_ref, lk1_ref, lq2_ref, lk2_ref, bg_ref = (next(it) for _ in range(5))
    gate_ref, y_ref = next(it), next(it)

    tq = q_ref.shape[1]
    lo = _lane_lo((tq, LANES))
    for c in range(N_QCOL):
        cols = slice(c * LANES, (c + 1) * LANES)
        kv_cols = cols if kv_per_col else slice(0, LANES)
        qrows = _pair_rows(q_ref[0, :, cols])
        s = lax.dot_general(qrows, k_ref[0, :, kv_cols].astype(BF16), (((1,), (1,)), ((), ())),
                            preferred_element_type=F32)
        v = v_ref[0, :, kv_cols].astype(BF16)
        halves = []
        for half, head in ((s[:tq], c), (s[tq:], c + N_QCOL)):
            m = jnp.max(half, axis=-1, keepdims=True)
            if has_sink:
                sk = sink_ref[head] * LOG2E
                m = jnp.maximum(m, sk)
            p = jnp.exp2(half - m)
            l = jnp.sum(p, axis=-1, keepdims=True)
            if has_sink:
                l = l + jnp.exp2(sk - m)
            halves.append(jnp.dot(p.astype(BF16), v, preferred_element_type=F32) / l)
        o2 = jnp.concatenate(halves, axis=0)
        if mode == "pair":
            o = jnp.where(lo, o2[:tq], o2[tq:])
        else:
            lam = (jnp.exp(jnp.sum(lq1_ref[...] * lk1_ref[...], axis=-1, keepdims=True))
                   - jnp.exp(jnp.sum(lq2_ref[...] * lk2_ref[...], axis=-1, keepdims=True)) + lam_init)
            o = o2[:tq] - lam * o2[tq:]
            o = o * lax.rsqrt(jnp.mean(o * o, axis=-1, keepdims=True) + EPS) * bg_ref[...] * (1.0 - lam_init)
        y_ref[0, :, cols] = (o * gate_ref[0, :, cols]).astype(y_ref.dtype)


def _ctx_attention(zq, zg, *, q_blk, k_blk, v_blk, kv_per_col, gate_blk, mode,
                   sink=None, diff_params=None, lam_init=0.0):
    b, n, _ = zq.shape
    w4 = N_QCOL * LANES
    kv_w = w4 if kv_per_col else LANES
    assert (q_blk * LANES) % w4 == 0 and (gate_blk * LANES) % w4 == 0
    assert (k_blk * LANES) % kv_w == 0 and (v_blk * LANES) % kv_w == 0
    in_specs = [pl.BlockSpec((1, n, w4), lambda bi: (bi, 0, q_blk * LANES // w4)),
                pl.BlockSpec((1, n, kv_w), lambda bi: (bi, 0, k_blk * LANES // kv_w)),
                pl.BlockSpec((1, n, kv_w), lambda bi: (bi, 0, v_blk * LANES // kv_w))]
    args = [zq, zq, zq]
    if sink is not None:
        in_specs.append(pl.BlockSpec(memory_space=pltpu.SMEM))
        args.append(sink)
    if mode == "diff":
        in_specs += [pl.BlockSpec((1, HEAD), lambda bi: (0, 0))] * 4
        in_specs.append(pl.BlockSpec((1, LANES), lambda bi: (0, 0)))
        args += list(diff_params)
    in_specs.append(pl.BlockSpec((1, n, w4), lambda bi: (bi, 0, gate_blk * LANES // w4)))
    args.append(zg)
    return pl.pallas_call(
        functools.partial(_ctx_attn_kernel, mode=mode, has_sink=sink is not None, kv_per_col=kv_per_col,
                          lam_init=lam_init),
        grid=(b,),
        in_specs=in_specs,
        out_specs=pl.BlockSpec((1, n, w4), lambda bi: (bi, 0, 0)),
        out_shape=jax.ShapeDtypeStruct((b, n, w4), BF16),
        compiler_params=pltpu.CompilerParams(vmem_limit_bytes=VMEM_LIMIT),
        name="ctx_attention_" + mode,
    )(*args)


def _keymajor_attn_kernel(*refs, mode, tk, lam_init):
    it = iter(refs)
    q_ref, k_ref, vt_ref, kx_ref, vxt_ref = (next(it) for _ in range(5))
    if mode == "diff":
        lq1_ref, lk1_ref, lq2_ref, lk2_ref, bg_ref = (next(it) for _ in range(5))
    gate_ref, y_ref, s_scr, sx_scr = next(it), next(it), next(it), next(it)

    tq = q_ref.shape[1]
    qt = _pair_cols(q_ref[0])

    def scores(k, slot):
        s = jnp.dot(k, qt, preferred_element_type=F32)
        slot[:, :2 * tq] = s
        return jnp.max(s, axis=0, keepdims=True)

    vd = HEAD if mode == "pair" else LANES

    def with_ones(vt):
        return jnp.concatenate([vt, jnp.ones((ONES_ROWS, vt.shape[1]), BF16)], axis=0)

    def consume(slot, cmax, vt, m, acc_lo, acc_hi):
        m_new = jnp.maximum(m, cmax)
        alpha = jnp.exp2(m - m_new)
        p = jnp.exp2(slot[:, :2 * tq] - m_new).astype(BF16)
        if mode == "pair":
            v_lo, v_hi = with_ones(vt[:HEAD]), with_ones(vt[HEAD:])
        else:
            v_lo = v_hi = with_ones(vt)
        acc_lo = alpha[:, :tq] * acc_lo + jnp.dot(v_lo, p[:, :tq], preferred_element_type=F32)
        acc_hi = alpha[:, tq:] * acc_hi + jnp.dot(v_hi, p[:, tq:], preferred_element_type=F32)
        return m_new, acc_lo, acc_hi

    n_chunks = k_ref.shape[1] // tk
    carry = (jnp.full((1, 2 * tq), NEG, F32), jnp.zeros((vd + ONES_ROWS, tq), F32),
             jnp.zeros((vd + ONES_ROWS, tq), F32))
    n_slots = s_scr.shape[0]
    ahead = n_slots - 1
    chunk_scores = lambda j: scores(k_ref[0, j * tk:(j + 1) * tk, :], s_scr.at[j % n_slots])
    cmax = [chunk_scores(j) for j in range(min(ahead, n_chunks))]
    cmax_x = scores(kx_ref[0].astype(BF16), sx_scr)
    for j in range(n_chunks):
        if j + ahead < n_chunks:
            cmax.append(chunk_scores(j + ahead))
        carry = consume(s_scr.at[j % n_slots], cmax[j], vt_ref[0, :, j * tk:(j + 1) * tk], *carry)
    _, acc_lo, acc_hi = consume(sx_scr, cmax_x, vxt_ref[0].astype(BF16), *carry)

    ot_lo = acc_lo[:vd] / acc_lo[vd:vd + 1]
    ot_hi = acc_hi[:vd] / acc_hi[vd:vd + 1]
    if mode == "pair":
        o = jnp.concatenate([ot_lo, ot_hi], axis=0).T
    else:
        lam = (jnp.exp(jnp.sum(lq1_ref[...] * lk1_ref[...], axis=-1, keepdims=True))
               - jnp.exp(jnp.sum(lq2_ref[...] * lk2_ref[...], axis=-1, keepdims=True)) + lam_init)
        o = (ot_lo - lam * ot_hi).T
        o = o * lax.rsqrt(jnp.mean(o * o, axis=-1, keepdims=True) + EPS) * bg_ref[...] * (1.0 - lam_init)
    y_ref[0] = (o * gate_ref[0]).astype(y_ref.dtype)


def _keymajor_attention(zq, zg, vt, kx, vxt, *, q_blk, k_blk, vt_blk, kv_per_col, gate_blk, mode, tq, tk,
                        diff_params=None, lam_init=0.0):
    b, n, _ = zq.shape
    per = (lambda c: c) if kv_per_col else (lambda c: 0)
    if kx.ndim == 4:
        kx_spec = pl.BlockSpec((None, 1) + kx.shape[2:], lambda bi, c, qi: (bi, c, 0, 0))
        vx_spec = pl.BlockSpec((None, 1) + vxt.shape[2:], lambda bi, c, qi: (bi, c, 0, 0))
    else:
        kx_spec = pl.BlockSpec((1,) + kx.shape[1:], lambda bi, c, qi: (bi, 0, 0))
        vx_spec = pl.BlockSpec((1,) + vxt.shape[1:], lambda bi, c, qi: (bi, 0, 0))
    in_specs = [pl.BlockSpec((1, tq, LANES), lambda bi, c, qi: (bi, qi, q_blk + c)),
                pl.BlockSpec((1, n, LANES), lambda bi, c, qi: (bi, 0, k_blk + per(c))),
                pl.BlockSpec((1, LANES, n), lambda bi, c, qi: (vt_blk + per(c), 0, bi)),
                kx_spec, vx_spec]
    args = [zq, zq, vt, kx, vxt]
    if mode == "diff":
        in_specs += [pl.BlockSpec((1, HEAD), lambda bi, c, qi: (0, 0))] * 4
        in_specs.append(pl.BlockSpec((1, LANES), lambda bi, c, qi: (0, 0)))
        args += list(diff_params)
    in_specs.append(pl.BlockSpec((1, tq, LANES), lambda bi, c, qi: (bi, qi, gate_blk + c)))
    args.append(zg)
    return pl.pallas_call(
        functools.partial(_keymajor_attn_kernel, mode=mode, tk=tk, lam_init=lam_init),
        grid=(b, N_QCOL, n // tq),
        in_specs=in_specs,
        out_specs=pl.BlockSpec((1, tq, LANES), lambda bi, c, qi: (bi, qi, c)),
        out_shape=jax.ShapeDtypeStruct((b, n, N_QCOL * LANES), BF16),
        scratch_shapes=[pltpu.VMEM((SCORE_SLOTS, tk, 2 * tq + LANES), F32),
                        pltpu.VMEM((kx.shape[-2], 2 * tq + LANES), F32)],
        compiler_params=pltpu.CompilerParams(vmem_limit_bytes=VMEM_LIMIT),
        name="attention_keymajor_" + mode,
    )(*args)


def _window_attn_kernel(q_ref, k_ref, vt_ref, kx_ref, vxt_ref, sink_ref, gate_ref, y_ref, s_scr):
    n = k_ref.shape[1]
    span = 3 * CHUNK
    sk = jnp.concatenate([jnp.full((1, CHUNK), sink_ref[h] * LOG2E, F32) for h in range(GQA_HEADS)], axis=1)
    kx = kx_ref[0].astype(BF16)
    vxt = vxt_ref[0].astype(BF16)
    half = N_QCOL * CHUNK

    def with_ones(vt):
        return jnp.concatenate([vt, jnp.ones((ONES_ROWS, vt.shape[1]), BF16)], axis=0)

    starts, maxes = [], []
    for sub in range(WINDOW_QBLOCKS):
        blk = pl.program_id(1) * WINDOW_QBLOCKS + sub
        qts = [_pair_cols(q_ref[0, sub * CHUNK:(sub + 1) * CHUNK, c * LANES:(c + 1) * LANES])
               for c in range(N_QCOL)]
        qt = jnp.concatenate([t[:, :CHUNK] for t in qts] + [t[:, CHUNK:] for t in qts], axis=1)
        w0 = pl.multiple_of(jnp.clip((blk - 1) * CHUNK, 0, n - span), CHUNK)
        s_w = jnp.dot(k_ref[0, pl.ds(w0, span), :], qt, preferred_element_type=F32)
        s_x = jnp.dot(kx, qt, preferred_element_type=F32)
        dist = (w0 - blk * CHUNK + lax.broadcasted_iota(jnp.int32, (span, CHUNK), 0)
                - lax.broadcasted_iota(jnp.int32, (span, CHUNK), 1))
        bias = jnp.where((dist >= -WINDOW) & (dist <= WINDOW), 0.0, NEG)
        s_w = s_w + jnp.concatenate([bias] * GQA_HEADS, axis=1)
        s_scr[sub, :span, :2 * half] = s_w
        s_scr[sub, span:, :2 * half] = s_x
        starts.append(w0)
        maxes.append(jnp.maximum(jnp.maximum(jnp.max(s_w, axis=0, keepdims=True),
                                             jnp.max(s_x, axis=0, keepdims=True)), sk))

    for sub in range(WINDOW_QBLOCKS):
        w0, m = starts[sub], maxes[sub]
        p = jnp.exp2(s_scr[sub, :, :2 * half] - m).astype(BF16)
        vt = jnp.concatenate([vt_ref[0, :, pl.ds(w0, span)], vxt], axis=1)
        acc_lo = jnp.dot(with_ones(vt[:HEAD]), p[:, :half], preferred_element_type=F32)
        acc_hi = jnp.dot(with_ones(vt[HEAD:]), p[:, half:], preferred_element_type=F32)
        sink_p = jnp.exp2(sk - m)
        ot_lo = acc_lo[:HEAD] / (acc_lo[HEAD:HEAD + 1] + sink_p[:, :half])
        ot_hi = acc_hi[:HEAD] / (acc_hi[HEAD:HEAD + 1] + sink_p[:, half:])
        o = jnp.concatenate(
            [jnp.concatenate([ot_lo[:, c * CHUNK:(c + 1) * CHUNK], ot_hi[:, c * CHUNK:(c + 1) * CHUNK]], axis=0).T
             for c in range(N_QCOL)], axis=1)
        rows = slice(sub * CHUNK, (sub + 1) * CHUNK)
        y_ref[0, rows, :] = (o * gate_ref[0, rows, :]).astype(y_ref.dtype)


def _window_attention(zq, zg, vt, kx, vxt, sink):
    b, n, _ = zq.shape
    w4 = N_QCOL * LANES
    tq = WINDOW_QBLOCKS * CHUNK
    return pl.pallas_call(
        _window_attn_kernel,
        grid=(b, n // tq),
        in_specs=[pl.BlockSpec((1, tq, w4), lambda bi, i: (bi, i, 0)),
                  pl.BlockSpec((1, n, LANES), lambda bi, i: (bi, 0, PAIR_K_BLK)),
                  pl.BlockSpec((1, LANES, n), lambda bi, i: (0, 0, bi)),
                  pl.BlockSpec((1,) + kx.shape[1:], lambda bi, i: (bi, 0, 0)),
                  pl.BlockSpec((1,) + vxt.shape[1:], lambda bi, i: (bi, 0, 0)),
                  pl.BlockSpec(memory_space=pltpu.SMEM),
                  pl.BlockSpec((1, tq, w4), lambda bi, i: (bi, i, 0))],
        out_specs=pl.BlockSpec((1, tq, w4), lambda bi, i: (bi, i, 0)),
        out_shape=jax.ShapeDtypeStruct((b, n, w4), BF16),
        scratch_shapes=[pltpu.VMEM((WINDOW_QBLOCKS, 3 * CHUNK + kx.shape[1], GQA_HEADS * CHUNK + LANES), F32)],
        compiler_params=pltpu.CompilerParams(vmem_limit_bytes=VMEM_LIMIT),
        name="attention_window",
    )(zq, zq, vt, kx, vxt, sink, zg)


def _retention_kernel(*refs, has_state_in, emit_state):
    it = iter(refs)
    q_ref, k_ref, v_ref, decf_ref, decb_ref, cg_ref, gate_ref = (next(it) for _ in range(7))
    sf_in = sb_in = sf_out = sb_out = None
    if has_state_in:
        sf_in, sb_in = next(it), next(it)
    y_ref = next(it)
    if emit_state:
        sf_out, sb_out = next(it), next(it)
    o_scr, u_scr, dm_scr = next(it), next(it), next(it)

    nc = q_ref.shape[1] // CHUNK
    hps = q_ref.shape[2] // LANES
    unroll_local = max(1, min(nc, RET_UNROLL_LOCAL // hps))
    unroll_finish = max(1, min(nc, RET_UNROLL_FINISH // hps))
    ri = lax.broadcasted_iota(jnp.int32, (CHUNK, CHUNK), 0).astype(F32)
    ci = lax.broadcasted_iota(jnp.int32, (CHUNK, CHUNK), 1).astype(F32)
    tok_col = lax.broadcasted_iota(jnp.int32, (CHUNK, 1), 0).astype(F32)
    tok_row = lax.broadcasted_iota(jnp.int32, (1, CHUNK), 1).astype(F32)
    rel_f, rel_b = ri - ci, ci - ri - 1.0

    cross_f, cross_b, kdec_f, kdec_b, cdec_f, cdec_b = [], [], [], [], [], []
    for h in range(hps):
        lg_f = -jnp.exp(decf_ref[h][:, :1])
        lg_b = -jnp.exp(decb_ref[h][:, :1])
        cross_f.append(jnp.exp(lg_f * (tok_col + 1.0)))
        cross_b.append(jnp.exp(lg_b * (CHUNK - 1.0 - tok_col)))
        kdec_f.append(jnp.exp(lg_f * (CHUNK - 1.0 - tok_row)))
        kdec_b.append(jnp.exp(lg_b * tok_row))
        cdec_f.append(jnp.exp(lg_f * CHUNK))
        cdec_b.append(jnp.exp(lg_b * CHUNK))
        dm_scr[h] = jnp.where(rel_f >= 0.0, jnp.exp(lg_f * jnp.maximum(rel_f, 0.0)),
                              jnp.exp(lg_b * jnp.maximum(rel_b, 0.0)))

    def chunk_rows(c):
        return pl.ds(pl.multiple_of(c * CHUNK, CHUNK), CHUNK)

    def head_cols(h):
        return slice(h * LANES, (h + 1) * LANES)

    def local_body(c, _):
        rows = chunk_rows(c)
        for h in range(hps):
            qh = q_ref[0, rows, head_cols(h)].astype(BF16)
            kf = k_ref[0, rows, head_cols(h)].astype(F32)
            vh = v_ref[0, rows, head_cols(h)].astype(BF16)
            kt = kf.T
            att = jnp.dot(qh, kt.astype(BF16), preferred_element_type=F32) * dm_scr[h]
            lhs = jnp.concatenate([att.astype(BF16), (kt * kdec_f[h]).astype(BF16), (kt * kdec_b[h]).astype(BF16)],
                                  axis=0)
            r = jnp.dot(lhs, vh, preferred_element_type=F32)
            o_scr[rows, head_cols(h)] = r[:CHUNK]
            u_scr[h, c, 0] = r[CHUNK:2 * CHUNK]
            u_scr[h, c, 1] = r[2 * CHUNK:]
        return 0
    lax.fori_loop(0, nc, local_body, 0, unroll=unroll_local)

    def scan_body(t, states):
        out = []
        for h in range(hps):
            s_f, s_b = states[2 * h], states[2 * h + 1]
            inc_f, inc_b = u_scr[h, t, 0], u_scr[h, nc - 1 - t, 1]
            u_scr[h, t, 0] = s_f
            u_scr[h, nc - 1 - t, 1] = s_b
            out += [cdec_f[h] * s_f + inc_f, cdec_b[h] * s_b + inc_b]
        return tuple(out)
    zero = jnp.zeros((CHUNK, CHUNK), F32)
    init = tuple(x for h in range(hps) for x in ((sf_in[0, h], sb_in[0, h]) if has_state_in else (zero, zero)))
    final = lax.fori_loop(0, nc, scan_body, init, unroll=2 if hps == 1 else 1)
    if emit_state:
        for h in range(hps):
            sf_out[0, h] = final[2 * h].astype(sf_out.dtype)
            sb_out[0, h] = final[2 * h + 1].astype(sb_out.dtype)

    def finish_body(c, _):
        rows = chunk_rows(c)
        for h in range(hps):
            qh = q_ref[0, rows, head_cols(h)].astype(BF16)
            states = jnp.concatenate([u_scr[h, c, 0], u_scr[h, c, 1]], axis=1).astype(BF16)
            r = jnp.dot(qh, states, preferred_element_type=F32)
            o = o_scr[rows, head_cols(h)] + cross_f[h] * r[:, :LANES] + cross_b[h] * r[:, LANES:]
            mu = jnp.mean(o, axis=-1, keepdims=True)
            d = o - mu
            var = jnp.mean(d * d, axis=-1, keepdims=True)
            y = d * lax.rsqrt(var + EPS) * cg_ref[h]
            y_ref[0, rows, head_cols(h)] = (y * gate_ref[0, rows, head_cols(h)]).astype(y_ref.dtype)
        return 0
    lax.fori_loop(0, nc, finish_body, 0, unroll=unroll_finish)


def _retention(zq, zg, dec_f, dec_b, c_norm_g, states, emit_state, hps):
    b, n, _ = zq.shape
    w = hps * LANES
    seq_spec = lambda blk: pl.BlockSpec((1, n, w), lambda bi, g: (bi, 0, blk // hps + g))
    st_spec = pl.BlockSpec((1, hps, CHUNK, CHUNK), lambda bi, g: (bi, g, 0, 0))
    head_spec = pl.BlockSpec((hps, 1, LANES), lambda bi, g: (g, 0, 0))
    bcast = lambda p: jnp.broadcast_to(p.astype(F32)[:, None, None], (C_HEADS, 1, LANES))
    in_specs = [seq_spec(0), seq_spec(C_HEADS), seq_spec(2 * C_HEADS), head_spec, head_spec, head_spec,
                seq_spec(0)]
    args = [zq, zq, zq, bcast(dec_f), bcast(dec_b), c_norm_g.astype(F32)[:, None, :], zg]
    if states is not None:
        in_specs += [st_spec, st_spec]
        args += list(states)
    out_specs = [seq_spec(0)]
    out_shape = [jax.ShapeDtypeStruct((b, n, C_HEADS * LANES), BF16)]
    if emit_state:
        out_specs += [st_spec, st_spec]
        out_shape += [jax.ShapeDtypeStruct((b, C_HEADS, CHUNK, CHUNK), F32)] * 2
    return pl.pallas_call(
        functools.partial(_retention_kernel, has_state_in=states is not None, emit_state=emit_state),
        grid=(b, C_HEADS // hps),
        in_specs=in_specs,
        out_specs=out_specs,
        out_shape=out_shape,
        scratch_shapes=[pltpu.VMEM((n, w), F32),
                        pltpu.VMEM((hps, n // CHUNK, 2, CHUNK, CHUNK), F32),
                        pltpu.VMEM((hps, CHUNK, CHUNK), F32)],
        compiler_params=pltpu.CompilerParams(vmem_limit_bytes=VMEM_LIMIT),
        name="retention",
    )(*args)


def _out_kernel(y1_ref, y2_ref, x_ref, mod_ref, w_ref, fg_ref, o_ref, *, final_norm):
    half = y1_ref.shape[1]
    y = (jnp.dot(y1_ref[...], w_ref[:half, :], preferred_element_type=F32)
         + jnp.dot(y2_ref[...], w_ref[half:, :], preferred_element_type=F32))
    x = x_ref[...] + mod_ref[0, 2:3, :] * y
    if final_norm:
        x = x * lax.rsqrt(jnp.mean(x * x, axis=-1, keepdims=True) + EPS) * fg_ref[...]
    o_ref[...] = x


def _out_project(y1, y2, x2d, mod, mod_row_of_tile, w_bf16, final_g, *, tm, final_norm):
    t, d = x2d.shape
    half = y1.shape[1]
    return pl.pallas_call(
        functools.partial(_out_kernel, final_norm=final_norm),
        grid=(t // tm,),
        in_specs=[pl.BlockSpec((tm, half), lambda i: (i, 0)),
                  pl.BlockSpec((tm, half), lambda i: (i, 0)),
                  pl.BlockSpec((tm, d), lambda i: (i, 0)),
                  pl.BlockSpec((1, 3, d), lambda i: (mod_row_of_tile(i), 0, 0)),
                  pl.BlockSpec(w_bf16.shape, lambda i: (0, 0)),
                  pl.BlockSpec((1, d), lambda i: (0, 0))],
        out_specs=pl.BlockSpec((tm, d), lambda i: (i, 0)),
        out_shape=jax.ShapeDtypeStruct((t, d), F32),
        compiler_params=pltpu.CompilerParams(vmem_limit_bytes=VMEM_LIMIT),
        name="out_project",
    )(y1, y2, x2d, mod, w_bf16, final_g.reshape(1, d))


def _pair_perm():
    return np.concatenate([np.arange(h * HEAD, (h + 1) * HEAD) for h in PAIR_ORDER])


def _reorder(w, idx, axis):
    idx = np.asarray(idx)
    runs = np.split(idx, np.flatnonzero(np.diff(idx) != 1) + 1)
    return jnp.concatenate([lax.slice_in_dim(w, int(r[0]), int(r[-1]) + 1, axis=axis) for r in runs], axis=axis)


def _rope_tables(n):
    rows = n // GRID_W
    row = jnp.repeat(jnp.arange(rows, dtype=F32), GRID_W)
    col = jnp.tile(jnp.arange(GRID_W, dtype=F32), rows)
    nf = HEAD // 4
    inv = ROPE_THETA ** (-jnp.arange(nf, dtype=F32) / nf)
    ang = jnp.concatenate([row[:, None] * inv, col[:, None] * inv], axis=-1)
    cos, sin = jnp.cos(ang), jnp.sin(ang)
    return jnp.tile(cos, (1, 4)), jnp.tile(jnp.concatenate([-sin, sin], axis=-1), (1, 2))


def _pair_kv(cache):
    b, g, p, d = cache.shape
    return cache.transpose(0, 2, 1, 3).reshape(b, p, g * d)


def _unpair_kv(z, blk, n_blk, heads):
    b, p, _ = z.shape
    t = z[:, :, blk * LANES:(blk + n_blk) * LANES]
    return t.reshape(b, p, heads, t.shape[-1] // heads).transpose(0, 2, 1, 3)


def kernel(x_prompt, x_sample, cache_a_k, cache_a_v, cache_b_k, cache_b_v, state_c_fwd, state_c_bwd, cache_d_k, cache_d_v, c, c_ctx, norm_g, mod_w, mod_b, ab_w_in, ab_w_out, a_sink, b_lq1, b_lk1, b_lq2, b_lk2, b_norm_g, cd_w_in, cd_w_out, c_decay_f, c_decay_b, c_norm_g, d_q_norm_g, d_k_norm_g, final_g):
    depth = norm_g.shape[0]
    bp, sp, d = x_prompt.shape
    bs, ss, _ = x_sample.shape
    dt = x_prompt.dtype

    ctx_row = bs
    pad = (-(bs + 1)) % 8
    cond = jnp.concatenate([c, c_ctx[None, :], jnp.zeros((pad, d), c.dtype)], axis=0)
    mod = _modulation(cond, mod_w, mod_b).reshape(depth, cond.shape[0], 3, d)

    perm = _pair_perm()
    qkv_w = N_QKV_BLK * LANES
    rope_tabs = _rope_tables(ss)
    bd = jnp.asarray(np.kron(np.eye(LANES // HEAD), np.ones((HEAD, HEAD))), BF16)

    tm_s = 1024
    tm_p = 512
    tiles_per_seq = ss // tm_s
    row_s = lambda i: i // tiles_per_seq
    row_p = lambda i: ctx_row

    xp = x_prompt.reshape(bp * sp, d)
    xs = x_sample.reshape(bs * ss, d)
    outs = {k: [] for k in ("a_k", "a_v", "b_k", "b_v", "c_f", "c_b", "d_k", "d_v")}

    pend_p = pend_s = None

    def project(x2d, pending, row_fn, tm, **kw):
        res = list(_project(x2d, mod[layer], row_fn, norm_g[layer], w_in, tm=tm, prev=pending, **kw))
        if pending is not None:
            x2d = res.pop()
        return x2d, res

    for layer in range(depth):
        i = layer // 2
        if layer % 2 == 0:
            lam_init = 0.8 - 0.6 * math.exp(-0.3 * layer)
            w4 = N_QCOL * LANES
            ar = np.arange
            cols = np.concatenate([perm, ar(w4 + 2 * LANES, 4 * w4 + 2 * LANES), ar(w4, w4 + 2 * LANES),
                                   qkv_w + perm, ar(qkv_w + w4, qkv_w + 2 * w4)])
            w_in = _reorder(ab_w_in[i], cols, 1).astype(BF16)
            rows = np.arange(ab_w_out.shape[1])
            rows[0:N_QCOL * LANES] = perm
            w_out = _reorder(ab_w_out[i], rows, 0).astype(BF16)
            diff_params = (b_lq1[i][None], b_lk1[i][None], b_lq2[i][None], b_lk2[i][None], b_norm_g[i][None])

            xp, (zq, zg) = project(xp, pend_p, row_p, tm_p, kind="ab", rope_tabs=None, norm_params=None,
                                   out_dtype=F32)
            zq3, zg3 = zq.reshape(bp, sp, -1), zg.reshape(bp, sp, -1)
            ya = _ctx_attention(zq3, zg3, q_blk=0, k_blk=PAIR_K_BLK, v_blk=PAIR_V_BLK, kv_per_col=False,
                                gate_blk=0, mode="pair", sink=a_sink[i])
            yb = _ctx_attention(zq3, zg3, q_blk=4, k_blk=8, v_blk=12, kv_per_col=True, gate_blk=4,
                                mode="diff", diff_params=diff_params, lam_init=lam_init)
            pend_p = (ya.reshape(bp * sp, -1), yb.reshape(bp * sp, -1), mod[layer], w_out)
            outs["a_k"].append(_unpair_kv(zq3, PAIR_K_BLK, 1, 2))
            outs["a_v"].append(_unpair_kv(zq3, PAIR_V_BLK, 1, 2))
            outs["b_k"].append(_unpair_kv(zq3, 8, 4, 4))
            outs["b_v"].append(_unpair_kv(zq3, 12, 4, 4))

            xs, (zq, zg, vt) = project(xs, pend_s, row_s, tm_s, kind="ab", rope_tabs=rope_tabs, norm_params=None,
                                       out_dtype=BF16, vt_blocks=(PAIR_V_BLK, 12, 13, 14, 15))
            zq3, zg3 = zq.reshape(bs, ss, -1), zg.reshape(bs, ss, -1)
            ya = _window_attention(zq3, zg3, vt, _pair_kv(cache_a_k[:, i]),
                                   _pair_kv(cache_a_v[:, i]).transpose(0, 2, 1), a_sink[i])
            yb = _keymajor_attention(zq3, zg3, vt, cache_b_k[:, i], cache_b_v[:, i].transpose(0, 1, 3, 2),
                                     q_blk=4, k_blk=8, vt_blk=1, kv_per_col=True, gate_blk=4, mode="diff",
                                     tq=TQ_DENSE, tk=TK_DENSE, diff_params=diff_params, lam_init=lam_init)
            pend_s = (ya.reshape(bs * ss, -1), yb.reshape(bs * ss, -1), mod[layer], w_out)
        else:
            cols = np.arange(cd_w_in.shape[2])
            cols[12 * LANES:16 * LANES] = 12 * LANES + perm
            cols[qkv_w + N_QCOL * LANES:qkv_w + 2 * N_QCOL * LANES] = qkv_w + N_QCOL * LANES + perm
            w_in = _reorder(cd_w_in[i], cols, 1).astype(BF16)
            rows = np.arange(cd_w_out.shape[1])
            rows[N_QCOL * LANES:] = N_QCOL * LANES + perm
            w_out = _reorder(cd_w_out[i], rows, 0).astype(BF16)
            norm_params = (jnp.tile(d_q_norm_g[i], 2)[None], jnp.tile(d_k_norm_g[i], 2)[None], bd)

            xp, (zq, zg) = project(xp, pend_p, row_p, tm_p, kind="cd", rope_tabs=None, norm_params=norm_params,
                                   out_dtype=F32)
            zq3, zg3 = zq.reshape(bp, sp, -1), zg.reshape(bp, sp, -1)
            yc, s_f, s_b = _retention(zq3, zg3, c_decay_f[i], c_decay_b[i], c_norm_g[i], None, True,
                                      RET_HEADS_CTX)
            yd = _ctx_attention(zq3, zg3, q_blk=12, k_blk=PAIR_K_BLK, v_blk=PAIR_V_BLK, kv_per_col=False,
                                gate_blk=4, mode="pair")
            pend_p = (yc.reshape(bp * sp, -1), yd.reshape(bp * sp, -1), mod[layer], w_out)
            outs["c_f"].append(s_f.astype(dt))
            outs["c_b"].append(s_b.astype(dt))
            outs["d_k"].append(_unpair_kv(zq3, 16, 1, 2))
            outs["d_v"].append(_unpair_kv(zq3, 17, 1, 2))

            xs, (zq, zg, vt) = project(xs, pend_s, row_s, tm_s, kind="cd", rope_tabs=rope_tabs,
                                       norm_params=norm_params, out_dtype=BF16, vt_blocks=(PAIR_V_BLK,))
            zq3, zg3 = zq.reshape(bs, ss, -1), zg.reshape(bs, ss, -1)
            yc = _retention(zq3, zg3, c_decay_f[i], c_decay_b[i], c_norm_g[i],
                            (state_c_fwd[:, i], state_c_bwd[:, i]), False, RET_HEADS_LATENT)[0]
            yd = _keymajor_attention(zq3, zg3, vt, _pair_kv(cache_d_k[:, i]),
                                     _pair_kv(cache_d_v[:, i]).transpose(0, 2, 1),
                                     q_blk=12, k_blk=PAIR_K_BLK, vt_blk=0, kv_per_col=False, gate_blk=4,
                                     mode="pair",
                                     tq=TQ_DENSE, tk=TK_DENSE)
            pend_s = (yc.reshape(bs * ss, -1), yd.reshape(bs * ss, -1), mod[layer], w_out)

    xp = _out_project(pend_p[0], pend_p[1], xp, pend_p[2], row_p, pend_p[3], final_g, tm=tm_p, final_norm=True)
    xs = _out_project(pend_s[0], pend_s[1], xs, pend_s[2], row_s, pend_s[3], final_g, tm=tm_s, final_norm=True)

    stack = lambda k: jnp.stack(outs[k], axis=1)
    return (xp.reshape(bp, sp, d), xs.reshape(bs, ss, d), stack("a_k"), stack("a_v"), stack("b_k"), stack("b_v"),
            stack("c_f"), stack("c_b"), stack("d_k"), stack("d_v"))
```

```python
import functools
import math

import numpy as np
import jax
import jax.numpy as jnp
from jax import lax
from jax.experimental import pallas as pl
from jax.experimental.pallas import tpu as pltpu

F32 = jnp.float32
BF16 = jnp.bfloat16

LANES = 128
HEAD = 64
GRID_W = 64
CHUNK = 128
WINDOW = 128
ROPE_THETA = 10000.0
EPS = 1e-6
NEG = -1e30
VMEM_LIMIT = 56 * 1024 * 1024
PROJ_SUBTILE = 512
TQ_DENSE = 2048
TK_DENSE = 512
SCORE_SLOTS = 4
WINDOW_QBLOCKS = 8
RET_HEADS_CTX = 4
RET_HEADS_LATENT = 1
RET_UNROLL_LOCAL = 16
RET_UNROLL_FINISH = 8
ONES_ROWS = 16
LOG2E = 1.4426950408889634

GQA_HEADS = 8
PAIR_ORDER = (0, 4, 1, 5, 2, 6, 3, 7)
N_QCOL = 4
C_HEADS = 4
C_DK = 128

N_QKV_BLK = 18
N_GATE_BLK = 8
PAIR_K_BLK, PAIR_V_BLK = 16, 17
AB_OPS = ("qrope",) * 8 + ("rope",) * 4 + ("plain",) * 4 + ("rope", "plain")
CD_OPS = ("plain",) * 4 + ("kscale",) * 4 + ("plain",) * 4 + ("qnorm",) * 4 + ("knorm", "plain")
PROJ_GROUPS = ((0, 4), (4, 8), (8, 12), (12, 16), (16, 18))


def _silu(x):
    return x / (1.0 + jnp.exp(-x))


def _lane_lo(shape):
    return lax.broadcasted_iota(jnp.int32, shape, len(shape) - 1) % LANES < HEAD


def _mod_kernel(cond_ref, w_ref, b_ref, o_ref):
    s = _silu(cond_ref[...])
    o_ref[0] = jnp.dot(s.astype(BF16), w_ref[0].astype(BF16), preferred_element_type=F32) + b_ref[0]


def _modulation(cond, mod_w, mod_b):
    depth, d, d3 = mod_w.shape
    rows = cond.shape[0]
    return pl.pallas_call(
        _mod_kernel,
        grid=(depth, d3 // d),
        in_specs=[pl.BlockSpec((rows, d), lambda l, j: (0, 0)),
                  pl.BlockSpec((1, d, d), lambda l, j: (l, 0, j)),
                  pl.BlockSpec((1, 1, d), lambda l, j: (l, 0, j))],
        out_specs=pl.BlockSpec((1, rows, d), lambda l, j: (l, 0, j)),
        out_shape=jax.ShapeDtypeStruct((depth, rows, d3), F32),
        compiler_params=pltpu.CompilerParams(vmem_limit_bytes=VMEM_LIMIT),
        name="modulation",
    )(cond, mod_w, mod_b.reshape(depth, 1, d3))


def _rot_half(x, first_half):
    return jnp.where(first_half, pltpu.roll(x, LANES - HEAD // 2, 1), pltpu.roll(x, HEAD // 2, 1))


def _head_sumsq(x, bd):
    sq = x * x
    hi = sq.astype(BF16)
    lo = (sq - hi.astype(F32)).astype(BF16)
    return jnp.dot(hi, bd, preferred_element_type=F32) + jnp.dot(lo, bd, preferred_element_type=F32)


def _proj_kernel(*refs, ops, groups, use_rope, has_norm, vt_blocks, has_prev):
    it = iter(refs)
    x_ref, mod_ref, g_ref, w_ref = next(it), next(it), next(it), next(it)
    cos_ref = sin_ref = qg_ref = kg_ref = bd_ref = vt_ref = None
    if has_prev:
        y1_ref, y2_ref, pmod_ref, pw_ref = next(it), next(it), next(it), next(it)
    if use_rope:
        cos_ref, sin_ref = next(it), next(it)
    if has_norm:
        qg_ref, kg_ref, bd_ref = next(it), next(it), next(it)
    zq_ref, zg_ref = next(it), next(it)
    if vt_blocks:
        vt_ref = next(it)
    if has_prev:
        xnew_ref = next(it)

    tm = x_ref.shape[0]
    sub = tm if has_norm else min(tm, PROJ_SUBTILE)
    q_scale = HEAD ** -0.5 * LOG2E
    first_half = lax.broadcasted_iota(jnp.int32, (sub, LANES), 1) % HEAD < HEAD // 2

    def head_norm(z, gain_ref):
        ss = _head_sumsq(z, bd_ref[...])
        return z * lax.rsqrt(ss * (1.0 / HEAD) + EPS) * gain_ref[...]

    for r0 in range(0, tm, sub):
        rows = slice(r0, r0 + sub)
        x = x_ref[rows, :]
        if has_prev:
            half = y1_ref.shape[1]
            y = (jnp.dot(y1_ref[rows, :], pw_ref[:half, :], preferred_element_type=F32)
                 + jnp.dot(y2_ref[rows, :], pw_ref[half:, :], preferred_element_type=F32))
            x = x + pmod_ref[0, 2:3, :] * y
            xnew_ref[rows, :] = x
        h = x * lax.rsqrt(jnp.mean(x * x, axis=-1, keepdims=True) + EPS) * g_ref[...]
        h = h * (1.0 + mod_ref[0, 1:2, :]) + mod_ref[0, 0:1, :]
        hb = h.astype(BF16)

        def rope(z):
            if not use_rope:
                return z
            return z * cos_ref[rows, :] + _rot_half(z, first_half) * sin_ref[rows, :]

        for b0, b1 in groups:
            z = jnp.dot(hb, w_ref[:, b0 * LANES:b1 * LANES], preferred_element_type=F32)
            for j in range(b0, b1):
                zz = z[:, (j - b0) * LANES:(j - b0 + 1) * LANES]
                op = ops[j]
                if op == "qrope":
                    zz = rope(zz * q_scale)
                elif op == "rope":
                    zz = rope(zz)
                elif op == "kscale":
                    zz = zz * (C_DK ** -0.5)
                elif op == "qnorm":
                    zz = rope(head_norm(zz, qg_ref)) * q_scale
                elif op == "knorm":
                    zz = rope(head_norm(zz, kg_ref))
                zq_ref[rows, j * LANES:(j + 1) * LANES] = zz.astype(zq_ref.dtype)
                if j in vt_blocks:
                    vt_ref[vt_blocks.index(j), :, rows] = zz.T.astype(vt_ref.dtype)
        n_half = N_GATE_BLK // 2
        for g0 in (0, n_half):
            c0 = (N_QKV_BLK + g0) * LANES
            z = jnp.dot(hb, w_ref[:, c0:c0 + n_half * LANES], preferred_element_type=F32)
            zg_ref[rows, g0 * LANES:(g0 + n_half) * LANES] = _silu(z)


def _project(x2d, mod, mod_row_of_tile, norm_g, w_bf16, *, tm, kind, rope_tabs, norm_params, out_dtype,
             vt_blocks=(), prev=None):
    t, d = x2d.shape
    out_specs = [pl.BlockSpec((tm, N_QKV_BLK * LANES), lambda i: (i, 0)),
                 pl.BlockSpec((tm, N_GATE_BLK * LANES), lambda i: (i, 0))]
    out_shape = [jax.ShapeDtypeStruct((t, N_QKV_BLK * LANES), out_dtype),
                 jax.ShapeDtypeStruct((t, N_GATE_BLK * LANES), F32)]
    if vt_blocks:
        out_specs.append(pl.BlockSpec((len(vt_blocks), LANES, tm), lambda i: (0, 0, i)))
        out_shape.append(jax.ShapeDtypeStruct((len(vt_blocks), LANES, t), out_dtype))
    if prev is not None:
        out_specs.append(pl.BlockSpec((tm, d), lambda i: (i, 0)))
        out_shape.append(jax.ShapeDtypeStruct((t, d), F32))
    use_rope = rope_tabs is not None
    has_norm = kind == "cd"
    ops, groups = (AB_OPS if kind == "ab" else CD_OPS), PROJ_GROUPS
    in_specs = [pl.BlockSpec((tm, d), lambda i: (i, 0)),
                pl.BlockSpec((1, 3, d), lambda i: (mod_row_of_tile(i), 0, 0)),
                pl.BlockSpec((1, d), lambda i: (0, 0)),
                pl.BlockSpec(w_bf16.shape, lambda i: (0, 0))]
    args = [x2d, mod, norm_g.reshape(1, d), w_bf16]
    if prev is not None:
        y1, y2, prev_mod, prev_w = prev
        in_specs += [pl.BlockSpec((tm, y1.shape[1]), lambda i: (i, 0)),
                     pl.BlockSpec((tm, y2.shape[1]), lambda i: (i, 0)),
                     pl.BlockSpec((1, 3, d), lambda i: (mod_row_of_tile(i), 0, 0)),
                     pl.BlockSpec(prev_w.shape, lambda i: (0, 0))]
        args += [y1, y2, prev_mod, prev_w]
    if use_rope:
        n_seq_tiles = rope_tabs[0].shape[0] // tm
        in_specs += [pl.BlockSpec((tm, LANES), lambda i: (i % n_seq_tiles, 0))] * 2
        args += list(rope_tabs)
    if has_norm:
        in_specs += [pl.BlockSpec((1, LANES), lambda i: (0, 0))] * 2 + [pl.BlockSpec((LANES, LANES), lambda i: (0, 0))]
        args += list(norm_params)
    return pl.pallas_call(
        functools.partial(_proj_kernel, ops=ops, groups=groups, use_rope=use_rope, has_norm=has_norm,
                          vt_blocks=tuple(vt_blocks), has_prev=prev is not None),
        grid=(t // tm,),
        in_specs=in_specs,
        out_specs=out_specs,
        out_shape=out_shape,
        compiler_params=pltpu.CompilerParams(vmem_limit_bytes=VMEM_LIMIT),
        name="project_" + kind,
    )(*args)


def _pair_rows(q):
    lo = jnp.where(_lane_lo((1, LANES)), 1.0, 0.0).astype(BF16)
    qb = q.astype(BF16)
    return jnp.concatenate([qb * lo, qb * (1.0 - lo).astype(BF16)], axis=0)


def _pair_cols(q):
    qt = q.astype(F32).T
    top = lax.broadcasted_iota(jnp.int32, qt.shape, 0) < HEAD
    return jnp.concatenate([jnp.where(top, qt, 0.0), jnp.where(top, 0.0, qt)], axis=1).astype(BF16)


def _ctx_attn_kernel(*refs, mode, has_sink, kv_per_col, lam_init):
    it = iter(refs)
    q_ref, k_ref, v_ref = next(it), next(it), next(it)
    sink_ref = next(it) if has_sink else None
    if mode == "diff":
        lq1_ref, lk1_ref, lq2_ref, lk2_ref, bg_ref = (next(it) for _ in range(5))
    gate_ref, y_ref = next(it), next(it)

    tq = q_ref.shape[1]
    lo = _lane_lo((tq, LANES))
    for c in range(N_QCOL):
        cols = slice(c * LANES, (c + 1) * LANES)
        kv_cols = cols if kv_per_col else slice(0, LANES)
        qrows = _pair_rows(q_ref[0, :, cols])
        s = lax.dot_general(qrows, k_ref[0, :, kv_cols].astype(BF16), (((1,), (1,)), ((), ())),
                            preferred_element_type=F32)
        v = v_ref[0, :, kv_cols].astype(BF16)
        halves = []
        for half, head in ((s[:tq], c), (s[tq:], c + N_QCOL)):
            m = jnp.max(half, axis=-1, keepdims=True)
            if has_sink:
                sk = sink_ref[head] * LOG2E
                m = jnp.maximum(m, sk)
            p = jnp.exp2(half - m)
            l = jnp.sum(p, axis=-1, keepdims=True)
            if has_sink:
                l = l + jnp.exp2(sk - m)
            halves.append(jnp.dot(p.astype(BF16), v, preferred_element_type=F32) / l)
        o2 = jnp.concatenate(halves, axis=0)
        if mode == "pair":
            o = jnp.where(lo, o2[:tq], o2[tq:])
        else:
            lam = (jnp.exp(jnp.sum(lq1_ref[...] * lk1_ref[...], axis=-1, keepdims=True))
                   - jnp.exp(jnp.sum(lq2_ref[...] * lk2_ref[...], axis=-1, keepdims=True)) + lam_init)
            o = o2[:tq] - lam * o2[tq:]
            o = o * lax.rsqrt(jnp.mean(o * o, axis=-1, keepdims=True) + EPS) * bg_ref[...] * (1.0 - lam_init)
        y_ref[0, :, cols] = (o * gate_ref[0, :, cols]).astype(y_ref.dtype)


def _ctx_attention(zq, zg, *, q_blk, k_blk, v_blk, kv_per_col, gate_blk, mode,
                   sink=None, diff_params=None, lam_init=0.0):
    b, n, _ = zq.shape
    w4 = N_QCOL * LANES
    kv_w = w4 if kv_per_col else LANES
    assert (q_blk * LANES) % w4 == 0 and (gate_blk * LANES) % w4 == 0
    assert (k_blk * LANES) % kv_w == 0 and (v_blk * LANES) % kv_w == 0
    in_specs = [pl.BlockSpec((1, n, w4), lambda bi: (bi, 0, q_blk * LANES // w4)),
                pl.BlockSpec((1, n, kv_w), lambda bi: (bi, 0, k_blk * LANES // kv_w)),
                pl.BlockSpec((1, n, kv_w), lambda bi: (bi, 0, v_blk * LANES // kv_w))]
    args = [zq, zq, zq]
    if sink is not None:
        in_specs.append(pl.BlockSpec(memory_space=pltpu.SMEM))
        args.append(sink)
    if mode == "diff":
        in_specs += [pl.BlockSpec((1, HEAD), lambda bi: (0, 0))] * 4
        in_specs.append(pl.BlockSpec((1, LANES), lambda bi: (0, 0)))
        args += list(diff_params)
    in_specs.append(pl.BlockSpec((1, n, w4), lambda bi: (bi, 0, gate_blk * LANES // w4)))
    args.append(zg)
    return pl.pallas_call(
        functools.partial(_ctx_attn_kernel, mode=mode, has_sink=sink is not None, kv_per_col=kv_per_col,
                          lam_init=lam_init),
        grid=(b,),
        in_specs=in_specs,
        out_specs=pl.BlockSpec((1, n, w4), lambda bi: (bi, 0, 0)),
        out_shape=jax.ShapeDtypeStruct((b, n, w4), BF16),
        compiler_params=pltpu.CompilerParams(vmem_limit_bytes=VMEM_LIMIT),
        name="ctx_attention_" + mode,
    )(*args)


def _keymajor_attn_kernel(*refs, mode, tk, lam_init):
    it = iter(refs)
    q_ref, k_ref, vt_ref, kx_ref, vxt_ref = (next(it) for _ in range(5))
    if mode == "diff":
        lq1_ref, lk1_ref, lq2_ref, lk2_ref, bg_ref = (next(it) for _ in range(5))
    gate_ref, y_ref, s_scr, sx_scr = next(it), next(it), next(it), next(it)

    tq = q_ref.shape[1]
    qt = _pair_cols(q_ref[0])

    def scores(k, slot):
        s = jnp.dot(k, qt, preferred_element_type=F32)
        slot[...] = s
        return jnp.max(s, axis=0, keepdims=True)

    vd = HEAD if mode == "pair" else LANES

    def with_ones(vt):
        return jnp.concatenate([vt, jnp.ones((ONES_ROWS, vt.shape[1]), BF16)], axis=0)

    def consume(slot, cmax, vt, m, acc_lo, acc_hi):
        m_new = jnp.maximum(m, cmax)
        alpha = jnp.exp2(m - m_new)
        p = jnp.exp2(slot[...] - m_new).astype(BF16)
        if mode == "pair":
            v_lo, v_hi = with_ones(vt[:HEAD]), with_ones(vt[HEAD:])
        else:
            v_lo = v_hi = with_ones(vt)
        acc_lo = alpha[:, :tq] * acc_lo + jnp.dot(v_lo, p[:, :tq], preferred_element_type=F32)
        acc_hi = alpha[:, tq:] * acc_hi + jnp.dot(v_hi, p[:, tq:], preferred_element_type=F32)
        return m_new, acc_lo, acc_hi

    n_chunks = k_ref.shape[1] // tk
    carry = (jnp.full((1, 2 * tq), NEG, F32), jnp.zeros((vd + ONES_ROWS, tq), F32),
             jnp.zeros((vd + ONES_ROWS, tq), F32))
    n_slots = s_scr.shape[0]
    ahead = n_slots - 1
    chunk_scores = lambda j: scores(k_ref[0, j * tk:(j + 1) * tk, :], s_scr.at[j % n_slots])
    cmax = [chunk_scores(j) for j in range(min(ahead, n_chunks))]
    cmax_x = scores(kx_ref[0].astype(BF16), sx_scr)
    for j in range(n_chunks):
        if j + ahead < n_chunks:
            cmax.append(chunk_scores(j + ahead))
        carry = consume(s_scr.at[j % n_slots], cmax[j], vt_ref[0, :, j * tk:(j + 1) * tk], *carry)
    _, acc_lo, acc_hi = consume(sx_scr, cmax_x, vxt_ref[0].astype(BF16), *carry)

    ot_lo = acc_lo[:vd] / acc_lo[vd:vd + 1]
    ot_hi = acc_hi[:vd] / acc_hi[vd:vd + 1]
    if mode == "pair":
        o = jnp.concatenate([ot_lo, ot_hi], axis=0).T
    else:
        lam = (jnp.exp(jnp.sum(lq1_ref[...] * lk1_ref[...], axis=-1, keepdims=True))
               - jnp.exp(jnp.sum(lq2_ref[...] * lk2_ref[...], axis=-1, keepdims=True)) + lam_init)
        o = (ot_lo - lam * ot_hi).T
        o = o * lax.rsqrt(jnp.mean(o * o, axis=-1, keepdims=True) + EPS) * bg_ref[...] * (1.0 - lam_init)
    y_ref[0] = (o * gate_ref[0]).astype(y_ref.dtype)


def _keymajor_attention(zq, zg, vt, kx, vxt, *, q_blk, k_blk, vt_blk, kv_per_col, gate_blk, mode, tq, tk,
                        diff_params=None, lam_init=0.0):
    b, n, _ = zq.shape
    per = (lambda c: c) if kv_per_col else (lambda c: 0)
    if kx.ndim == 4:
        kx_spec = pl.BlockSpec((None, 1) + kx.shape[2:], lambda bi, c, qi: (bi, c, 0, 0))
        vx_spec = pl.BlockSpec((None, 1) + vxt.shape[2:], lambda bi, c, qi: (bi, c, 0, 0))
    else:
        kx_spec = pl.BlockSpec((1,) + kx.shape[1:], lambda bi, c, qi: (bi, 0, 0))
        vx_spec = pl.BlockSpec((1,) + vxt.shape[1:], lambda bi, c, qi: (bi, 0, 0))
    in_specs = [pl.BlockSpec((1, tq, LANES), lambda bi, c, qi: (bi, qi, q_blk + c)),
                pl.BlockSpec((1, n, LANES), lambda bi, c, qi: (bi, 0, k_blk + per(c))),
                pl.BlockSpec((1, LANES, n), lambda bi, c, qi: (vt_blk + per(c), 0, bi)),
                kx_spec, vx_spec]
    args = [zq, zq, vt, kx, vxt]
    if mode == "diff":
        in_specs += [pl.BlockSpec((1, HEAD), lambda bi, c, qi: (0, 0))] * 4
        in_specs.append(pl.BlockSpec((1, LANES), lambda bi, c, qi: (0, 0)))
        args += list(diff_params)
    in_specs.append(pl.BlockSpec((1, tq, LANES), lambda bi, c, qi: (bi, qi, gate_blk + c)))
    args.append(zg)
    return pl.pallas_call(
        functools.partial(_keymajor_attn_kernel, mode=mode, tk=tk, lam_init=lam_init),
        grid=(b, N_QCOL, n // tq),
        in_specs=in_specs,
        out_specs=pl.BlockSpec((1, tq, LANES), lambda bi, c, qi: (bi, qi, c)),
        out_shape=jax.ShapeDtypeStruct((b, n, N_QCOL * LANES), BF16),
        scratch_shapes=[pltpu.VMEM((SCORE_SLOTS, tk, 2 * tq), F32), pltpu.VMEM((kx.shape[-2], 2 * tq), F32)],
        compiler_params=pltpu.CompilerParams(vmem_limit_bytes=VMEM_LIMIT),
        name="attention_keymajor_" + mode,
    )(*args)


def _window_attn_kernel(q_ref, k_ref, vt_ref, kx_ref, vxt_ref, sink_ref, gate_ref, y_ref, s_scr):
    n = k_ref.shape[1]
    span = 3 * CHUNK
    sk = jnp.concatenate([jnp.full((1, CHUNK), sink_ref[h] * LOG2E, F32) for h in range(GQA_HEADS)], axis=1)
    kx = kx_ref[0].astype(BF16)
    vxt = vxt_ref[0].astype(BF16)
    half = N_QCOL * CHUNK

    def with_ones(vt):
        return jnp.concatenate([vt, jnp.ones((ONES_ROWS, vt.shape[1]), BF16)], axis=0)

    starts, maxes = [], []
    for sub in range(WINDOW_QBLOCKS):
        blk = pl.program_id(1) * WINDOW_QBLOCKS + sub
        qts = [_pair_cols(q_ref[0, sub * CHUNK:(sub + 1) * CHUNK, c * LANES:(c + 1) * LANES])
               for c in range(N_QCOL)]
        qt = jnp.concatenate([t[:, :CHUNK] for t in qts] + [t[:, CHUNK:] for t in qts], axis=1)
        w0 = pl.multiple_of(jnp.clip((blk - 1) * CHUNK, 0, n - span), CHUNK)
        s_w = jnp.dot(k_ref[0, pl.ds(w0, span), :], qt, preferred_element_type=F32)
        s_x = jnp.dot(kx, qt, preferred_element_type=F32)
        dist = (w0 - blk * CHUNK + lax.broadcasted_iota(jnp.int32, (span, CHUNK), 0)
                - lax.broadcasted_iota(jnp.int32, (span, CHUNK), 1))
        bias = jnp.where((dist >= -WINDOW) & (dist <= WINDOW), 0.0, NEG)
        s_w = s_w + jnp.concatenate([bias] * GQA_HEADS, axis=1)
        s_scr[sub, :span] = s_w
        s_scr[sub, span:] = s_x
        starts.append(w0)
        maxes.append(jnp.maximum(jnp.maximum(jnp.max(s_w, axis=0, keepdims=True),
                                             jnp.max(s_x, axis=0, keepdims=True)), sk))

    for sub in range(WINDOW_QBLOCKS):
        w0, m = starts[sub], maxes[sub]
        p = jnp.exp2(s_scr[sub] - m).astype(BF16)
        vt = jnp.concatenate([vt_ref[0, :, pl.ds(w0, span)], vxt], axis=1)
        acc_lo = jnp.dot(with_ones(vt[:HEAD]), p[:, :half], preferred_element_type=F32)
        acc_hi = jnp.dot(with_ones(vt[HEAD:]), p[:, half:], preferred_element_type=F32)
        sink_p = jnp.exp2(sk - m)
        ot_lo = acc_lo[:HEAD] / (acc_lo[HEAD:HEAD + 1] + sink_p[:, :half])
        ot_hi = acc_hi[:HEAD] / (acc_hi[HEAD:HEAD + 1] + sink_p[:, half:])
        o = jnp.concatenate(
            [jnp.concatenate([ot_lo[:, c * CHUNK:(c + 1) * CHUNK], ot_hi[:, c * CHUNK:(c + 1) * CHUNK]], axis=0).T
             for c in range(N_QCOL)], axis=1)
        rows = slice(sub * CHUNK, (sub + 1) * CHUNK)
        y_ref[0, rows, :] = (o * gate_ref[0, rows, :]).astype(y_ref.dtype)


def _window_attention(zq, zg, vt, kx, vxt, sink):
    b, n, _ = zq.shape
    w4 = N_QCOL * LANES
    tq = WINDOW_QBLOCKS * CHUNK
    return pl.pallas_call(
        _window_attn_kernel,
        grid=(b, n // tq),
        in_specs=[pl.BlockSpec((1, tq, w4), lambda bi, i: (bi, i, 0)),
                  pl.BlockSpec((1, n, LANES), lambda bi, i: (bi, 0, PAIR_K_BLK)),
                  pl.BlockSpec((1, LANES, n), lambda bi, i: (0, 0, bi)),
                  pl.BlockSpec((1,) + kx.shape[1:], lambda bi, i: (bi, 0, 0)),
                  pl.BlockSpec((1,) + vxt.shape[1:], lambda bi, i: (bi, 0, 0)),
                  pl.BlockSpec(memory_space=pltpu.SMEM),
                  pl.BlockSpec((1, tq, w4), lambda bi, i: (bi, i, 0))],
        out_specs=pl.BlockSpec((1, tq, w4), lambda bi, i: (bi, i, 0)),
        out_shape=jax.ShapeDtypeStruct((b, n, w4), BF16),
        scratch_shapes=[pltpu.VMEM((WINDOW_QBLOCKS, 3 * CHUNK + kx.shape[1], GQA_HEADS * CHUNK), F32)],
        compiler_params=pltpu.CompilerParams(vmem_limit_bytes=VMEM_LIMIT),
        name="attention_window",
    )(zq, zq, vt, kx, vxt, sink, zg)


def _retention_kernel(*refs, has_state_in, emit_state):
    it = iter(refs)
    q_ref, k_ref, v_ref, decf_ref, decb_ref, cg_ref, gate_ref = (next(it) for _ in range(7))
    sf_in = sb_in = sf_out = sb_out = None
    if has_state_in:
        sf_in, sb_in = next(it), next(it)
    y_ref = next(it)
    if emit_state:
        sf_out, sb_out = next(it), next(it)
    o_scr, u_scr, dm_scr = next(it), next(it), next(it)

    nc = q_ref.shape[1] // CHUNK
    hps = q_ref.shape[2] // LANES
    unroll_local = max(1, min(nc, RET_UNROLL_LOCAL // hps))
    unroll_finish = max(1, min(nc, RET_UNROLL_FINISH // hps))
    ri = lax.broadcasted_iota(jnp.int32, (CHUNK, CHUNK), 0).astype(F32)
    ci = lax.broadcasted_iota(jnp.int32, (CHUNK, CHUNK), 1).astype(F32)
    tok_col = lax.broadcasted_iota(jnp.int32, (CHUNK, 1), 0).astype(F32)
    tok_row = lax.broadcasted_iota(jnp.int32, (1, CHUNK), 1).astype(F32)
    rel_f, rel_b = ri - ci, ci - ri - 1.0

    cross_f, cross_b, kdec_f, kdec_b, cdec_f, cdec_b = [], [], [], [], [], []
    for h in range(hps):
        lg_f = -jnp.exp(decf_ref[h][:, :1])
        lg_b = -jnp.exp(decb_ref[h][:, :1])
        cross_f.append(jnp.exp(lg_f * (tok_col + 1.0)))
        cross_b.append(jnp.exp(lg_b * (CHUNK - 1.0 - tok_col)))
        kdec_f.append(jnp.exp(lg_f * (CHUNK - 1.0 - tok_row)))
        kdec_b.append(jnp.exp(lg_b * tok_row))
        cdec_f.append(jnp.exp(lg_f * CHUNK))
        cdec_b.append(jnp.exp(lg_b * CHUNK))
        dm_scr[h] = jnp.where(rel_f >= 0.0, jnp.exp(lg_f * jnp.maximum(rel_f, 0.0)),
                              jnp.exp(lg_b * jnp.maximum(rel_b, 0.0)))

    def chunk_rows(c):
        return pl.ds(pl.multiple_of(c * CHUNK, CHUNK), CHUNK)

    def head_cols(h):
        return slice(h * LANES, (h + 1) * LANES)

    def local_body(c, _):
        rows = chunk_rows(c)
        for h in range(hps):
            qh = q_ref[0, rows, head_cols(h)].astype(BF16)
            kf = k_ref[0, rows, head_cols(h)].astype(F32)
            vh = v_ref[0, rows, head_cols(h)].astype(BF16)
            kt = kf.T
            att = jnp.dot(qh, kt.astype(BF16), preferred_element_type=F32) * dm_scr[h]
            lhs = jnp.concatenate([att.astype(BF16), (kt * kdec_f[h]).astype(BF16), (kt * kdec_b[h]).astype(BF16)],
                                  axis=0)
            r = jnp.dot(lhs, vh, preferred_element_type=F32)
            o_scr[rows, head_cols(h)] = r[:CHUNK]
            u_scr[h, c, 0] = r[CHUNK:2 * CHUNK]
            u_scr[h, c, 1] = r[2 * CHUNK:]
        return 0
    lax.fori_loop(0, nc, local_body, 0, unroll=unroll_local)

    def scan_body(t, states):
        out = []
        for h in range(hps):
            s_f, s_b = states[2 * h], states[2 * h + 1]
            inc_f, inc_b = u_scr[h, t, 0], u_scr[h, nc - 1 - t, 1]
            u_scr[h, t, 0] = s_f
            u_scr[h, nc - 1 - t, 1] = s_b
            out += [cdec_f[h] * s_f + inc_f, cdec_b[h] * s_b + inc_b]
        return tuple(out)
    zero = jnp.zeros((CHUNK, CHUNK), F32)
    init = tuple(x for h in range(hps) for x in ((sf_in[0, h], sb_in[0, h]) if has_state_in else (zero, zero)))
    final = lax.fori_loop(0, nc, scan_body, init, unroll=2 if hps == 1 else 1)
    if emit_state:
        for h in range(hps):
            sf_out[0, h] = final[2 * h].astype(sf_out.dtype)
            sb_out[0, h] = final[2 * h + 1].astype(sb_out.dtype)

    def finish_body(c, _):
        rows = chunk_rows(c)
        for h in range(hps):
            qh = q_ref[0, rows, head_cols(h)].astype(BF16)
            states = jnp.concatenate([u_scr[h, c, 0], u_scr[h, c, 1]], axis=1).astype(BF16)
            r = jnp.dot(qh, states, preferred_element_type=F32)
            o = o_scr[rows, head_cols(h)] + cross_f[h] * r[:, :LANES] + cross_b[h] * r[:, LANES:]
            mu = jnp.mean(o, axis=-1, keepdims=True)
            d = o - mu
            var = jnp.mean(d * d, axis=-1, keepdims=True)
            y = d * lax.rsqrt(var + EPS) * cg_ref[h]
            y_ref[0, rows, head_cols(h)] = (y * gate_ref[0, rows, head_cols(h)]).astype(y_ref.dtype)
        return 0
    lax.fori_loop(0, nc, finish_body, 0, unroll=unroll_finish)


def _retention(zq, zg, dec_f, dec_b, c_norm_g, states, emit_state, hps):
    b, n, _ = zq.shape
    w = hps * LANES
    seq_spec = lambda blk: pl.BlockSpec((1, n, w), lambda bi, g: (bi, 0, blk // hps + g))
    st_spec = pl.BlockSpec((1, hps, CHUNK, CHUNK), lambda bi, g: (bi, g, 0, 0))
    head_spec = pl.BlockSpec((hps, 1, LANES), lambda bi, g: (g, 0, 0))
    bcast = lambda p: jnp.broadcast_to(p.astype(F32)[:, None, None], (C_HEADS, 1, LANES))
    in_specs = [seq_spec(0), seq_spec(C_HEADS), seq_spec(2 * C_HEADS), head_spec, head_spec, head_spec,
                seq_spec(0)]
    args = [zq, zq, zq, bcast(dec_f), bcast(dec_b), c_norm_g.astype(F32)[:, None, :], zg]
    if states is not None:
        in_specs += [st_spec, st_spec]
        args += list(states)
    out_specs = [seq_spec(0)]
    out_shape = [jax.ShapeDtypeStruct((b, n, C_HEADS * LANES), BF16)]
    if emit_state:
        out_specs += [st_spec, st_spec]
        out_shape += [jax.ShapeDtypeStruct((b, C_HEADS, CHUNK, CHUNK), F32)] * 2
    return pl.pallas_call(
        functools.partial(_retention_kernel, has_state_in=states is not None, emit_state=emit_state),
        grid=(b, C_HEADS // hps),
        in_specs=in_specs,
        out_specs=out_specs,
        out_shape=out_shape,
        scratch_shapes=[pltpu.VMEM((n, w), F32),
                        pltpu.VMEM((hps, n // CHUNK, 2, CHUNK, CHUNK), F32),
                        pltpu.VMEM((hps, CHUNK, CHUNK), F32)],
        compiler_params=pltpu.CompilerParams(vmem_limit_bytes=VMEM_LIMIT),
        name="retention",
    )(*args)


def _out_kernel(y1_ref, y2_ref, x_ref, mod_ref, w_ref, fg_ref, o_ref, *, final_norm):
    half = y1_ref.shape[1]
    y = (jnp.dot(y1_ref[...], w_ref[:half, :], preferred_element_type=F32)
         + jnp.dot(y2_ref[...], w_ref[half:, :], preferred_element_type=F32))
    x = x_ref[...] + mod_ref[0, 2:3, :] * y
    if final_norm:
        x = x * lax.rsqrt(jnp.mean(x * x, axis=-1, keepdims=True) + EPS) * fg_ref[...]
    o_ref[...] = x


def _out_project(y1, y2, x2d, mod, mod_row_of_tile, w_bf16, final_g, *, tm, final_norm):
    t, d = x2d.shape
    half = y1.shape[1]
    return pl.pallas_call(
        functools.partial(_out_kernel, final_norm=final_norm),
        grid=(t // tm,),
        in_specs=[pl.BlockSpec((tm, half), lambda i: (i, 0)),
                  pl.BlockSpec((tm, half), lambda i: (i, 0)),
                  pl.BlockSpec((tm, d), lambda i: (i, 0)),
                  pl.BlockSpec((1, 3, d), lambda i: (mod_row_of_tile(i), 0, 0)),
                  pl.BlockSpec(w_bf16.shape, lambda i: (0, 0)),
                  pl.BlockSpec((1, d), lambda i: (0, 0))],
        out_specs=pl.BlockSpec((tm, d), lambda i: (i, 0)),
        out_shape=jax.ShapeDtypeStruct((t, d), F32),
        compiler_params=pltpu.CompilerParams(vmem_limit_bytes=VMEM_LIMIT),
        name="out_project",
    )(y1, y2, x2d, mod, w_bf16, final_g.reshape(1, d))


def _pair_perm():
    return np.concatenate([np.arange(h * HEAD, (h + 1) * HEAD) for h in PAIR_ORDER])


def _reorder(w, idx, axis):
    idx = np.asarray(idx)
    runs = np.split(idx, np.flatnonzero(np.diff(idx) != 1) + 1)
    return jnp.concatenate([lax.slice_in_dim(w, int(r[0]), int(r[-1]) + 1, axis=axis) for r in runs], axis=axis)


def _rope_tables(n):
    rows = n // GRID_W
    row = jnp.repeat(jnp.arange(rows, dtype=F32), GRID_W)
    col = jnp.tile(jnp.arange(GRID_W, dtype=F32), rows)
    nf = HEAD // 4
    inv = ROPE_THETA ** (-jnp.arange(nf, dtype=F32) / nf)
    ang = jnp.concatenate([row[:, None] * inv, col[:, None] * inv], axis=-1)
    cos, sin = jnp.cos(ang), jnp.sin(ang)
    return jnp.tile(cos, (1, 4)), jnp.tile(jnp.concatenate([-sin, sin], axis=-1), (1, 2))


def _pair_kv(cache):
    b, g, p, d = cache.shape
    return cache.transpose(0, 2, 1, 3).reshape(b, p, g * d)


def _unpair_kv(z, blk, n_blk, heads):
    b, p, _ = z.shape
    t = z[:, :, blk * LANES:(blk + n_blk) * LANES]
    return t.reshape(b, p, heads, t.shape[-1] // heads).transpose(0, 2, 1, 3)


def kernel(x_prompt, x_sample, cache_a_k, cache_a_v, cache_b_k, cache_b_v, state_c_fwd, state_c_bwd, cache_d_k, cache_d_v, c, c_ctx, norm_g, mod_w, mod_b, ab_w_in, ab_w_out, a_sink, b_lq1, b_lk1, b_lq2, b_lk2, b_norm_g, cd_w_in, cd_w_out, c_decay_f, c_decay_b, c_norm_g, d_q_norm_g, d_k_norm_g, final_g):
    depth = norm_g.shape[0]
    bp, sp, d = x_prompt.shape
    bs, ss, _ = x_sample.shape
    dt = x_prompt.dtype

    ctx_row = bs
    pad = (-(bs + 1)) % 8
    cond = jnp.concatenate([c, c_ctx[None, :], jnp.zeros((pad, d), c.dtype)], axis=0)
    mod = _modulation(cond, mod_w, mod_b).reshape(depth, cond.shape[0], 3, d)

    perm = _pair_perm()
    qkv_w = N_QKV_BLK * LANES
    rope_tabs = _rope_tables(ss)
    bd = jnp.asarray(np.kron(np.eye(LANES // HEAD), np.ones((HEAD, HEAD))), BF16)

    tm_s = 1024
    tm_p = 512
    tiles_per_seq = ss // tm_s
    row_s = lambda i: i // tiles_per_seq
    row_p = lambda i: ctx_row

    xp = x_prompt.reshape(bp * sp, d)
    xs = x_sample.reshape(bs * ss, d)
    outs = {k: [] for k in ("a_k", "a_v", "b_k", "b_v", "c_f", "c_b", "d_k", "d_v")}

    pend_p = pend_s = None

    def project(x2d, pending, row_fn, tm, **kw):
        res = list(_project(x2d, mod[layer], row_fn, norm_g[layer], w_in, tm=tm, prev=pending, **kw))
        if pending is not None:
            x2d = res.pop()
        return x2d, res

    for layer in range(depth):
        i = layer // 2
        if layer % 2 == 0:
            lam_init = 0.8 - 0.6 * math.exp(-0.3 * layer)
            w4 = N_QCOL * LANES
            ar = np.arange
            cols = np.concatenate([perm, ar(w4 + 2 * LANES, 4 * w4 + 2 * LANES), ar(w4, w4 + 2 * LANES),
                                   qkv_w + perm, ar(qkv_w + w4, qkv_w + 2 * w4)])
            w_in = _reorder(ab_w_in[i], cols, 1).astype(BF16)
            rows = np.arange(ab_w_out.shape[1])
            rows[0:N_QCOL * LANES] = perm
            w_out = _reorder(ab_w_out[i], rows, 0).astype(BF16)
            diff_params = (b_lq1[i][None], b_lk1[i][None], b_lq2[i][None], b_lk2[i][None], b_norm_g[i][None])

            xp, (zq, zg) = project(xp, pend_p, row_p, tm_p, kind="ab", rope_tabs=None, norm_params=None,
                                   out_dtype=F32)
            zq3, zg3 = zq.reshape(bp, sp, -1), zg.reshape(bp, sp, -1)
            ya = _ctx_attention(zq3, zg3, q_blk=0, k_blk=PAIR_K_BLK, v_blk=PAIR_V_BLK, kv_per_col=False,
                                gate_blk=0, mode="pair", sink=a_sink[i])
            yb = _ctx_attention(zq3, zg3, q_blk=4, k_blk=8, v_blk=12, kv_per_col=True, gate_blk=4,
                                mode="diff", diff_params=diff_params, lam_init=lam_init)
            pend_p = (ya.reshape(bp * sp, -1), yb.reshape(bp * sp, -1), mod[layer], w_out)
            outs["a_k"].append(_unpair_kv(zq3, PAIR_K_BLK, 1, 2))
            outs["a_v"].append(_unpair_kv(zq3, PAIR_V_BLK, 1, 2))
            outs["b_k"].append(_unpair_kv(zq3, 8, 4, 4))
            outs["b_v"].append(_unpair_kv(zq3, 12, 4, 4))

            xs, (zq, zg, vt) = project(xs, pend_s, row_s, tm_s, kind="ab", rope_tabs=rope_tabs, norm_params=None,
                                       out_dtype=BF16, vt_blocks=(PAIR_V_BLK, 12, 13, 14, 15))
            zq3, zg3 = zq.reshape(bs, ss, -1), zg.reshape(bs, ss, -1)
            ya = _window_attention(zq3, zg3, vt, _pair_kv(cache_a_k[:, i]),
                                   _pair_kv(cache_a_v[:, i]).transpose(0, 2, 1), a_sink[i])
            yb = _keymajor_attention(zq3, zg3, vt, cache_b_k[:, i], cache_b_v[:, i].transpose(0, 1, 3, 2),
                                     q_blk=4, k_blk=8, vt_blk=1, kv_per_col=True, gate_blk=4, mode="diff",
                                     tq=TQ_DENSE, tk=TK_DENSE, diff_params=diff_params, lam_init=lam_init)
            pend_s = (ya.reshape(bs * ss, -1), yb.reshape(bs * ss, -1), mod[layer], w_out)
        else:
            cols = np.arange(cd_w_in.shape[2])
            cols[12 * LANES:16 * LANES] = 12 * LANES + perm
            cols[qkv_w + N_QCOL * LANES:qkv_w + 2 * N_QCOL * LANES] = qkv_w + N_QCOL * LANES + perm
            w_in = _reorder(cd_w_in[i], cols, 1).astype(BF16)
            rows = np.arange(cd_w_out.shape[1])
            rows[N_QCOL * LANES:] = N_QCOL * LANES + perm
            w_out = _reorder(cd_w_out[i], rows, 0).astype(BF16)
            norm_params = (jnp.tile(d_q_norm_g[i], 2)[None], jnp.tile(d_k_norm_g[i], 2)[None], bd)

            xp, (zq, zg) = project(xp, pend_p, row_p, tm_p, kind="cd", rope_tabs=None, norm_params=norm_params,
                                   out_dtype=F32)
            zq3, zg3 = zq.reshape(bp, sp, -1), zg.reshape(bp, sp, -1)
            yc, s_f, s_b = _retention(zq3, zg3, c_decay_f[i], c_decay_b[i], c_norm_g[i], None, True,
                                      RET_HEADS_CTX)
            yd = _ctx_attention(zq3, zg3, q_blk=12, k_blk=PAIR_K_BLK, v_blk=PAIR_V_BLK, kv_per_col=False,
                                gate_blk=4, mode="pair")
            pend_p = (yc.reshape(bp * sp, -1), yd.reshape(bp * sp, -1), mod[layer], w_out)
            outs["c_f"].append(s_f.astype(dt))
            outs["c_b"].append(s_b.astype(dt))
            outs["d_k"].append(_unpair_kv(zq3, 16, 1, 2))
            outs["d_v"].append(_unpair_kv(zq3, 17, 1, 2))

            xs, (zq, zg, vt) = project(xs, pend_s, row_s, tm_s, kind="cd", rope_tabs=rope_tabs,
                                       norm_params=norm_params, out_dtype=BF16, vt_blocks=(PAIR_V_BLK,))
            zq3, zg3 = zq.reshape(bs, ss, -1), zg.reshape(bs, ss, -1)
            yc = _retention(zq3, zg3, c_decay_f[i], c_decay_b[i], c_norm_g[i],
                            (state_c_fwd[:, i], state_c_bwd[:, i]), False, RET_HEADS_LATENT)[0]
            yd = _keymajor_attention(zq3, zg3, vt, _pair_kv(cache_d_k[:, i]),
                                     _pair_kv(cache_d_v[:, i]).transpose(0, 2, 1),
                                     q_blk=12, k_blk=PAIR_K_BLK, vt_blk=0, kv_per_col=False, gate_blk=4,
                                     mode="pair",
                                     tq=TQ_DENSE, tk=TK_DENSE)
            pend_s = (yc.reshape(bs * ss, -1), yd.reshape(bs * ss, -1), mod[layer], w_out)

    xp = _out_project(pend_p[0], pend_p[1], xp, pend_p[2], row_p, pend_p[3], final_g, tm=tm_p, final_norm=True)
    xs = _out_project(pend_s[0], pend_s[1], xs, pend_s[2], row_s, pend_s[3], final_g, tm=tm_s, final_norm=True)

    stack = lambda k: jnp.stack(outs[k], axis=1)
    return (xp.reshape(bp, sp, d), xs.reshape(bs, ss, d), stack("a_k"), stack("a_v"), stack("b_k"), stack("b_v"),
            stack("c_f"), stack("c_b"), stack("d_k"), stack("d_v"))
```

```python
import functools
import math

import numpy as np
import jax
import jax.numpy as jnp
from jax import lax
from jax.experimental import pallas as pl
from jax.experimental.pallas import tpu as pltpu

F32 = jnp.float32
BF16 = jnp.bfloat16

LANES = 128
HEAD = 64
GRID_W = 64
CHUNK = 128
WINDOW = 128
ROPE_THETA = 10000.0
EPS = 1e-6
NEG = -1e30
VMEM_LIMIT = 56 * 1024 * 1024
PROJ_SUBTILE = 512
TQ_DENSE = 2048
TK_DENSE = 512
SCORE_SLOTS = 4
WINDOW_QBLOCKS = 8
RET_HEADS_CTX = 4
RET_HEADS_LATENT = 1
RET_UNROLL_LOCAL = 16
RET_UNROLL_FINISH = 8
ONES_ROWS = 16
LOG2E = 1.4426950408889634

GQA_HEADS = 8
PAIR_ORDER = (0, 4, 1, 5, 2, 6, 3, 7)
N_QCOL = 4
C_HEADS = 4
C_DK = 128

N_QKV_BLK = 18
N_GATE_BLK = 8
PAIR_K_BLK, PAIR_V_BLK = 16, 17
AB_OPS = ("qrope",) * 8 + ("rope",) * 4 + ("plain",) * 4 + ("rope", "plain")
CD_OPS = ("plain",) * 4 + ("kscale",) * 4 + ("plain",) * 4 + ("qnorm",) * 4 + ("knorm", "plain")
PROJ_GROUPS = ((0, 4), (4, 8), (8, 12), (12, 16), (16, 18))


def _silu(x):
    return x / (1.0 + jnp.exp(-x))


def _lane_lo(shape):
    return lax.broadcasted_iota(jnp.int32, shape, len(shape) - 1) % LANES < HEAD


def _mod_kernel(cond_ref, w_ref, b_ref, o_ref):
    s = _silu(cond_ref[...])
    o_ref[0] = jnp.dot(s.astype(BF16), w_ref[0].astype(BF16), preferred_element_type=F32) + b_ref[0]


def _modulation(cond, mod_w, mod_b):
    depth, d, d3 = mod_w.shape
    rows = cond.shape[0]
    return pl.pallas_call(
        _mod_kernel,
        grid=(depth, d3 // d),
        in_specs=[pl.BlockSpec((rows, d), lambda l, j: (0, 0)),
                  pl.BlockSpec((1, d, d), lambda l, j: (l, 0, j)),
                  pl.BlockSpec((1, 1, d), lambda l, j: (l, 0, j))],
        out_specs=pl.BlockSpec((1, rows, d), lambda l, j: (l, 0, j)),
        out_shape=jax.ShapeDtypeStruct((depth, rows, d3), F32),
        compiler_params=pltpu.CompilerParams(vmem_limit_bytes=VMEM_LIMIT),
        name="modulation",
    )(cond, mod_w, mod_b.reshape(depth, 1, d3))


def _rot_half(x, first_half):
    return jnp.where(first_half, pltpu.roll(x, LANES - HEAD // 2, 1), pltpu.roll(x, HEAD // 2, 1))


def _head_sumsq(x, bd):
    sq = x * x
    hi = sq.astype(BF16)
    lo = (sq - hi.astype(F32)).astype(BF16)
    return jnp.dot(hi, bd, preferred_element_type=F32) + jnp.dot(lo, bd, preferred_element_type=F32)


def _proj_kernel(*refs, ops, groups, use_rope, has_norm, vt_blocks, has_prev):
    it = iter(refs)
    x_ref, mod_ref, g_ref, w_ref = next(it), next(it), next(it), next(it)
    cos_ref = sin_ref = qg_ref = kg_ref = bd_ref = vt_ref = None
    if has_prev:
        y1_ref, y2_ref, pmod_ref, pw_ref = next(it), next(it), next(it), next(it)
    if use_rope:
        cos_ref, sin_ref = next(it), next(it)
    if has_norm:
        qg_ref, kg_ref, bd_ref = next(it), next(it), next(it)
    zq_ref, zg_ref = next(it), next(it)
    if vt_blocks:
        vt_ref = next(it)
    if has_prev:
        xnew_ref = next(it)

    tm = x_ref.shape[0]
    sub = tm if has_norm else min(tm, PROJ_SUBTILE)
    q_scale = HEAD ** -0.5 * LOG2E
    first_half = lax.broadcasted_iota(jnp.int32, (sub, LANES), 1) % HEAD < HEAD // 2

    def head_norm(z, gain_ref):
        ss = _head_sumsq(z, bd_ref[...])
        return z * lax.rsqrt(ss * (1.0 / HEAD) + EPS) * gain_ref[...]

    for r0 in range(0, tm, sub):
        rows = slice(r0, r0 + sub)
        x = x_ref[rows, :]
        if has_prev:
            half = y1_ref.shape[1]
            y = (jnp.dot(y1_ref[rows, :], pw_ref[:half, :], preferred_element_type=F32)
                 + jnp.dot(y2_ref[rows, :], pw_ref[half:, :], preferred_element_type=F32))
            x = x + pmod_ref[0, 2:3, :] * y
            xnew_ref[rows, :] = x
        h = x * lax.rsqrt(jnp.mean(x * x, axis=-1, keepdims=True) + EPS) * g_ref[...]
        h = h * (1.0 + mod_ref[0, 1:2, :]) + mod_ref[0, 0:1, :]
        hb = h.astype(BF16)

        def rope(z):
            if not use_rope:
                return z
            return z * cos_ref[rows, :] + _rot_half(z, first_half) * sin_ref[rows, :]

        for b0, b1 in groups:
            z = jnp.dot(hb, w_ref[:, b0 * LANES:b1 * LANES], preferred_element_type=F32)
            for j in range(b0, b1):
                zz = z[:, (j - b0) * LANES:(j - b0 + 1) * LANES]
                op = ops[j]
                if op == "qrope":
                    zz = rope(zz * q_scale)
                elif op == "rope":
                    zz = rope(zz)
                elif op == "kscale":
                    zz = zz * (C_DK ** -0.5)
                elif op == "qnorm":
                    zz = rope(head_norm(zz, qg_ref)) * q_scale
                elif op == "knorm":
                    zz = rope(head_norm(zz, kg_ref))
                zq_ref[rows, j * LANES:(j + 1) * LANES] = zz.astype(zq_ref.dtype)
                if j in vt_blocks:
                    vt_ref[vt_blocks.index(j), :, rows] = zz.T.astype(vt_ref.dtype)
        n_half = N_GATE_BLK // 2
        for g0 in (0, n_half):
            c0 = (N_QKV_BLK + g0) * LANES
            z = jnp.dot(hb, w_ref[:, c0:c0 + n_half * LANES], preferred_element_type=F32)
            zg_ref[rows, g0 * LANES:(g0 + n_half) * LANES] = _silu(z)


def _project(x2d, mod, mod_row_of_tile, norm_g, w_bf16, *, tm, kind, rope_tabs, norm_params, out_dtype,
             vt_blocks=(), prev=None):
    t, d = x2d.shape
    out_specs = [pl.BlockSpec((tm, N_QKV_BLK * LANES), lambda i: (i, 0)),
                 pl.BlockSpec((tm, N_GATE_BLK * LANES), lambda i: (i, 0))]
    out_shape = [jax.ShapeDtypeStruct((t, N_QKV_BLK * LANES), out_dtype),
                 jax.ShapeDtypeStruct((t, N_GATE_BLK * LANES), F32)]
    if vt_blocks:
        out_specs.append(pl.BlockSpec((len(vt_blocks), LANES, tm), lambda i: (0, 0, i)))
        out_shape.append(jax.ShapeDtypeStruct((len(vt_blocks), LANES, t), out_dtype))
    if prev is not None:
        out_specs.append(pl.BlockSpec((tm, d), lambda i: (i, 0)))
        out_shape.append(jax.ShapeDtypeStruct((t, d), F32))
    use_rope = rope_tabs is not None
    has_norm = kind == "cd"
    ops, groups = (AB_OPS if kind == "ab" else CD_OPS), PROJ_GROUPS
    in_specs = [pl.BlockSpec((tm, d), lambda i: (i, 0)),
                pl.BlockSpec((1, 3, d), lambda i: (mod_row_of_tile(i), 0, 0)),
                pl.BlockSpec((1, d), lambda i: (0, 0)),
                pl.BlockSpec(w_bf16.shape, lambda i: (0, 0))]
    args = [x2d, mod, norm_g.reshape(1, d), w_bf16]
    if prev is not None:
        y1, y2, prev_mod, prev_w = prev
        in_specs += [pl.BlockSpec((tm, y1.shape[1]), lambda i: (i, 0)),
                     pl.BlockSpec((tm, y2.shape[1]), lambda i: (i, 0)),
                     pl.BlockSpec((1, 3, d), lambda i: (mod_row_of_tile(i), 0, 0)),
                     pl.BlockSpec(prev_w.shape, lambda i: (0, 0))]
        args += [y1, y2, prev_mod, prev_w]
    if use_rope:
        n_seq_tiles = rope_tabs[0].shape[0] // tm
        in_specs += [pl.BlockSpec((tm, LANES), lambda i: (i % n_seq_tiles, 0))] * 2
        args += list(rope_tabs)
    if has_norm:
        in_specs += [pl.BlockSpec((1, LANES), lambda i: (0, 0))] * 2 + [pl.BlockSpec((LANES, LANES), lambda i: (0, 0))]
        args += list(norm_params)
    return pl.pallas_call(
        functools.partial(_proj_kernel, ops=ops, groups=groups, use_rope=use_rope, has_norm=has_norm,
                          vt_blocks=tuple(vt_blocks), has_prev=prev is not None),
        grid=(t // tm,),
        in_specs=in_specs,
        out_specs=out_specs,
        out_shape=out_shape,
        compiler_params=pltpu.CompilerParams(vmem_limit_bytes=VMEM_LIMIT),
        name="project_" + kind,
    )(*args)


def _pair_rows(q):
    lo = jnp.where(_lane_lo((1, LANES)), 1.0, 0.0).astype(BF16)
    qb = q.astype(BF16)
    return jnp.concatenate([qb * lo, qb * (1.0 - lo).astype(BF16)], axis=0)


def _pair_cols(q):
    qt = q.astype(F32).T
    top = lax.broadcasted_iota(jnp.int32, qt.shape, 0) < HEAD
    return jnp.concatenate([jnp.where(top, qt, 0.0), jnp.where(top, 0.0, qt)], axis=1).astype(BF16)


def _ctx_attn_kernel(*refs, mode, has_sink, kv_per_col, lam_init):
    it = iter(refs)
    q_ref, k_ref, v_ref = next(it), next(it), next(it)
    sink_ref = next(it) if has_sink else None
    if mode == "diff":
        lq1_ref, lk1_ref, lq2_ref, lk2_ref, bg_ref = (next(it) for _ in range(5))
    gate_ref, y_ref = next(it), next(it)

    tq = q_ref.shape[1]
    lo = _lane_lo((tq, LANES))
    for c in range(N_QCOL):
        cols = slice(c * LANES, (c + 1) * LANES)
        kv_cols = cols if kv_per_col else slice(0, LANES)
        qrows = _pair_rows(q_ref[0, :, cols])
        s = lax.dot_general(qrows, k_ref[0, :, kv_cols].astype(BF16), (((1,), (1,)), ((), ())),
                            preferred_element_type=F32)
        v = v_ref[0, :, kv_cols].astype(BF16)
        halves = []
        for half, head in ((s[:tq], c), (s[tq:], c + N_QCOL)):
            m = jnp.max(half, axis=-1, keepdims=True)
            if has_sink:
                sk = sink_ref[head] * LOG2E
                m = jnp.maximum(m, sk)
            p = jnp.exp2(half - m)
            l = jnp.sum(p, axis=-1, keepdims=True)
            if has_sink:
                l = l + jnp.exp2(sk - m)
            halves.append(jnp.dot(p.astype(BF16), v, preferred_element_type=F32) / l)
        o2 = jnp.concatenate(halves, axis=0)
        if mode == "pair":
            o = jnp.where(lo, o2[:tq], o2[tq:])
        else:
            lam = (jnp.exp(jnp.sum(lq1_ref[...] * lk1_ref[...], axis=-1, keepdims=True))
                   - jnp.exp(jnp.sum(lq2_ref[...] * lk2_ref[...], axis=-1, keepdims=True)) + lam_init)
            o = o2[:tq] - lam * o2[tq:]
            o = o * lax.rsqrt(jnp.mean(o * o, axis=-1, keepdims=True) + EPS) * bg_ref[...] * (1.0 - lam_init)
        y_ref[0, :, cols] = (o * gate_ref[0, :, cols]).astype(y_ref.dtype)


def _ctx_attention(zq, zg, *, q_blk, k_blk, v_blk, kv_per_col, gate_blk, mode,
                   sink=None, diff_params=None, lam_init=0.0):
    b, n, _ = zq.shape
    w4 = N_QCOL * LANES
    kv_w = w4 if kv_per_col else LANES
    assert (q_blk * LANES) % w4 == 0 and (gate_blk * LANES) % w4 == 0
    assert (k_blk * LANES) % kv_w == 0 and (v_blk * LANES) % kv_w == 0
    in_specs = [pl.BlockSpec((1, n, w4), lambda bi: (bi, 0, q_blk * LANES // w4)),
                pl.BlockSpec((1, n, kv_w), lambda bi: (bi, 0, k_blk * LANES // kv_w)),
                pl.BlockSpec((1, n, kv_w), lambda bi: (bi, 0, v_blk * LANES // kv_w))]
    args = [zq, zq, zq]
    if sink is not None:
        in_specs.append(pl.BlockSpec(memory_space=pltpu.SMEM))
        args.append(sink)
    if mode == "diff":
        in_specs += [pl.BlockSpec((1, HEAD), lambda bi: (0, 0))] * 4
        in_specs.append(pl.BlockSpec((1, LANES), lambda bi: (0, 0)))
        args += list(diff_params)
    in_specs.append(pl.BlockSpec((1, n, w4), lambda bi: (bi, 0, gate_blk * LANES // w4)))
    args.append(zg)
    return pl.pallas_call(
        functools.partial(_ctx_attn_kernel, mode=mode, has_sink=sink is not None, kv_per_col=kv_per_col,
                          lam_init=lam_init),
        grid=(b,),
        in_specs=in_specs,
        out_specs=pl.BlockSpec((1, n, w4), lambda bi: (bi, 0, 0)),
        out_shape=jax.ShapeDtypeStruct((b, n, w4), BF16),
        compiler_params=pltpu.CompilerParams(vmem_limit_bytes=VMEM_LIMIT),
        name="ctx_attention_" + mode,
    )(*args)


def _keymajor_attn_kernel(*refs, mode, tk, lam_init):
    it = iter(refs)
    q_ref, k_ref, vt_ref, kx_ref, vxt_ref = (next(it) for _ in range(5))
    if mode == "diff":
        lq1_ref, lk1_ref, lq2_ref, lk2_ref, bg_ref = (next(it) for _ in range(5))
    gate_ref, y_ref, s_scr, sx_scr = next(it), next(it), next(it), next(it)

    tq = q_ref.shape[1]
    qt = _pair_cols(q_ref[0])

    def scores(k, slot):
        s = jnp.dot(k, qt, preferred_element_type=F32)
        slot[...] = s
        return jnp.max(s, axis=0, keepdims=True)

    vd = HEAD if mode == "pair" else LANES

    def with_ones(vt):
        return jnp.concatenate([vt, jnp.ones((ONES_ROWS, vt.shape[1]), BF16)], axis=0)

    def consume(slot, cmax, vt, m, acc_lo, acc_hi):
        m_new = jnp.maximum(m, cmax)
        alpha = jnp.exp2(m - m_new)
        p = jnp.exp2(slot[...] - m_new).astype(BF16)
        if mode == "pair":
            v_lo, v_hi = with_ones(vt[:HEAD]), with_ones(vt[HEAD:])
        else:
            v_lo = v_hi = with_ones(vt)
        acc_lo = alpha[:, :tq] * acc_lo + jnp.dot(v_lo, p[:, :tq], preferred_element_type=F32)
        acc_hi = alpha[:, tq:] * acc_hi + jnp.dot(v_hi, p[:, tq:], preferred_element_type=F32)
        return m_new, acc_lo, acc_hi

    n_chunks = k_ref.shape[1] // tk
    carry = (jnp.full((1, 2 * tq), NEG, F32), jnp.zeros((vd + ONES_ROWS, tq), F32),
             jnp.zeros((vd + ONES_ROWS, tq), F32))
    n_slots = s_scr.shape[0]
    ahead = n_slots - 1
    chunk_scores = lambda j: scores(k_ref[0, j * tk:(j + 1) * tk, :], s_scr.at[j % n_slots])
    cmax = [chunk_scores(j) for j in range(min(ahead, n_chunks))]
    cmax_x = scores(kx_ref[0].astype(BF16), sx_scr)
    for j in range(n_chunks):
        if j + ahead < n_chunks:
            cmax.append(chunk_scores(j + ahead))
        carry = consume(s_scr.at[j % n_slots], cmax[j], vt_ref[0, :, j * tk:(j + 1) * tk], *carry)
    _, acc_lo, acc_hi = consume(sx_scr, cmax_x, vxt_ref[0].astype(BF16), *carry)

    ot_lo = acc_lo[:vd] / acc_lo[vd:vd + 1]
    ot_hi = acc_hi[:vd] / acc_hi[vd:vd + 1]
    if mode == "pair":
        o = jnp.concatenate([ot_lo, ot_hi], axis=0).T
    else:
        lam = (jnp.exp(jnp.sum(lq1_ref[...] * lk1_ref[...], axis=-1, keepdims=True))
               - jnp.exp(jnp.sum(lq2_ref[...] * lk2_ref[...], axis=-1, keepdims=True)) + lam_init)
        o = (ot_lo - lam * ot_hi).T
        o = o * lax.rsqrt(jnp.mean(o * o, axis=-1, keepdims=True) + EPS) * bg_ref[...] * (1.0 - lam_init)
    y_ref[0] = (o * gate_ref[0]).astype(y_ref.dtype)


def _keymajor_attention(zq, zg, vt, kx, vxt, *, q_blk, k_blk, vt_blk, kv_per_col, gate_blk, mode, tq, tk,
                        diff_params=None, lam_init=0.0):
    b, n, _ = zq.shape
    per = (lambda c: c) if kv_per_col else (lambda c: 0)
    if kx.ndim == 4:
        kx_spec = pl.BlockSpec((None, 1) + kx.shape[2:], lambda bi, c, qi: (bi, c, 0, 0))
        vx_spec = pl.BlockSpec((None, 1) + vxt.shape[2:], lambda bi, c, qi: (bi, c, 0, 0))
    else:
        kx_spec = pl.BlockSpec((1,) + kx.shape[1:], lambda bi, c, qi: (bi, 0, 0))
        vx_spec = pl.BlockSpec((1,) + vxt.shape[1:], lambda bi, c, qi: (bi, 0, 0))
    in_specs = [pl.BlockSpec((1, tq, LANES), lambda bi, c, qi: (bi, qi, q_blk + c)),
                pl.BlockSpec((1, n, LANES), lambda bi, c, qi: (bi, 0, k_blk + per(c))),
                pl.BlockSpec((1, LANES, n), lambda bi, c, qi: (vt_blk + per(c), 0, bi)),
                kx_spec, vx_spec]
    args = [zq, zq, vt, kx, vxt]
    if mode == "diff":
        in_specs += [pl.BlockSpec((1, HEAD), lambda bi, c, qi: (0, 0))] * 4
        in_specs.append(pl.BlockSpec((1, LANES), lambda bi, c, qi: (0, 0)))
        args += list(diff_params)
    in_specs.append(pl.BlockSpec((1, tq, LANES), lambda bi, c, qi: (bi, qi, gate_blk + c)))
    args.append(zg)
    return pl.pallas_call(
        functools.partial(_keymajor_attn_kernel, mode=mode, tk=tk, lam_init=lam_init),
        grid=(b, N_QCOL, n // tq),
        in_specs=in_specs,
        out_specs=pl.BlockSpec((1, tq, LANES), lambda bi, c, qi: (bi, qi, c)),
        out_shape=jax.ShapeDtypeStruct((b, n, N_QCOL * LANES), BF16),
        scratch_shapes=[pltpu.VMEM((SCORE_SLOTS, tk, 2 * tq), F32), pltpu.VMEM((kx.shape[-2], 2 * tq), F32)],
        compiler_params=pltpu.CompilerParams(vmem_limit_bytes=VMEM_LIMIT),
        name="attention_keymajor_" + mode,
    )(*args)


def _window_attn_kernel(q_ref, k_ref, vt_ref, kx_ref, vxt_ref, sink_ref, gate_ref, y_ref, s_scr):
    n = k_ref.shape[1]
    span = 3 * CHUNK
    sk = jnp.concatenate([jnp.full((1, CHUNK), sink_ref[h] * LOG2E, F32) for h in range(GQA_HEADS)], axis=1)
    kx = kx_ref[0].astype(BF16)
    vxt = vxt_ref[0].astype(BF16)
    half = N_QCOL * CHUNK

    def with_ones(vt):
        return jnp.concatenate([vt, jnp.ones((ONES_ROWS, vt.shape[1]), BF16)], axis=0)

    starts, maxes = [], []
    for sub in range(WINDOW_QBLOCKS):
        blk = pl.program_id(1) * WINDOW_QBLOCKS + sub
        qts = [_pair_cols(q_ref[0, sub * CHUNK:(sub + 1) * CHUNK, c * LANES:(c + 1) * LANES])
               for c in range(N_QCOL)]
        qt = jnp.concatenate([t[:, :CHUNK] for t in qts] + [t[:, CHUNK:] for t in qts], axis=1)
        w0 = pl.multiple_of(jnp.clip((blk - 1) * CHUNK, 0, n - span), CHUNK)
        s_w = jnp.dot(k_ref[0, pl.ds(w0, span), :], qt, preferred_element_type=F32)
        s_x = jnp.dot(kx, qt, preferred_element_type=F32)
        dist = (w0 - blk * CHUNK + lax.broadcasted_iota(jnp.int32, (span, CHUNK), 0)
                - lax.broadcasted_iota(jnp.int32, (span, CHUNK), 1))
        bias = jnp.where((dist >= -WINDOW) & (dist <= WINDOW), 0.0, NEG)
        s_w = s_w + jnp.concatenate([bias] * GQA_HEADS, axis=1)
        s_scr[sub, :span] = s_w
        s_scr[sub, span:] = s_x
        starts.append(w0)
        maxes.append(jnp.maximum(jnp.maximum(jnp.max(s_w, axis=0, keepdims=True),
                                             jnp.max(s_x, axis=0, keepdims=True)), sk))

    for sub in range(WINDOW_QBLOCKS):
        w0, m = starts[sub], maxes[sub]
        p = jnp.exp2(s_scr[sub] - m).astype(BF16)
        vt = jnp.concatenate([vt_ref[0, :, pl.ds(w0, span)], vxt], axis=1)
        acc_lo = jnp.dot(with_ones(vt[:HEAD]), p[:, :half], preferred_element_type=F32)
        acc_hi = jnp.dot(with_ones(vt[HEAD:]), p[:, half:], preferred_element_type=F32)
        sink_p = jnp.exp2(sk - m)
        ot_lo = acc_lo[:HEAD] / (acc_lo[HEAD:HEAD + 1] + sink_p[:, :half])
        ot_hi = acc_hi[:HEAD] / (acc_hi[HEAD:HEAD + 1] + sink_p[:, half:])
        o = jnp.concatenate(
            [jnp.concatenate([ot_lo[:, c * CHUNK:(c + 1) * CHUNK], ot_hi[:, c * CHUNK:(c + 1) * CHUNK]], axis=0).T
             for c in range(N_QCOL)], axis=1)
        rows = slice(sub * CHUNK, (sub + 1) * CHUNK)
        y_ref[0, rows, :] = (o * gate_ref[0, rows, :]).astype(y_ref.dtype)


def _window_attention(zq, zg, vt, kx, vxt, sink):
    b, n, _ = zq.shape
    w4 = N_QCOL * LANES
    tq = WINDOW_QBLOCKS * CHUNK
    return pl.pallas_call(
        _window_attn_kernel,
        grid=(b, n // tq),
        in_specs=[pl.BlockSpec((1, tq, w4), lambda bi, i: (bi, i, 0)),
                  pl.BlockSpec((1, n, LANES), lambda bi, i: (bi, 0, PAIR_K_BLK)),
                  pl.BlockSpec((1, LANES, n), lambda bi, i: (0, 0, bi)),
                  pl.BlockSpec((1,) + kx.shape[1:], lambda bi, i: (bi, 0, 0)),
                  pl.BlockSpec((1,) + vxt.shape[1:], lambda bi, i: (bi, 0, 0)),
                  pl.BlockSpec(memory_space=pltpu.SMEM),
                  pl.BlockSpec((1, tq, w4), lambda bi, i: (bi, i, 0))],
        out_specs=pl.BlockSpec((1, tq, w4), lambda bi, i: (bi, i, 0)),
        out_shape=jax.ShapeDtypeStruct((b, n, w4), BF16),
        scratch_shapes=[pltpu.VMEM((WINDOW_QBLOCKS, 3 * CHUNK + kx.shape[1], GQA_HEADS * CHUNK), F32)],
        compiler_params=pltpu.CompilerParams(vmem_limit_bytes=VMEM_LIMIT),
        name="attention_window",
    )(zq, zq, vt, kx, vxt, sink, zg)


def _retention_kernel(*refs, has_state_in, emit_state):
    it = iter(refs)
    q_ref, k_ref, v_ref, decf_ref, decb_ref, cg_ref, gate_ref = (next(it) for _ in range(7))
    sf_in = sb_in = sf_out = sb_out = None
    if has_state_in:
        sf_in, sb_in = next(it), next(it)
    y_ref = next(it)
    if emit_state:
        sf_out, sb_out = next(it), next(it)
    o_scr, u_scr, dm_scr = next(it), next(it), next(it)

    nc = q_ref.shape[1] // CHUNK
    hps = q_ref.shape[2] // LANES
    unroll_local = max(1, min(nc, RET_UNROLL_LOCAL // hps))
    unroll_finish = max(1, min(nc, RET_UNROLL_FINISH // hps))
    ri = lax.broadcasted_iota(jnp.int32, (CHUNK, CHUNK), 0).astype(F32)
    ci = lax.broadcasted_iota(jnp.int32, (CHUNK, CHUNK), 1).astype(F32)
    tok_col = lax.broadcasted_iota(jnp.int32, (CHUNK, 1), 0).astype(F32)
    tok_row = lax.broadcasted_iota(jnp.int32, (1, CHUNK), 1).astype(F32)
    rel_f, rel_b = ri - ci, ci - ri - 1.0

    cross_f, cross_b, kdec_f, kdec_b, cdec_f, cdec_b = [], [], [], [], [], []
    for h in range(hps):
        lg_f = -jnp.exp(decf_ref[h][:, :1])
        lg_b = -jnp.exp(decb_ref[h][:, :1])
        cross_f.append(jnp.exp(lg_f * (tok_col + 1.0)))
        cross_b.append(jnp.exp(lg_b * (CHUNK - 1.0 - tok_col)))
        kdec_f.append(jnp.exp(lg_f * (CHUNK - 1.0 - tok_row)))
        kdec_b.append(jnp.exp(lg_b * tok_row))
        cdec_f.append(jnp.exp(lg_f * CHUNK))
        cdec_b.append(jnp.exp(lg_b * CHUNK))
        dm_scr[h] = jnp.where(rel_f >= 0.0, jnp.exp(lg_f * jnp.maximum(rel_f, 0.0)),
                              jnp.exp(lg_b * jnp.maximum(rel_b, 0.0)))

    def chunk_rows(c):
        return pl.ds(pl.multiple_of(c * CHUNK, CHUNK), CHUNK)

    def head_cols(h):
        return slice(h * LANES, (h + 1) * LANES)

    def local_body(c, _):
        rows = chunk_rows(c)
        for h in range(hps):
            qh = q_ref[0, rows, head_cols(h)].astype(BF16)
            kf = k_ref[0, rows, head_cols(h)].astype(F32)
            vh = v_ref[0, rows, head_cols(h)].astype(BF16)
            kt = kf.T
            att = jnp.dot(qh, kt.astype(BF16), preferred_element_type=F32) * dm_scr[h]
            lhs = jnp.concatenate([att.astype(BF16), (kt * kdec_f[h]).astype(BF16), (kt * kdec_b[h]).astype(BF16)],
                                  axis=0)
            r = jnp.dot(lhs, vh, preferred_element_type=F32)
            o_scr[rows, head_cols(h)] = r[:CHUNK]
            u_scr[h, c, 0] = r[CHUNK:2 * CHUNK]
            u_scr[h, c, 1] = r[2 * CHUNK:]
        return 0
    lax.fori_loop(0, nc, local_body, 0, unroll=unroll_local)

    def scan_body(t, states):
        out = []
        for h in range(hps):
            s_f, s_b = states[2 * h], states[2 * h + 1]
            inc_f, inc_b = u_scr[h, t, 0], u_scr[h, nc - 1 - t, 1]
            u_scr[h, t, 0] = s_f
            u_scr[h, nc - 1 - t, 1] = s_b
            out += [cdec_f[h] * s_f + inc_f, cdec_b[h] * s_b + inc_b]
        return tuple(out)
    zero = jnp.zeros((CHUNK, CHUNK), F32)
    init = tuple(x for h in range(hps) for x in ((sf_in[0, h], sb_in[0, h]) if has_state_in else (zero, zero)))
    final = lax.fori_loop(0, nc, scan_body, init, unroll=2 if hps == 1 else 1)
    if emit_state:
        for h in range(hps):
            sf_out[0, h] = final[2 * h].astype(sf_out.dtype)
            sb_out[0, h] = final[2 * h + 1].astype(sb_out.dtype)

    def finish_body(c, _):
        rows = chunk_rows(c)
        for h in range(hps):
            qh = q_ref[0, rows, head_cols(h)].astype(BF16)
            states = jnp.concatenate([u_scr[h, c, 0], u_scr[h, c, 1]], axis=1).astype(BF16)
            r = jnp.dot(qh, states, preferred_element_type=F32)
            o = o_scr[rows, head_cols(h)] + cross_f[h] * r[:, :LANES] + cross_b[h] * r[:, LANES:]
            mu = jnp.mean(o, axis=-1, keepdims=True)
            d = o - mu
            var = jnp.mean(d * d, axis=-1, keepdims=True)
            y = d * lax.rsqrt(var + EPS) * cg_ref[h]
            y_ref[0, rows, head_cols(h)] = (y * gate_ref[0, rows, head_cols(h)]).astype(y_ref.dtype)
        return 0
    lax.fori_loop(0, nc, finish_body, 0, unroll=unroll_finish)


def _retention(zq, zg, dec_f, dec_b, c_norm_g, states, emit_state, hps):
    b, n, _ = zq.shape
    w = hps * LANES
    seq_spec = lambda blk: pl.BlockSpec((1, n, w), lambda bi, g: (bi, 0, blk // hps + g))
    st_spec = pl.BlockSpec((1, hps, CHUNK, CHUNK), lambda bi, g: (bi, g, 0, 0))
    head_spec = pl.BlockSpec((hps, 1, LANES), lambda bi, g: (g, 0, 0))
    bcast = lambda p: jnp.broadcast_to(p.astype(F32)[:, None, None], (C_HEADS, 1, LANES))
    in_specs = [seq_spec(0), seq_spec(C_HEADS), seq_spec(2 * C_HEADS), head_spec, head_spec, head_spec,
                seq_spec(0)]
    args = [zq, zq, zq, bcast(dec_f), bcast(dec_b), c_norm_g.astype(F32)[:, None, :], zg]
    if states is not None:
        in_specs += [st_spec, st_spec]
        args += list(states)
    out_specs = [seq_spec(0)]
    out_shape = [jax.ShapeDtypeStruct((b, n, C_HEADS * LANES), BF16)]
    if emit_state:
        out_specs += [st_spec, st_spec]
        out_shape += [jax.ShapeDtypeStruct((b, C_HEADS, CHUNK, CHUNK), F32)] * 2
    return pl.pallas_call(
        functools.partial(_retention_kernel, has_state_in=states is not None, emit_state=emit_state),
        grid=(b, C_HEADS // hps),
        in_specs=in_specs,
        out_specs=out_specs,
        out_shape=out_shape,
        scratch_shapes=[pltpu.VMEM((n, w), F32),
                        pltpu.VMEM((hps, n // CHUNK, 2, CHUNK, CHUNK), F32),
                        pltpu.VMEM((hps, CHUNK, CHUNK), F32)],
        compiler_params=pltpu.CompilerParams(vmem_limit_bytes=VMEM_LIMIT),
        name="retention",
    )(*args)


def _out_kernel(y1_ref, y2_ref, x_ref, mod_ref, w_ref, fg_ref, o_ref, *, final_norm):
    half = y1_ref.shape[1]
    y = (jnp.dot(y1_ref[...], w_ref[:half, :], preferred_element_type=F32)
         + jnp.dot(y2_ref[...], w_ref[half:, :], preferred_element_type=F32))
    x = x_ref[...] + mod_ref[0, 2:3, :] * y
    if final_norm:
        x = x * lax.rsqrt(jnp.mean(x * x, axis=-1, keepdims=True) + EPS) * fg_ref[...]
    o_ref[...] = x


def _out_project(y1, y2, x2d, mod, mod_row_of_tile, w_bf16, final_g, *, tm, final_norm):
    t, d = x2d.shape
    half = y1.shape[1]
    return pl.pallas_call(
        functools.partial(_out_kernel, final_norm=final_norm),
        grid=(t // tm,),
        in_specs=[pl.BlockSpec((tm, half), lambda i: (i, 0)),
                  pl.BlockSpec((tm, half), lambda i: (i, 0)),
                  pl.BlockSpec((tm, d), lambda i: (i, 0)),
                  pl.BlockSpec((1, 3, d), lambda i: (mod_row_of_tile(i), 0, 0)),
                  pl.BlockSpec(w_bf16.shape, lambda i: (0, 0)),
                  pl.BlockSpec((1, d), lambda i: (0, 0))],
        out_specs=pl.BlockSpec((tm, d), lambda i: (i, 0)),
        out_shape=jax.ShapeDtypeStruct((t, d), F32),
        compiler_params=pltpu.CompilerParams(vmem_limit_bytes=VMEM_LIMIT),
        name="out_project",
    )(y1, y2, x2d, mod, w_bf16, final_g.reshape(1, d))


def _pair_perm():
    return np.concatenate([np.arange(h * HEAD, (h + 1) * HEAD) for h in PAIR_ORDER])


def _reorder(w, idx, axis):
    idx = np.asarray(idx)
    runs = np.split(idx, np.flatnonzero(np.diff(idx) != 1) + 1)
    return jnp.concatenate([lax.slice_in_dim(w, int(r[0]), int(r[-1]) + 1, axis=axis) for r in runs], axis=axis)


def _rope_tables(n):
    rows = n // GRID_W
    row = jnp.repeat(jnp.arange(rows, dtype=F32), GRID_W)
    col = jnp.tile(jnp.arange(GRID_W, dtype=F32), rows)
    nf = HEAD // 4
    inv = ROPE_THETA ** (-jnp.arange(nf, dtype=F32) / nf)
    ang = jnp.concatenate([row[:, None] * inv, col[:, None] * inv], axis=-1)
    cos, sin = jnp.cos(ang), jnp.sin(ang)
    return jnp.tile(cos, (1, 4)), jnp.tile(jnp.concatenate([-sin, sin], axis=-1), (1, 2))


def _pair_kv(cache):
    b, g, p, d = cache.shape
    return cache.transpose(0, 2, 1, 3).reshape(b, p, g * d)


def _unpair_kv(z, blk, n_blk, heads):
    b, p, _ = z.shape
    t = z[:, :, blk * LANES:(blk + n_blk) * LANES]
    return t.reshape(b, p, heads, t.shape[-1] // heads).transpose(0, 2, 1, 3)


def kernel(x_prompt, x_sample, cache_a_k, cache_a_v, cache_b_k, cache_b_v, state_c_fwd, state_c_bwd, cache_d_k, cache_d_v, c, c_ctx, norm_g, mod_w, mod_b, ab_w_in, ab_w_out, a_sink, b_lq1, b_lk1, b_lq2, b_lk2, b_norm_g, cd_w_in, cd_w_out, c_decay_f, c_decay_b, c_norm_g, d_q_norm_g, d_k_norm_g, final_g):
    depth = norm_g.shape[0]
    bp, sp, d = x_prompt.shape
    bs, ss, _ = x_sample.shape
    dt = x_prompt.dtype

    ctx_row = bs
    pad = (-(bs + 1)) % 8
    cond = jnp.concatenate([c, c_ctx[None, :], jnp.zeros((pad, d), c.dtype)], axis=0)
    mod = _modulation(cond, mod_w, mod_b).reshape(depth, cond.shape[0], 3, d)

    perm = _pair_perm()
    qkv_w = N_QKV_BLK * LANES
    rope_tabs = _rope_tables(ss)
    bd = jnp.asarray(np.kron(np.eye(LANES // HEAD), np.ones((HEAD, HEAD))), BF16)

    tm_s = 1024
    tm_p = 512
    tiles_per_seq = ss // tm_s
    row_s = lambda i: i // tiles_per_seq
    row_p = lambda i: ctx_row

    xp = x_prompt.reshape(bp * sp, d)
    xs = x_sample.reshape(bs * ss, d)
    outs = {k: [] for k in ("a_k", "a_v", "b_k", "b_v", "c_f", "c_b", "d_k", "d_v")}

    pend_p = pend_s = None

    def project(x2d, pending, row_fn, tm, **kw):
        res = list(_project(x2d, mod[layer], row_fn, norm_g[layer], w_in, tm=tm, prev=pending, **kw))
        if pending is not None:
            x2d = res.pop()
        return x2d, res

    for layer in range(depth):
        i = layer // 2
        if layer % 2 == 0:
            lam_init = 0.8 - 0.6 * math.exp(-0.3 * layer)
            w4 = N_QCOL * LANES
            ar = np.arange
            cols = np.concatenate([perm, ar(w4 + 2 * LANES, 4 * w4 + 2 * LANES), ar(w4, w4 + 2 * LANES),
                                   qkv_w + perm, ar(qkv_w + w4, qkv_w + 2 * w4)])
            w_in = _reorder(ab_w_in[i].astype(BF16), cols, 1)
            rows = np.arange(ab_w_out.shape[1])
            rows[0:N_QCOL * LANES] = perm
            w_out = _reorder(ab_w_out[i].astype(BF16), rows, 0)
            diff_params = (b_lq1[i][None], b_lk1[i][None], b_lq2[i][None], b_lk2[i][None], b_norm_g[i][None])

            xp, (zq, zg) = project(xp, pend_p, row_p, tm_p, kind="ab", rope_tabs=None, norm_params=None,
                                   out_dtype=F32)
            zq3, zg3 = zq.reshape(bp, sp, -1), zg.reshape(bp, sp, -1)
            ya = _ctx_attention(zq3, zg3, q_blk=0, k_blk=PAIR_K_BLK, v_blk=PAIR_V_BLK, kv_per_col=False,
                                gate_blk=0, mode="pair", sink=a_sink[i])
            yb = _ctx_attention(zq3, zg3, q_blk=4, k_blk=8, v_blk=12, kv_per_col=True, gate_blk=4,
                                mode="diff", diff_params=diff_params, lam_init=lam_init)
            pend_p = (ya.reshape(bp * sp, -1), yb.reshape(bp * sp, -1), mod[layer], w_out)
            outs["a_k"].append(_unpair_kv(zq3, PAIR_K_BLK, 1, 2))
            outs["a_v"].append(_unpair_kv(zq3, PAIR_V_BLK, 1, 2))
            outs["b_k"].append(_unpair_kv(zq3, 8, 4, 4))
            outs["b_v"].append(_unpair_kv(zq3, 12, 4, 4))

            xs, (zq, zg, vt) = project(xs, pend_s, row_s, tm_s, kind="ab", rope_tabs=rope_tabs, norm_params=None,
                                       out_dtype=BF16, vt_blocks=(PAIR_V_BLK, 12, 13, 14, 15))
            zq3, zg3 = zq.reshape(bs, ss, -1), zg.reshape(bs, ss, -1)
            ya = _window_attention(zq3, zg3, vt, _pair_kv(cache_a_k[:, i]),
                                   _pair_kv(cache_a_v[:, i]).transpose(0, 2, 1), a_sink[i])
            yb = _keymajor_attention(zq3, zg3, vt, cache_b_k[:, i], cache_b_v[:, i].transpose(0, 1, 3, 2),
                                     q_blk=4, k_blk=8, vt_blk=1, kv_per_col=True, gate_blk=4, mode="diff",
                                     tq=TQ_DENSE, tk=TK_DENSE, diff_params=diff_params, lam_init=lam_init)
            pend_s = (ya.reshape(bs * ss, -1), yb.reshape(bs * ss, -1), mod[layer], w_out)
        else:
            cols = np.arange(cd_w_in.shape[2])
            cols[12 * LANES:16 * LANES] = 12 * LANES + perm
            cols[qkv_w + N_QCOL * LANES:qkv_w + 2 * N_QCOL * LANES] = qkv_w + N_QCOL * LANES + perm
            w_in = _reorder(cd_w_in[i].astype(BF16), cols, 1)
            rows = np.arange(cd_w_out.shape[1])
            rows[N_QCOL * LANES:] = N_QCOL * LANES + perm
            w_out = _reorder(cd_w_out[i].astype(BF16), rows, 0)
            norm_params = (jnp.tile(d_q_norm_g[i], 2)[None], jnp.tile(d_k_norm_g[i], 2)[None], bd)

            xp, (zq, zg) = project(xp, pend_p, row_p, tm_p, kind="cd", rope_tabs=None, norm_params=norm_params,
                                   out_dtype=F32)
            zq3, zg3 = zq.reshape(bp, sp, -1), zg.reshape(bp, sp, -1)
            yc, s_f, s_b = _retention(zq3, zg3, c_decay_f[i], c_decay_b[i], c_norm_g[i], None, True,
                                      RET_HEADS_CTX)
            yd = _ctx_attention(zq3, zg3, q_blk=12, k_blk=PAIR_K_BLK, v_blk=PAIR_V_BLK, kv_per_col=False,
                                gate_blk=4, mode="pair")
            pend_p = (yc.reshape(bp * sp, -1), yd.reshape(bp * sp, -1), mod[layer], w_out)
            outs["c_f"].append(s_f.astype(dt))
            outs["c_b"].append(s_b.astype(dt))
            outs["d_k"].append(_unpair_kv(zq3, 16, 1, 2))
            outs["d_v"].append(_unpair_kv(zq3, 17, 1, 2))

            xs, (zq, zg, vt) = project(xs, pend_s, row_s, tm_s, kind="cd", rope_tabs=rope_tabs,
                                       norm_params=norm_params, out_dtype=BF16, vt_blocks=(PAIR_V_BLK,))
            zq3, zg3 = zq.reshape(bs, ss, -1), zg.reshape(bs, ss, -1)
            yc = _retention(zq3, zg3, c_decay_f[i], c_decay_b[i], c_norm_g[i],
                            (state_c_fwd[:, i], state_c_bwd[:, i]), False, RET_HEADS_LATENT)[0]
            yd = _keymajor_attention(zq3, zg3, vt, _pair_kv(cache_d_k[:, i]),
                                     _pair_kv(cache_d_v[:, i]).transpose(0, 2, 1),
                                     q_blk=12, k_blk=PAIR_K_BLK, vt_blk=0, kv_per_col=False, gate_blk=4,
                                     mode="pair",
                                     tq=TQ_DENSE, tk=TK_DENSE)
            pend_s = (yc.reshape(bs * ss, -1), yd.reshape(bs * ss, -1), mod[layer], w_out)

    xp = _out_project(pend_p[0], pend_p[1], xp, pend_p[2], row_p, pend_p[3], final_g, tm=tm_p, final_norm=True)
    xs = _out_project(pend_s[0], pend_s[1], xs, pend_s[2], row_s, pend_s[3], final_g, tm=tm_s, final_norm=True)

    stack = lambda k: jnp.stack(outs[k], axis=1)
    return (xp.reshape(bp, sp, d), xs.reshape(bs, ss, d), stack("a_k"), stack("a_v"), stack("b_k"), stack("b_v"),
            stack("c_f"), stack("c_b"), stack("d_k"), stack("d_v"))
```

```python
import functools
import math

import numpy as np
import jax
import jax.numpy as jnp
from jax import lax
from jax.experimental import pallas as pl
from jax.experimental.pallas import tpu as pltpu

F32 = jnp.float32
BF16 = jnp.bfloat16

LANES = 128
HEAD = 64
GRID_W = 64
CHUNK = 128
WINDOW = 128
ROPE_THETA = 10000.0
EPS = 1e-6
NEG = -1e30
VMEM_LIMIT = 56 * 1024 * 1024
PROJ_SUBTILE = 512
TQ_DENSE = 2048
TK_DENSE = 512
SCORE_SLOTS = 4
WINDOW_QBLOCKS = 8
RET_HEADS_CTX = 4
RET_HEADS_LATENT = 1
RET_UNROLL_LOCAL = 16
RET_UNROLL_FINISH = 8
ONES_ROWS = 16
LOG2E = 1.4426950408889634

GQA_HEADS = 8
PAIR_ORDER = (0, 4, 1, 5, 2, 6, 3, 7)
N_QCOL = 4
C_HEADS = 4
C_DK = 128

N_QKV_BLK = 18
N_GATE_BLK = 8
PAIR_K_BLK, PAIR_V_BLK = 16, 17
AB_OPS = ("qrope",) * 8 + ("rope",) * 4 + ("plain",) * 4 + ("rope", "plain")
CD_OPS = ("plain",) * 4 + ("kscale",) * 4 + ("plain",) * 4 + ("qnorm",) * 4 + ("knorm", "plain")
PROJ_GROUPS = ((0, 4), (4, 8), (8, 12), (12, 16), (16, 18))


def _silu(x):
    return x / (1.0 + jnp.exp(-x))


def _lane_lo(shape):
    return lax.broadcasted_iota(jnp.int32, shape, len(shape) - 1) % LANES < HEAD


def _mod_kernel(cond_ref, w_ref, b_ref, o_ref):
    s = _silu(cond_ref[...])
    o_ref[0] = jnp.dot(s.astype(BF16), w_ref[0].astype(BF16), preferred_element_type=F32) + b_ref[0]


def _modulation(cond, mod_w, mod_b):
    depth, d, d3 = mod_w.shape
    rows = cond.shape[0]
    return pl.pallas_call(
        _mod_kernel,
        grid=(depth, d3 // d),
        in_specs=[pl.BlockSpec((rows, d), lambda l, j: (0, 0)),
                  pl.BlockSpec((1, d, d), lambda l, j: (l, 0, j)),
                  pl.BlockSpec((1, 1, d), lambda l, j: (l, 0, j))],
        out_specs=pl.BlockSpec((1, rows, d), lambda l, j: (l, 0, j)),
        out_shape=jax.ShapeDtypeStruct((depth, rows, d3), F32),
        compiler_params=pltpu.CompilerParams(vmem_limit_bytes=VMEM_LIMIT),
        name="modulation",
    )(cond, mod_w, mod_b.reshape(depth, 1, d3))


def _rot_half(x, first_half):
    return jnp.where(first_half, pltpu.roll(x, LANES - HEAD // 2, 1), pltpu.roll(x, HEAD // 2, 1))


def _head_sumsq(x, bd):
    sq = x * x
    hi = sq.astype(BF16)
    lo = (sq - hi.astype(F32)).astype(BF16)
    return jnp.dot(hi, bd, preferred_element_type=F32) + jnp.dot(lo, bd, preferred_element_type=F32)


def _proj_kernel(*refs, ops, groups, use_rope, has_norm, vt_blocks, has_prev):
    it = iter(refs)
    x_ref, mod_ref, g_ref, w_ref = next(it), next(it), next(it), next(it)
    cos_ref = sin_ref = qg_ref = kg_ref = bd_ref = vt_ref = None
    if has_prev:
        y1_ref, y2_ref, pmod_ref, pw_ref = next(it), next(it), next(it), next(it)
    if use_rope:
        cos_ref, sin_ref = next(it), next(it)
    if has_norm:
        qg_ref, kg_ref, bd_ref = next(it), next(it), next(it)
    zq_ref, zg_ref = next(it), next(it)
    if vt_blocks:
        vt_ref = next(it)
    if has_prev:
        xnew_ref = next(it)

    tm = x_ref.shape[0]
    sub = tm if has_norm else min(tm, PROJ_SUBTILE)
    q_scale = HEAD ** -0.5 * LOG2E
    first_half = lax.broadcasted_iota(jnp.int32, (sub, LANES), 1) % HEAD < HEAD // 2

    def head_norm(z, gain_ref):
        ss = _head_sumsq(z, bd_ref[...])
        return z * lax.rsqrt(ss * (1.0 / HEAD) + EPS) * gain_ref[...]

    for r0 in range(0, tm, sub):
        rows = slice(r0, r0 + sub)
        x = x_ref[rows, :]
        if has_prev:
            half = y1_ref.shape[1]
            y = (jnp.dot(y1_ref[rows, :], pw_ref[:half, :], preferred_element_type=F32)
                 + jnp.dot(y2_ref[rows, :], pw_ref[half:, :], preferred_element_type=F32))
            x = x + pmod_ref[0, 2:3, :] * y
            xnew_ref[rows, :] = x
        h = x * lax.rsqrt(jnp.mean(x * x, axis=-1, keepdims=True) + EPS) * g_ref[...]
        h = h * (1.0 + mod_ref[0, 1:2, :]) + mod_ref[0, 0:1, :]
        hb = h.astype(BF16)

        def rope(z):
            if not use_rope:
                return z
            return z * cos_ref[rows, :] + _rot_half(z, first_half) * sin_ref[rows, :]

        for b0, b1 in groups:
            z = jnp.dot(hb, w_ref[:, b0 * LANES:b1 * LANES], preferred_element_type=F32)
            for j in range(b0, b1):
                zz = z[:, (j - b0) * LANES:(j - b0 + 1) * LANES]
                op = ops[j]
                if op == "qrope":
                    zz = rope(zz * q_scale)
                elif op == "rope":
                    zz = rope(zz)
                elif op == "kscale":
                    zz = zz * (C_DK ** -0.5)
                elif op == "qnorm":
                    zz = rope(head_norm(zz, qg_ref)) * q_scale
                elif op == "knorm":
                    zz = rope(head_norm(zz, kg_ref))
                zq_ref[rows, j * LANES:(j + 1) * LANES] = zz.astype(zq_ref.dtype)
                if j in vt_blocks:
                    vt_ref[vt_blocks.index(j), :, rows] = zz.T.astype(vt_ref.dtype)
        n_half = N_GATE_BLK // 2
        for g0 in (0, n_half):
            c0 = (N_QKV_BLK + g0) * LANES
            z = jnp.dot(hb, w_ref[:, c0:c0 + n_half * LANES], preferred_element_type=F32)
            zg_ref[rows, g0 * LANES:(g0 + n_half) * LANES] = _silu(z)


def _project(x2d, mod, mod_row_of_tile, norm_g, w_bf16, *, tm, kind, rope_tabs, norm_params, out_dtype,
             vt_blocks=(), prev=None):
    t, d = x2d.shape
    out_specs = [pl.BlockSpec((tm, N_QKV_BLK * LANES), lambda i: (i, 0)),
                 pl.BlockSpec((tm, N_GATE_BLK * LANES), lambda i: (i, 0))]
    out_shape = [jax.ShapeDtypeStruct((t, N_QKV_BLK * LANES), out_dtype),
                 jax.ShapeDtypeStruct((t, N_GATE_BLK * LANES), F32)]
    if vt_blocks:
        out_specs.append(pl.BlockSpec((len(vt_blocks), LANES, tm), lambda i: (0, 0, i)))
        out_shape.append(jax.ShapeDtypeStruct((len(vt_blocks), LANES, t), out_dtype))
    if prev is not None:
        out_specs.append(pl.BlockSpec((tm, d), lambda i: (i, 0)))
        out_shape.append(jax.ShapeDtypeStruct((t, d), F32))
    use_rope = rope_tabs is not None
    has_norm = kind == "cd"
    ops, groups = (AB_OPS if kind == "ab" else CD_OPS), PROJ_GROUPS
    in_specs = [pl.BlockSpec((tm, d), lambda i: (i, 0)),
                pl.BlockSpec((1, 3, d), lambda i: (mod_row_of_tile(i), 0, 0)),
                pl.BlockSpec((1, d), lambda i: (0, 0)),
                pl.BlockSpec(w_bf16.shape, lambda i: (0, 0))]
    args = [x2d, mod, norm_g.reshape(1, d), w_bf16]
    if prev is not None:
        y1, y2, prev_mod, prev_w = prev
        in_specs += [pl.BlockSpec((tm, y1.shape[1]), lambda i: (i, 0)),
                     pl.BlockSpec((tm, y2.shape[1]), lambda i: (i, 0)),
                     pl.BlockSpec((1, 3, d), lambda i: (mod_row_of_tile(i), 0, 0)),
                     pl.BlockSpec(prev_w.shape, lambda i: (0, 0))]
        args += [y1, y2, prev_mod, prev_w]
    if use_rope:
        n_seq_tiles = rope_tabs[0].shape[0] // tm
        in_specs += [pl.BlockSpec((tm, LANES), lambda i: (i % n_seq_tiles, 0))] * 2
        args += list(rope_tabs)
    if has_norm:
        in_specs += [pl.BlockSpec((1, LANES), lambda i: (0, 0))] * 2 + [pl.BlockSpec((LANES, LANES), lambda i: (0, 0))]
        args += list(norm_params)
    return pl.pallas_call(
        functools.partial(_proj_kernel, ops=ops, groups=groups, use_rope=use_rope, has_norm=has_norm,
                          vt_blocks=tuple(vt_blocks), has_prev=prev is not None),
        grid=(t // tm,),
        in_specs=in_specs,
        out_specs=out_specs,
        out_shape=out_shape,
        compiler_params=pltpu.CompilerParams(vmem_limit_bytes=VMEM_LIMIT),
        name="project_" + kind,
    )(*args)


def _pair_rows(q):
    lo = jnp.where(_lane_lo((1, LANES)), 1.0, 0.0).astype(BF16)
    qb = q.astype(BF16)
    return jnp.concatenate([qb * lo, qb * (1.0 - lo).astype(BF16)], axis=0)


def _pair_cols(q):
    qt = q.astype(F32).T
    top = lax.broadcasted_iota(jnp.int32, qt.shape, 0) < HEAD
    return jnp.concatenate([jnp.where(top, qt, 0.0), jnp.where(top, 0.0, qt)], axis=1).astype(BF16)


def _ctx_attn_kernel(*refs, mode, has_sink, kv_per_col, lam_init):
    it = iter(refs)
    q_ref, k_ref, v_ref = next(it), next(it), next(it)
    sink_ref = next(it) if has_sink else None
    if mode == "diff":
        lq1_ref, lk1_ref, lq2_ref, lk2_ref, bg_ref = (next(it) for _ in range(5))
    gate_ref, y_ref = next(it), next(it)

    tq = q_ref.shape[1]
    lo = _lane_lo((tq, LANES))
    for c in range(N_QCOL):
        cols = slice(c * LANES, (c + 1) * LANES)
        kv_cols = cols if kv_per_col else slice(0, LANES)
        qrows = _pair_rows(q_ref[0, :, cols])
        s = lax.dot_general(qrows, k_ref[0, :, kv_cols].astype(BF16), (((1,), (1,)), ((), ())),
                            preferred_element_type=F32)
        v = v_ref[0, :, kv_cols].astype(BF16)
        halves = []
        for half, head in ((s[:tq], c), (s[tq:], c + N_QCOL)):
            m = jnp.max(half, axis=-1, keepdims=True)
            if has_sink:
                sk = sink_ref[head] * LOG2E
                m = jnp.maximum(m, sk)
            p = jnp.exp2(half - m)
            l = jnp.sum(p, axis=-1, keepdims=True)
            if has_sink:
                l = l + jnp.exp2(sk - m)
            halves.append(jnp.dot(p.astype(BF16), v, preferred_element_type=F32) / l)
        o2 = jnp.concatenate(halves, axis=0)
        if mode == "pair":
            o = jnp.where(lo, o2[:tq], o2[tq:])
        else:
            lam = (jnp.exp(jnp.sum(lq1_ref[...] * lk1_ref[...], axis=-1, keepdims=True))
                   - jnp.exp(jnp.sum(lq2_ref[...] * lk2_ref[...], axis=-1, keepdims=True)) + lam_init)
            o = o2[:tq] - lam * o2[tq:]
            o = o * lax.rsqrt(jnp.mean(o * o, axis=-1, keepdims=True) + EPS) * bg_ref[...] * (1.0 - lam_init)
        y_ref[0, :, cols] = (o * gate_ref[0, :, cols]).astype(y_ref.dtype)


def _ctx_attention(zq, zg, *, q_blk, k_blk, v_blk, kv_per_col, gate_blk, mode,
                   sink=None, diff_params=None, lam_init=0.0, build_only=False):
    b, n, _ = zq.shape
    w4 = N_QCOL * LANES
    kv_w = w4 if kv_per_col else LANES
    assert (q_blk * LANES) % w4 == 0 and (gate_blk * LANES) % w4 == 0
    assert (k_blk * LANES) % kv_w == 0 and (v_blk * LANES) % kv_w == 0
    in_specs = [pl.BlockSpec((1, n, w4), lambda bi: (bi, 0, q_blk * LANES // w4)),
                pl.BlockSpec((1, n, kv_w), lambda bi: (bi, 0, k_blk * LANES // kv_w)),
                pl.BlockSpec((1, n, kv_w), lambda bi: (bi, 0, v_blk * LANES // kv_w))]
    args = [zq, zq, zq]
    if sink is not None:
        in_specs.append(pl.BlockSpec(memory_space=pltpu.SMEM))
        args.append(sink)
    if mode == "diff":
        in_specs += [pl.BlockSpec((1, HEAD), lambda bi: (0, 0))] * 4
        in_specs.append(pl.BlockSpec((1, LANES), lambda bi: (0, 0)))
        args += list(diff_params)
    in_specs.append(pl.BlockSpec((1, n, w4), lambda bi: (bi, 0, gate_blk * LANES // w4)))
    args.append(zg)
    kw = dict(mode=mode, has_sink=sink is not None, kv_per_col=kv_per_col, lam_init=lam_init)
    if build_only:
        return in_specs, args, kw
    return pl.pallas_call(
        functools.partial(_ctx_attn_kernel, **kw),
        grid=(b,),
        in_specs=in_specs,
        out_specs=pl.BlockSpec((1, n, w4), lambda bi: (bi, 0, 0)),
        out_shape=jax.ShapeDtypeStruct((b, n, w4), BF16),
        compiler_params=pltpu.CompilerParams(vmem_limit_bytes=VMEM_LIMIT),
        name="ctx_attention_" + mode,
    )(*args)


def _ctx_pair_kernel(*refs, n_in, kws):
    ins, outs = refs[:sum(n_in)], refs[sum(n_in):]
    off = 0
    for cnt, kw, o_ref in zip(n_in, kws, outs):
        _ctx_attn_kernel(*ins[off:off + cnt], o_ref, **kw)
        off += cnt


def _ctx_attention_both(zq, zg, branch_a, branch_b):
    b, n, _ = zq.shape
    w4 = N_QCOL * LANES
    built = [_ctx_attention(zq, zg, build_only=True, **br) for br in (branch_a, branch_b)]
    out_spec = pl.BlockSpec((1, n, w4), lambda bi: (bi, 0, 0))
    return pl.pallas_call(
        functools.partial(_ctx_pair_kernel, n_in=tuple(len(s) for s, _, _ in built),
                          kws=tuple(kw for _, _, kw in built)),
        grid=(b,),
        in_specs=[s for specs, _, _ in built for s in specs],
        out_specs=[out_spec, out_spec],
        out_shape=[jax.ShapeDtypeStruct((b, n, w4), BF16)] * 2,
        compiler_params=pltpu.CompilerParams(vmem_limit_bytes=VMEM_LIMIT),
        name="ctx_attention_both",
    )(*[a for _, args, _ in built for a in args])


def _keymajor_attn_kernel(*refs, mode, tk, lam_init):
    it = iter(refs)
    q_ref, k_ref, vt_ref, kx_ref, vxt_ref = (next(it) for _ in range(5))
    if mode == "diff":
        lq1_ref, lk1_ref, lq2_ref, lk2_ref, bg_ref = (next(it) for _ in range(5))
    gate_ref, y_ref, s_scr, sx_scr = next(it), next(it), next(it), next(it)

    tq = q_ref.shape[1]
    qt = _pair_cols(q_ref[0])

    def scores(k, slot):
        s = jnp.dot(k, qt, preferred_element_type=F32)
        slot[...] = s
        return jnp.max(s, axis=0, keepdims=True)

    vd = HEAD if mode == "pair" else LANES

    def with_ones(vt):
        return jnp.concatenate([vt, jnp.ones((ONES_ROWS, vt.shape[1]), BF16)], axis=0)

    def consume(slot, cmax, vt, m, acc_lo, acc_hi):
        m_new = jnp.maximum(m, cmax)
        alpha = jnp.exp2(m - m_new)
        p = jnp.exp2(slot[...] - m_new).astype(BF16)
        if mode == "pair":
            v_lo, v_hi = with_ones(vt[:HEAD]), with_ones(vt[HEAD:])
        else:
            v_lo = v_hi = with_ones(vt)
        acc_lo = alpha[:, :tq] * acc_lo + jnp.dot(v_lo, p[:, :tq], preferred_element_type=F32)
        acc_hi = alpha[:, tq:] * acc_hi + jnp.dot(v_hi, p[:, tq:], preferred_element_type=F32)
        return m_new, acc_lo, acc_hi

    n_chunks = k_ref.shape[1] // tk
    carry = (jnp.full((1, 2 * tq), NEG, F32), jnp.zeros((vd + ONES_ROWS, tq), F32),
             jnp.zeros((vd + ONES_ROWS, tq), F32))
    n_slots = s_scr.shape[0]
    ahead = n_slots - 1
    chunk_scores = lambda j: scores(k_ref[0, j * tk:(j + 1) * tk, :], s_scr.at[j % n_slots])
    cmax = [chunk_scores(j) for j in range(min(ahead, n_chunks))]
    cmax_x = scores(kx_ref[0].astype(BF16), sx_scr)
    for j in range(n_chunks):
        if j + ahead < n_chunks:
            cmax.append(chunk_scores(j + ahead))
        carry = consume(s_scr.at[j % n_slots], cmax[j], vt_ref[0, :, j * tk:(j + 1) * tk], *carry)
    _, acc_lo, acc_hi = consume(sx_scr, cmax_x, vxt_ref[0].astype(BF16), *carry)

    ot_lo = acc_lo[:vd] / acc_lo[vd:vd + 1]
    ot_hi = acc_hi[:vd] / acc_hi[vd:vd + 1]
    if mode == "pair":
        o = jnp.concatenate([ot_lo, ot_hi], axis=0).T
    else:
        lam = (jnp.exp(jnp.sum(lq1_ref[...] * lk1_ref[...], axis=-1, keepdims=True))
               - jnp.exp(jnp.sum(lq2_ref[...] * lk2_ref[...], axis=-1, keepdims=True)) + lam_init)
        o = (ot_lo - lam * ot_hi).T
        o = o * lax.rsqrt(jnp.mean(o * o, axis=-1, keepdims=True) + EPS) * bg_ref[...] * (1.0 - lam_init)
    y_ref[0] = (o * gate_ref[0]).astype(y_ref.dtype)


def _keymajor_attention(zq, zg, vt, kx, vxt, *, q_blk, k_blk, vt_blk, kv_per_col, gate_blk, mode, tq, tk,
                        diff_params=None, lam_init=0.0):
    b, n, _ = zq.shape
    per = (lambda c: c) if kv_per_col else (lambda c: 0)
    if kx.ndim == 4:
        kx_spec = pl.BlockSpec((None, 1) + kx.shape[2:], lambda bi, c, qi: (bi, c, 0, 0))
        vx_spec = pl.BlockSpec((None, 1) + vxt.shape[2:], lambda bi, c, qi: (bi, c, 0, 0))
    else:
        kx_spec = pl.BlockSpec((1,) + kx.shape[1:], lambda bi, c, qi: (bi, 0, 0))
        vx_spec = pl.BlockSpec((1,) + vxt.shape[1:], lambda bi, c, qi: (bi, 0, 0))
    in_specs = [pl.BlockSpec((1, tq, LANES), lambda bi, c, qi: (bi, qi, q_blk + c)),
                pl.BlockSpec((1, n, LANES), lambda bi, c, qi: (bi, 0, k_blk + per(c))),
                pl.BlockSpec((1, LANES, n), lambda bi, c, qi: (vt_blk + per(c), 0, bi)),
                kx_spec, vx_spec]
    args = [zq, zq, vt, kx, vxt]
    if mode == "diff":
        in_specs += [pl.BlockSpec((1, HEAD), lambda bi, c, qi: (0, 0))] * 4
        in_specs.append(pl.BlockSpec((1, LANES), lambda bi, c, qi: (0, 0)))
        args += list(diff_params)
    in_specs.append(pl.BlockSpec((1, tq, LANES), lambda bi, c, qi: (bi, qi, gate_blk + c)))
    args.append(zg)
    return pl.pallas_call(
        functools.partial(_keymajor_attn_kernel, mode=mode, tk=tk, lam_init=lam_init),
        grid=(b, N_QCOL, n // tq),
        in_specs=in_specs,
        out_specs=pl.BlockSpec((1, tq, LANES), lambda bi, c, qi: (bi, qi, c)),
        out_shape=jax.ShapeDtypeStruct((b, n, N_QCOL * LANES), BF16),
        scratch_shapes=[pltpu.VMEM((SCORE_SLOTS, tk, 2 * tq), F32), pltpu.VMEM((kx.shape[-2], 2 * tq), F32)],
        compiler_params=pltpu.CompilerParams(vmem_limit_bytes=VMEM_LIMIT),
        name="attention_keymajor_" + mode,
    )(*args)


def _window_attn_kernel(q_ref, k_ref, vt_ref, kx_ref, vxt_ref, sink_ref, gate_ref, y_ref, s_scr):
    n = k_ref.shape[1]
    span = 3 * CHUNK
    sk = jnp.concatenate([jnp.full((1, CHUNK), sink_ref[h] * LOG2E, F32) for h in range(GQA_HEADS)], axis=1)
    kx = kx_ref[0].astype(BF16)
    vxt = vxt_ref[0].astype(BF16)
    half = N_QCOL * CHUNK

    def with_ones(vt):
        return jnp.concatenate([vt, jnp.ones((ONES_ROWS, vt.shape[1]), BF16)], axis=0)

    starts, maxes = [], []
    for sub in range(WINDOW_QBLOCKS):
        blk = pl.program_id(1) * WINDOW_QBLOCKS + sub
        qts = [_pair_cols(q_ref[0, sub * CHUNK:(sub + 1) * CHUNK, c * LANES:(c + 1) * LANES])
               for c in range(N_QCOL)]
        qt = jnp.concatenate([t[:, :CHUNK] for t in qts] + [t[:, CHUNK:] for t in qts], axis=1)
        w0 = pl.multiple_of(jnp.clip((blk - 1) * CHUNK, 0, n - span), CHUNK)
        s_w = jnp.dot(k_ref[0, pl.ds(w0, span), :], qt, preferred_element_type=F32)
        s_x = jnp.dot(kx, qt, preferred_element_type=F32)
        dist = (w0 - blk * CHUNK + lax.broadcasted_iota(jnp.int32, (span, CHUNK), 0)
                - lax.broadcasted_iota(jnp.int32, (span, CHUNK), 1))
        bias = jnp.where((dist >= -WINDOW) & (dist <= WINDOW), 0.0, NEG)
        s_w = s_w + jnp.concatenate([bias] * GQA_HEADS, axis=1)
        s_scr[sub, :span] = s_w
        s_scr[sub, span:] = s_x
        starts.append(w0)
        maxes.append(jnp.maximum(jnp.maximum(jnp.max(s_w, axis=0, keepdims=True),
                                             jnp.max(s_x, axis=0, keepdims=True)), sk))

    for sub in range(WINDOW_QBLOCKS):
        w0, m = starts[sub], maxes[sub]
        p = jnp.exp2(s_scr[sub] - m).astype(BF16)
        vt = jnp.concatenate([vt_ref[0, :, pl.ds(w0, span)], vxt], axis=1)
        acc_lo = jnp.dot(with_ones(vt[:HEAD]), p[:, :half], preferred_element_type=F32)
        acc_hi = jnp.dot(with_ones(vt[HEAD:]), p[:, half:], preferred_element_type=F32)
        sink_p = jnp.exp2(sk - m)
        ot_lo = acc_lo[:HEAD] / (acc_lo[HEAD:HEAD + 1] + sink_p[:, :half])
        ot_hi = acc_hi[:HEAD] / (acc_hi[HEAD:HEAD + 1] + sink_p[:, half:])
        o = jnp.concatenate(
            [jnp.concatenate([ot_lo[:, c * CHUNK:(c + 1) * CHUNK], ot_hi[:, c * CHUNK:(c + 1) * CHUNK]], axis=0).T
             for c in range(N_QCOL)], axis=1)
        rows = slice(sub * CHUNK, (sub + 1) * CHUNK)
        y_ref[0, rows, :] = (o * gate_ref[0, rows, :]).astype(y_ref.dtype)


def _window_attention(zq, zg, vt, kx, vxt, sink):
    b, n, _ = zq.shape
    w4 = N_QCOL * LANES
    tq = WINDOW_QBLOCKS * CHUNK
    return pl.pallas_call(
        _window_attn_kernel,
        grid=(b, n // tq),
        in_specs=[pl.BlockSpec((1, tq, w4), lambda bi, i: (bi, i, 0)),
                  pl.BlockSpec((1, n, LANES), lambda bi, i: (bi, 0, PAIR_K_BLK)),
                  pl.BlockSpec((1, LANES, n), lambda bi, i: (0, 0, bi)),
                  pl.BlockSpec((1,) + kx.shape[1:], lambda bi, i: (bi, 0, 0)),
                  pl.BlockSpec((1,) + vxt.shape[1:], lambda bi, i: (bi, 0, 0)),
                  pl.BlockSpec(memory_space=pltpu.SMEM),
                  pl.BlockSpec((1, tq, w4), lambda bi, i: (bi, i, 0))],
        out_specs=pl.BlockSpec((1, tq, w4), lambda bi, i: (bi, i, 0)),
        out_shape=jax.ShapeDtypeStruct((b, n, w4), BF16),
        scratch_shapes=[pltpu.VMEM((WINDOW_QBLOCKS, 3 * CHUNK + kx.shape[1], GQA_HEADS * CHUNK), F32)],
        compiler_params=pltpu.CompilerParams(vmem_limit_bytes=VMEM_LIMIT),
        name="attention_window",
    )(zq, zq, vt, kx, vxt, sink, zg)


def _retention_kernel(*refs, has_state_in, emit_state):
    it = iter(refs)
    q_ref, k_ref, v_ref, decf_ref, decb_ref, cg_ref, gate_ref = (next(it) for _ in range(7))
    sf_in = sb_in = sf_out = sb_out = None
    if has_state_in:
        sf_in, sb_in = next(it), next(it)
    y_ref = next(it)
    if emit_state:
        sf_out, sb_out = next(it), next(it)
    o_scr, u_scr, dm_scr = next(it), next(it), next(it)

    nc = q_ref.shape[1] // CHUNK
    hps = q_ref.shape[2] // LANES
    unroll_local = max(1, min(nc, RET_UNROLL_LOCAL // hps))
    unroll_finish = max(1, min(nc, RET_UNROLL_FINISH // hps))
    ri = lax.broadcasted_iota(jnp.int32, (CHUNK, CHUNK), 0).astype(F32)
    ci = lax.broadcasted_iota(jnp.int32, (CHUNK, CHUNK), 1).astype(F32)
    tok_col = lax.broadcasted_iota(jnp.int32, (CHUNK, 1), 0).astype(F32)
    tok_row = lax.broadcasted_iota(jnp.int32, (1, CHUNK), 1).astype(F32)
    rel_f, rel_b = ri - ci, ci - ri - 1.0

    cross_f, cross_b, kdec_f, kdec_b, cdec_f, cdec_b = [], [], [], [], [], []
    for h in range(hps):
        lg_f = -jnp.exp(decf_ref[h][:, :1])
        lg_b = -jnp.exp(decb_ref[h][:, :1])
        cross_f.append(jnp.exp(lg_f * (tok_col + 1.0)))
        cross_b.append(jnp.exp(lg_b * (CHUNK - 1.0 - tok_col)))
        kdec_f.append(jnp.exp(lg_f * (CHUNK - 1.0 - tok_row)))
        kdec_b.append(jnp.exp(lg_b * tok_row))
        cdec_f.append(jnp.exp(lg_f * CHUNK))
        cdec_b.append(jnp.exp(lg_b * CHUNK))
        dm_scr[h] = jnp.where(rel_f >= 0.0, jnp.exp(lg_f * jnp.maximum(rel_f, 0.0)),
                              jnp.exp(lg_b * jnp.maximum(rel_b, 0.0)))

    def chunk_rows(c):
        return pl.ds(pl.multiple_of(c * CHUNK, CHUNK), CHUNK)

    def head_cols(h):
        return slice(h * LANES, (h + 1) * LANES)

    def local_body(c, _):
        rows = chunk_rows(c)
        for h in range(hps):
            qh = q_ref[0, rows, head_cols(h)].astype(BF16)
            kf = k_ref[0, rows, head_cols(h)].astype(F32)
            vh = v_ref[0, rows, head_cols(h)].astype(BF16)
            kt = kf.T
            att = jnp.dot(qh, kt.astype(BF16), preferred_element_type=F32) * dm_scr[h]
            lhs = jnp.concatenate([att.astype(BF16), (kt * kdec_f[h]).astype(BF16), (kt * kdec_b[h]).astype(BF16)],
                                  axis=0)
            r = jnp.dot(lhs, vh, preferred_element_type=F32)
            o_scr[rows, head_cols(h)] = r[:CHUNK]
            u_scr[h, c, 0] = r[CHUNK:2 * CHUNK]
            u_scr[h, c, 1] = r[2 * CHUNK:]
        return 0
    lax.fori_loop(0, nc, local_body, 0, unroll=unroll_local)

    def scan_body(t, states):
        out = []
        for h in range(hps):
            s_f, s_b = states[2 * h], states[2 * h + 1]
            inc_f, inc_b = u_scr[h, t, 0], u_scr[h, nc - 1 - t, 1]
            u_scr[h, t, 0] = s_f
            u_scr[h, nc - 1 - t, 1] = s_b
            out += [cdec_f[h] * s_f + inc_f, cdec_b[h] * s_b + inc_b]
        return tuple(out)
    zero = jnp.zeros((CHUNK, CHUNK), F32)
    init = tuple(x for h in range(hps) for x in ((sf_in[0, h], sb_in[0, h]) if has_state_in else (zero, zero)))
    final = lax.fori_loop(0, nc, scan_body, init, unroll=2 if hps == 1 else 1)
    if emit_state:
        for h in range(hps):
            sf_out[0, h] = final[2 * h].astype(sf_out.dtype)
            sb_out[0, h] = final[2 * h + 1].astype(sb_out.dtype)

    def finish_body(c, _):
        rows = chunk_rows(c)
        for h in range(hps):
            qh = q_ref[0, rows, head_cols(h)].astype(BF16)
            states = jnp.concatenate([u_scr[h, c, 0], u_scr[h, c, 1]], axis=1).astype(BF16)
            r = jnp.dot(qh, states, preferred_element_type=F32)
            o = o_scr[rows, head_cols(h)] + cross_f[h] * r[:, :LANES] + cross_b[h] * r[:, LANES:]
            mu = jnp.mean(o, axis=-1, keepdims=True)
            d = o - mu
            var = jnp.mean(d * d, axis=-1, keepdims=True)
            y = d * lax.rsqrt(var + EPS) * cg_ref[h]
            y_ref[0, rows, head_cols(h)] = (y * gate_ref[0, rows, head_cols(h)]).astype(y_ref.dtype)
        return 0
    lax.fori_loop(0, nc, finish_body, 0, unroll=unroll_finish)


def _retention(zq, zg, dec_f, dec_b, c_norm_g, states, emit_state, hps):
    b, n, _ = zq.shape
    w = hps * LANES
    seq_spec = lambda blk: pl.BlockSpec((1, n, w), lambda bi, g: (bi, 0, blk // hps + g))
    st_spec = pl.BlockSpec((1, hps, CHUNK, CHUNK), lambda bi, g: (bi, g, 0, 0))
    head_spec = pl.BlockSpec((hps, 1, LANES), lambda bi, g: (g, 0, 0))
    bcast = lambda p: jnp.broadcast_to(p.astype(F32)[:, None, None], (C_HEADS, 1, LANES))
    in_specs = [seq_spec(0), seq_spec(C_HEADS), seq_spec(2 * C_HEADS), head_spec, head_spec, head_spec,
                seq_spec(0)]
    args = [zq, zq, zq, bcast(dec_f), bcast(dec_b), c_norm_g.astype(F32)[:, None, :], zg]
    if states is not None:
        in_specs += [st_spec, st_spec]
        args += list(states)
    out_specs = [seq_spec(0)]
    out_shape = [jax.ShapeDtypeStruct((b, n, C_HEADS * LANES), BF16)]
    if emit_state:
        out_specs += [st_spec, st_spec]
        out_shape += [jax.ShapeDtypeStruct((b, C_HEADS, CHUNK, CHUNK), F32)] * 2
    return pl.pallas_call(
        functools.partial(_retention_kernel, has_state_in=states is not None, emit_state=emit_state),
        grid=(b, C_HEADS // hps),
        in_specs=in_specs,
        out_specs=out_specs,
        out_shape=out_shape,
        scratch_shapes=[pltpu.VMEM((n, w), F32),
                        pltpu.VMEM((hps, n // CHUNK, 2, CHUNK, CHUNK), F32),
                        pltpu.VMEM((hps, CHUNK, CHUNK), F32)],
        compiler_params=pltpu.CompilerParams(vmem_limit_bytes=VMEM_LIMIT),
        name="retention",
    )(*args)


def _out_kernel(y1_ref, y2_ref, x_ref, mod_ref, w_ref, fg_ref, o_ref, *, final_norm):
    half = y1_ref.shape[1]
    y = (jnp.dot(y1_ref[...], w_ref[:half, :], preferred_element_type=F32)
         + jnp.dot(y2_ref[...], w_ref[half:, :], preferred_element_type=F32))
    x = x_ref[...] + mod_ref[0, 2:3, :] * y
    if final_norm:
        x = x * lax.rsqrt(jnp.mean(x * x, axis=-1, keepdims=True) + EPS) * fg_ref[...]
    o_ref[...] = x


def _out_project(y1, y2, x2d, mod, mod_row_of_tile, w_bf16, final_g, *, tm, final_norm):
    t, d = x2d.shape
    half = y1.shape[1]
    return pl.pallas_call(
        functools.partial(_out_kernel, final_norm=final_norm),
        grid=(t // tm,),
        in_specs=[pl.BlockSpec((tm, half), lambda i: (i, 0)),
                  pl.BlockSpec((tm, half), lambda i: (i, 0)),
                  pl.BlockSpec((tm, d), lambda i: (i, 0)),
                  pl.BlockSpec((1, 3, d), lambda i: (mod_row_of_tile(i), 0, 0)),
                  pl.BlockSpec(w_bf16.shape, lambda i: (0, 0)),
                  pl.BlockSpec((1, d), lambda i: (0, 0))],
        out_specs=pl.BlockSpec((tm, d), lambda i: (i, 0)),
        out_shape=jax.ShapeDtypeStruct((t, d), F32),
        compiler_params=pltpu.CompilerParams(vmem_limit_bytes=VMEM_LIMIT),
        name="out_project",
    )(y1, y2, x2d, mod, w_bf16, final_g.reshape(1, d))


def _pair_perm():
    return np.concatenate([np.arange(h * HEAD, (h + 1) * HEAD) for h in PAIR_ORDER])


def _reorder(w, idx, axis):
    idx = np.asarray(idx)
    runs = np.split(idx, np.flatnonzero(np.diff(idx) != 1) + 1)
    return jnp.concatenate([lax.slice_in_dim(w, int(r[0]), int(r[-1]) + 1, axis=axis) for r in runs], axis=axis)


def _rope_tables(n):
    rows = n // GRID_W
    row = jnp.repeat(jnp.arange(rows, dtype=F32), GRID_W)
    col = jnp.tile(jnp.arange(GRID_W, dtype=F32), rows)
    nf = HEAD // 4
    inv = ROPE_THETA ** (-jnp.arange(nf, dtype=F32) / nf)
    ang = jnp.concatenate([row[:, None] * inv, col[:, None] * inv], axis=-1)
    cos, sin = jnp.cos(ang), jnp.sin(ang)
    return jnp.tile(cos, (1, 4)), jnp.tile(jnp.concatenate([-sin, sin], axis=-1), (1, 2))


def _pair_kv(cache):
    b, g, p, d = cache.shape
    return cache.transpose(0, 2, 1, 3).reshape(b, p, g * d)


def _unpair_kv(z, blk, n_blk, heads):
    b, p, _ = z.shape
    t = z[:, :, blk * LANES:(blk + n_blk) * LANES]
    return t.reshape(b, p, heads, t.shape[-1] // heads).transpose(0, 2, 1, 3)


def kernel(x_prompt, x_sample, cache_a_k, cache_a_v, cache_b_k, cache_b_v, state_c_fwd, state_c_bwd, cache_d_k, cache_d_v, c, c_ctx, norm_g, mod_w, mod_b, ab_w_in, ab_w_out, a_sink, b_lq1, b_lk1, b_lq2, b_lk2, b_norm_g, cd_w_in, cd_w_out, c_decay_f, c_decay_b, c_norm_g, d_q_norm_g, d_k_norm_g, final_g):
    depth = norm_g.shape[0]
    bp, sp, d = x_prompt.shape
    bs, ss, _ = x_sample.shape
    dt = x_prompt.dtype

    ctx_row = bs
    pad = (-(bs + 1)) % 8
    cond = jnp.concatenate([c, c_ctx[None, :], jnp.zeros((pad, d), c.dtype)], axis=0)
    mod = _modulation(cond, mod_w, mod_b).reshape(depth, cond.shape[0], 3, d)

    perm = _pair_perm()
    qkv_w = N_QKV_BLK * LANES
    rope_tabs = _rope_tables(ss)
    bd = jnp.asarray(np.kron(np.eye(LANES // HEAD), np.ones((HEAD, HEAD))), BF16)

    tm_s = 1024
    tm_p = 512
    tiles_per_seq = ss // tm_s
    row_s = lambda i: i // tiles_per_seq
    row_p = lambda i: ctx_row

    xp = x_prompt.reshape(bp * sp, d)
    xs = x_sample.reshape(bs * ss, d)
    outs = {k: [] for k in ("a_k", "a_v", "b_k", "b_v", "c_f", "c_b", "d_k", "d_v")}

    pend_p = pend_s = None

    def project(x2d, pending, row_fn, tm, **kw):
        res = list(_project(x2d, mod[layer], row_fn, norm_g[layer], w_in, tm=tm, prev=pending, **kw))
        if pending is not None:
            x2d = res.pop()
        return x2d, res

    for layer in range(depth):
        i = layer // 2
        if layer % 2 == 0:
            lam_init = 0.8 - 0.6 * math.exp(-0.3 * layer)
            w4 = N_QCOL * LANES
            ar = np.arange
            cols = np.concatenate([perm, ar(w4 + 2 * LANES, 4 * w4 + 2 * LANES), ar(w4, w4 + 2 * LANES),
                                   qkv_w + perm, ar(qkv_w + w4, qkv_w + 2 * w4)])
            w_in = _reorder(ab_w_in[i].astype(BF16), cols, 1)
            rows = np.arange(ab_w_out.shape[1])
            rows[0:N_QCOL * LANES] = perm
            w_out = _reorder(ab_w_out[i].astype(BF16), rows, 0)
            diff_params = (b_lq1[i][None], b_lk1[i][None], b_lq2[i][None], b_lk2[i][None], b_norm_g[i][None])

            xp, (zq, zg) = project(xp, pend_p, row_p, tm_p, kind="ab", rope_tabs=None, norm_params=None,
                                   out_dtype=F32)
            zq3, zg3 = zq.reshape(bp, sp, -1), zg.reshape(bp, sp, -1)
            ya, yb = _ctx_attention_both(
                zq3, zg3,
                dict(q_blk=0, k_blk=PAIR_K_BLK, v_blk=PAIR_V_BLK, kv_per_col=False, gate_blk=0, mode="pair",
                     sink=a_sink[i]),
                dict(q_blk=4, k_blk=8, v_blk=12, kv_per_col=True, gate_blk=4, mode="diff",
                     diff_params=diff_params, lam_init=lam_init))
            pend_p = (ya.reshape(bp * sp, -1), yb.reshape(bp * sp, -1), mod[layer], w_out)
            outs["a_k"].append(_unpair_kv(zq3, PAIR_K_BLK, 1, 2))
            outs["a_v"].append(_unpair_kv(zq3, PAIR_V_BLK, 1, 2))
            outs["b_k"].append(_unpair_kv(zq3, 8, 4, 4))
            outs["b_v"].append(_unpair_kv(zq3, 12, 4, 4))

            xs, (zq, zg, vt) = project(xs, pend_s, row_s, tm_s, kind="ab", rope_tabs=rope_tabs, norm_params=None,
                                       out_dtype=BF16, vt_blocks=(PAIR_V_BLK, 12, 13, 14, 15))
            zq3, zg3 = zq.reshape(bs, ss, -1), zg.reshape(bs, ss, -1)
            ya = _window_attention(zq3, zg3, vt, _pair_kv(cache_a_k[:, i]),
                                   _pair_kv(cache_a_v[:, i]).transpose(0, 2, 1), a_sink[i])
            yb = _keymajor_attention(zq3, zg3, vt, cache_b_k[:, i], cache_b_v[:, i].transpose(0, 1, 3, 2),
                                     q_blk=4, k_blk=8, vt_blk=1, kv_per_col=True, gate_blk=4, mode="diff",
                                     tq=TQ_DENSE, tk=TK_DENSE, diff_params=diff_params, lam_init=lam_init)
            pend_s = (ya.reshape(bs * ss, -1), yb.reshape(bs * ss, -1), mod[layer], w_out)
        else:
            cols = np.arange(cd_w_in.shape[2])
            cols[12 * LANES:16 * LANES] = 12 * LANES + perm
            cols[qkv_w + N_QCOL * LANES:qkv_w + 2 * N_QCOL * LANES] = qkv_w + N_QCOL * LANES + perm
            w_in = _reorder(cd_w_in[i].astype(BF16), cols, 1)
            rows = np.arange(cd_w_out.shape[1])
            rows[N_QCOL * LANES:] = N_QCOL * LANES + perm
            w_out = _reorder(cd_w_out[i].astype(BF16), rows, 0)
            norm_params = (jnp.tile(d_q_norm_g[i], 2)[None], jnp.tile(d_k_norm_g[i], 2)[None], bd)

            xp, (zq, zg) = project(xp, pend_p, row_p, tm_p, kind="cd", rope_tabs=None, norm_params=norm_params,
                                   out_dtype=F32)
            zq3, zg3 = zq.reshape(bp, sp, -1), zg.reshape(bp, sp, -1)
            yc, s_f, s_b = _retention(zq3, zg3, c_decay_f[i], c_decay_b[i], c_norm_g[i], None, True,
                                      RET_HEADS_CTX)
            yd = _ctx_attention(zq3, zg3, q_blk=12, k_blk=PAIR_K_BLK, v_blk=PAIR_V_BLK, kv_per_col=False,
                                gate_blk=4, mode="pair")
            pend_p = (yc.reshape(bp * sp, -1), yd.reshape(bp * sp, -1), mod[layer], w_out)
            outs["c_f"].append(s_f.astype(dt))
            outs["c_b"].append(s_b.astype(dt))
            outs["d_k"].append(_unpair_kv(zq3, 16, 1, 2))
            outs["d_v"].append(_unpair_kv(zq3, 17, 1, 2))

            xs, (zq, zg, vt) = project(xs, pend_s, row_s, tm_s, kind="cd", rope_tabs=rope_tabs,
                                       norm_params=norm_params, out_dtype=BF16, vt_blocks=(PAIR_V_BLK,))
            zq3, zg3 = zq.reshape(bs, ss, -1), zg.reshape(bs, ss, -1)
            yc = _retention(zq3, zg3, c_decay_f[i], c_decay_b[i], c_norm_g[i],
                            (state_c_fwd[:, i], state_c_bwd[:, i]), False, RET_HEADS_LATENT)[0]
            yd = _keymajor_attention(zq3, zg3, vt, _pair_kv(cache_d_k[:, i]),
                                     _pair_kv(cache_d_v[:, i]).transpose(0, 2, 1),
                                     q_blk=12, k_blk=PAIR_K_BLK, vt_blk=0, kv_per_col=False, gate_blk=4,
                                     mode="pair",
                                     tq=TQ_DENSE, tk=TK_DENSE)
            pend_s = (yc.reshape(bs * ss, -1), yd.reshape(bs * ss, -1), mod[layer], w_out)

    xp = _out_project(pend_p[0], pend_p[1], xp, pend_p[2], row_p, pend_p[3], final_g, tm=tm_p, final_norm=True)
    xs = _out_project(pend_s[0], pend_s[1], xs, pend_s[2], row_s, pend_s[3], final_g, tm=tm_s, final_norm=True)

    stack = lambda k: jnp.stack(outs[k], axis=1)
    return (xp.reshape(bp, sp, d), xs.reshape(bs, ss, d), stack("a_k"), stack("a_v"), stack("b_k"), stack("b_v"),
            stack("c_f"), stack("c_b"), stack("d_k"), stack("d_v"))
```
